```python
import math
import jax, jax.numpy as jnp
from jax import lax
import numpy as np

D_MODEL = 2048
BATCH = 2
SEQ = 8192
DEPTH = 1

HEAD_DIM = 64
N_META = 16
GRID_W = 64
A_HEADS = 16
A_KV_HEADS = 4
A_WINDOW = 128
A_BLOCK = 128
N_BUCKETS = 32
MAX_DISTANCE = 128
B_HEADS = 16
NA_ROWS = 8
NA_COLS = 16
NA_COL_BLOCK = 16
NA_COL_BAND = 32
A_WIDTH = A_HEADS * HEAD_DIM
KV_WIDTH = A_KV_HEADS * HEAD_DIM
B_WIDTH = B_HEADS * HEAD_DIM
MIX_WIDTH = A_WIDTH + B_WIDTH
IN_WIDTH = A_WIDTH + 2 * KV_WIDTH + 3 * B_WIDTH
N_EXPERTS = 64
TOP_K = 8
N_GROUPS = 8
TOPK_GROUPS = 4
D_EXPERT = 512
D_SHARED = 512
ROUTED_SCALE = 2.5
MOE_BLOCK = 128
DEEPNORM_ALPHA = (2 * DEPTH) ** 0.25
DEEPNORM_BETA = (8 * DEPTH) ** -0.25
LN_EPS = 1e-5
RMS_EPS = 1e-6
NEG = -1e30

kernel_name = "hybrid_window_gqa_natten_moe_encoder"


def layer_norm(x, g, b):
    xf = x.astype(jnp.float32)
    mu = jnp.mean(xf, axis=-1, keepdims=True)
    xc = xf - mu
    var = jnp.mean(xc * xc, axis=-1, keepdims=True)
    y = xc * lax.rsqrt(var + LN_EPS) * g.astype(jnp.float32) + b.astype(jnp.float32)
    return y.astype(x.dtype)


def rms_norm(x, g):
    xf = x.astype(jnp.float32)
    y = xf * lax.rsqrt(jnp.mean(xf * xf, axis=-1, keepdims=True) + RMS_EPS) * g.astype(jnp.float32)
    return y.astype(x.dtype)


def t5_bucket(rel):
    nb = N_BUCKETS // 2
    max_exact = nb // 2
    ret = jnp.where(rel > 0, nb, 0)
    n = jnp.abs(rel)
    nf = jnp.maximum(n, 1).astype(jnp.float32)
    large = max_exact + (jnp.log(nf / max_exact) / math.log(MAX_DISTANCE / max_exact)
                         * (nb - max_exact)).astype(jnp.int32)
    large = jnp.minimum(large, nb - 1)
    return ret + jnp.where(n < max_exact, n, large)


def window_gqa(q, k, v, sink, t5_table):
    bsz, L = q.shape[0], q.shape[1]
    S = L - N_META
    nblk = S // A_BLOCK
    G = A_HEADS // A_KV_HEADS
    scale = HEAD_DIM ** -0.5
    qm, qr = q[:, :N_META], q[:, N_META:]
    km, kr = k[:, :N_META], k[:, N_META:]
    vm, vr = v[:, :N_META], v[:, N_META:]
    sink_f = sink.astype(jnp.float32).reshape(A_KV_HEADS, G)

    qb = qr.reshape(bsz, nblk, A_BLOCK, A_KV_HEADS, G, HEAD_DIM)
    pad = ((0, 0), (A_BLOCK, A_BLOCK), (0, 0), (0, 0))

    def band(t):
        t = jnp.pad(t, pad).reshape(bsz, nblk + 2, A_BLOCK, A_KV_HEADS, HEAD_DIM)
        return jnp.concatenate([t[:, :-2], t[:, 1:-1], t[:, 2:]], axis=2)

    kb, vb = band(kr), band(vr)
    q_loc = jnp.arange(A_BLOCK, dtype=jnp.int32)
    k_loc = jnp.arange(3 * A_BLOCK, dtype=jnp.int32) - A_BLOCK
    rel = k_loc[None, :] - q_loc[:, None]
    blk0 = jnp.arange(nblk, dtype=jnp.int32)[:, None, None] * A_BLOCK
    gk = blk0 + k_loc[None, None, :]
    mask = (jnp.abs(rel)[None] <= A_WINDOW) & (gk >= 0) & (gk < S)
    bias_band = jnp.moveaxis(t5_table.astype(jnp.float32)[t5_bucket(rel)], -1, 0)
    bias_band = bias_band.reshape(A_KV_HEADS, G, A_BLOCK, 3 * A_BLOCK)
    s_band = jnp.einsum('bnqhgd,bnkhd->bnhgqk', qb, kb).astype(jnp.float32) * scale + bias_band
    s_band = jnp.where(mask[None, :, None, None], s_band, NEG)

    q_pos = N_META + blk0[:, :, 0:1] + q_loc[None, :, None]
    rel_m = jnp.arange(N_META, dtype=jnp.int32)[None, None, :] - q_pos
    bias_m = jnp.moveaxis(t5_table.astype(jnp.float32)[t5_bucket(rel_m)], -1, 1)
    bias_m = bias_m.reshape(nblk, A_KV_HEADS, G, A_BLOCK, N_META)
    s_meta = jnp.einsum('bnqhgd,bmhd->bnhgqm', qb, km).astype(jnp.float32) * scale + bias_m
    s_sink = jnp.broadcast_to(sink_f[None, None, :, :, None, None],
                              (bsz, nblk, A_KV_HEADS, G, A_BLOCK, 1))
    p = jax.nn.softmax(jnp.concatenate([s_meta, s_band, s_sink], axis=-1), axis=-1)
    p_m = p[..., :N_META].astype(v.dtype)
    p_b = p[..., N_META:N_META + 3 * A_BLOCK].astype(v.dtype)
    o_real = (jnp.einsum('bnhgqm,bmhd->bnqhgd', p_m, vm)
              + jnp.einsum('bnhgqk,bnkhd->bnqhgd', p_b, vb))
    o_real = o_real.reshape(bsz, S, A_HEADS, HEAD_DIM)

    qmg = qm.reshape(bsz, N_META, A_KV_HEADS, G, HEAD_DIM)
    kk = jnp.concatenate([km, kr[:, :A_BLOCK]], axis=1)
    vv = jnp.concatenate([vm, vr[:, :A_BLOCK]], axis=1)
    k_pos = jnp.concatenate([jnp.arange(N_META, dtype=jnp.int32),
                             N_META + jnp.arange(A_BLOCK, dtype=jnp.int32)])
    relq = k_pos[None, :] - jnp.arange(N_META, dtype=jnp.int32)[:, None]
    bias_q = jnp.moveaxis(t5_table.astype(jnp.float32)[t5_bucket(relq)], -1, 0)
    bias_q = bias_q.reshape(A_KV_HEADS, G, N_META, N_META + A_BLOCK)
    s_q = jnp.einsum('bqhgd,bkhd->bhgqk', qmg, kk).astype(jnp.float32) * scale + bias_q
    s_q = jnp.where(jnp.abs(relq) <= A_WINDOW, s_q, NEG)
    s_qs = jnp.broadcast_to(sink_f[None, :, :, None, None], (bsz, A_KV_HEADS, G, N_META, 1))
    pq = jax.nn.softmax(jnp.concatenate([s_q, s_qs], axis=-1), axis=-1)[..., :-1].astype(v.dtype)
    o_meta = jnp.einsum('bhgqk,bkhd->bqhgd', pq, vv).reshape(bsz, N_META, A_HEADS, HEAD_DIM)
    return jnp.concatenate([o_meta, o_real], axis=1)


def neighbourhood_attention(q, k, v, rpb):
    bsz, L = q.shape[0], q.shape[1]
    S = L - N_META
    rows = S // GRID_W
    kh = min(NA_ROWS, rows)
    n_cb = GRID_W // NA_COL_BLOCK
    scale = HEAD_DIM ** -0.5
    qm, km, vm = q[:, :N_META], k[:, :N_META], v[:, :N_META]
    qg = q[:, N_META:].reshape(bsz, rows, GRID_W, B_HEADS, HEAD_DIM)
    kg = k[:, N_META:].reshape(bsz, rows, GRID_W, B_HEADS, HEAD_DIM)
    vg = v[:, N_META:].reshape(bsz, rows, GRID_W, B_HEADS, HEAD_DIM)

    cb_start = np.clip(np.arange(n_cb) * NA_COL_BLOCK - NA_COLS // 2, 0, GRID_W - NA_COL_BAND)
    col_idx = cb_start[:, None] + np.arange(NA_COL_BAND)
    qc = np.arange(GRID_W).reshape(n_cb, NA_COL_BLOCK)
    cs = np.clip(qc - NA_COLS // 2, 0, GRID_W - NA_COLS)[..., None]
    kc = col_idx[:, None, :]
    col_mask = jnp.asarray((kc >= cs) & (kc < cs + NA_COLS))
    dc = np.clip(kc - qc[..., None] + NA_COLS - 1, 0, 2 * NA_COLS - 2)
    rpb_cols = rpb.astype(jnp.float32)[:, :, dc]

    def row_fn(r):
        rs = jnp.clip(r - kh // 2, 0, rows - kh)
        kband = lax.dynamic_slice_in_dim(kg, rs, kh, axis=1)[:, :, col_idx]
        vband = lax.dynamic_slice_in_dim(vg, rs, kh, axis=1)[:, :, col_idx]
        qrow = lax.dynamic_index_in_dim(qg, r, axis=1, keepdims=False)
        qrow = qrow.reshape(bsz, n_cb, NA_COL_BLOCK, B_HEADS, HEAD_DIM)
        s = jnp.einsum('bjqhd,bijkhd->bhjqik', qrow, kband).astype(jnp.float32) * scale
        dr = rs + jnp.arange(kh, dtype=jnp.int32) - r + NA_ROWS - 1
        s = s + jnp.transpose(rpb_cols[:, dr], (0, 2, 3, 1, 4))
        s = jnp.where(col_mask[:, :, None, :], s, NEG)
        s = s.reshape(bsz, B_HEADS, n_cb, NA_COL_BLOCK, kh * NA_COL_BAND)
        s_m = jnp.einsum('bjqhd,bmhd->bhjqm', qrow, km).astype(jnp.float32) * scale
        p = jax.nn.softmax(jnp.concatenate([s_m, s], axis=-1), axis=-1)
        p_m = p[..., :N_META].astype(v.dtype)
        p_w = p[..., N_META:].reshape(bsz, B_HEADS, n_cb, NA_COL_BLOCK, kh, NA_COL_BAND).astype(v.dtype)
        o = (jnp.einsum('bhjqm,bmhd->bjqhd', p_m, vm)
             + jnp.einsum('bhjqik,bijkhd->bjqhd', p_w, vband))
        return o.reshape(bsz, GRID_W, B_HEADS, HEAD_DIM)

    o_real = lax.map(row_fn, jnp.arange(rows, dtype=jnp.int32))
    o_real = jnp.transpose(o_real, (1, 0, 2, 3, 4)).reshape(bsz, S, B_HEADS, HEAD_DIM)
    s_mm = jnp.einsum('bqhd,bmhd->bhqm', qm, km).astype(jnp.float32) * scale
    p_mm = jax.nn.softmax(s_mm, axis=-1).astype(v.dtype)
    o_meta = jnp.einsum('bhqm,bmhd->bqhd', p_mm, vm)
    return jnp.concatenate([o_meta, o_real], axis=1)


def mixer(h, w_in, a_sink, na_rpb, g_norm_a, g_norm_b, w_out, t5_table):
    bsz, L, _ = h.shape
    u = h @ w_in
    cuts = np.cumsum([A_WIDTH, KV_WIDTH, KV_WIDTH, B_WIDTH, B_WIDTH]).tolist()
    qa, ka, va, qb, kb, vb = jnp.split(u, cuts, axis=-1)
    ya = window_gqa(qa.reshape(bsz, L, A_HEADS, HEAD_DIM),
                    ka.reshape(bsz, L, A_KV_HEADS, HEAD_DIM),
                    va.reshape(bsz, L, A_KV_HEADS, HEAD_DIM), a_sink, t5_table)
    yb = neighbourhood_attention(qb.reshape(bsz, L, B_HEADS, HEAD_DIM),
                                 kb.reshape(bsz, L, B_HEADS, HEAD_DIM),
                                 vb.reshape(bsz, L, B_HEADS, HEAD_DIM), na_rpb)
    ya = rms_norm(ya.reshape(bsz, L, A_WIDTH), g_norm_a)
    yb = rms_norm(yb.reshape(bsz, L, B_WIDTH), g_norm_b)
    return jnp.concatenate([ya, yb], axis=-1) @ w_out


def moe(h, w_router, router_bias, w_gate, w_up, w_down, ws_gate, ws_up, ws_down):
    bsz, L, D = h.shape
    x = h.reshape(-1, D)
    N = x.shape[0]
    scores = jax.nn.sigmoid((x @ w_router).astype(jnp.float32))
    biased = scores + router_bias.astype(jnp.float32)
    grp = biased.reshape(N, N_GROUPS, N_EXPERTS // N_GROUPS)
    grp_score = lax.top_k(grp, 2)[0].sum(-1)
    _, grp_idx = lax.top_k(grp_score, TOPK_GROUPS)
    grp_mask = jnp.any(grp_idx[..., None] == jnp.arange(N_GROUPS)[None, None, :], axis=1)
    expert_mask = jnp.repeat(grp_mask, N_EXPERTS // N_GROUPS, axis=1)
    _, top_idx = lax.top_k(jnp.where(expert_mask, biased, -jnp.inf), TOP_K)
    top_w = jnp.take_along_axis(scores, top_idx, axis=1)
    top_w = top_w / (top_w.sum(-1, keepdims=True) + 1e-20) * ROUTED_SCALE

    NK = N * TOP_K
    flat_e = top_idx.reshape(-1)
    flat_tok = jnp.repeat(jnp.arange(N, dtype=jnp.int32), TOP_K)
    flat_w = top_w.reshape(-1)
    order = jnp.argsort(flat_e)
    se, stok, sw = flat_e[order], flat_tok[order], flat_w[order]
    counts = jnp.bincount(flat_e, length=N_EXPERTS)
    starts = jnp.cumsum(counts) - counts
    padded = (counts + MOE_BLOCK - 1) // MOE_BLOCK * MOE_BLOCK
    pend = jnp.cumsum(padded)
    pstart = pend - padded
    dest = pstart[se] + jnp.arange(NK, dtype=jnp.int32) - starts[se]
    n_blocks = -(-NK // MOE_BLOCK) + N_EXPERTS
    P = n_blocks * MOE_BLOCK
    buf_tok = jnp.full((P,), N, jnp.int32).at[dest].set(stok)
    buf_w = jnp.zeros((P,), jnp.float32).at[dest].set(sw)
    block_e = jnp.minimum(jnp.searchsorted(pend, jnp.arange(n_blocks) * MOE_BLOCK, side='right'),
                          N_EXPERTS - 1)
    x_pad = jnp.concatenate([x, jnp.zeros((1, D), x.dtype)], axis=0)

    def expert_block(acc, inp):
        e, tok, w = inp
        xb = x_pad[tok]
        hb = jax.nn.silu(xb @ w_gate[e]) * (xb @ w_up[e])
        yb = (hb @ w_down[e]) * w[:, None].astype(x.dtype)
        return acc.at[tok].add(yb), None

    acc, _ = lax.scan(expert_block, jnp.zeros((N + 1, D), x.dtype),
                      (block_e, buf_tok.reshape(n_blocks, MOE_BLOCK), buf_w.reshape(n_blocks, MOE_BLOCK)))
    shared = (jax.nn.silu(x @ ws_gate) * (x @ ws_up)) @ ws_down
    return (acc[:N] + shared).reshape(bsz, L, D)


def setup_inputs(seed: int = 0) -> dict:
    key = jax.random.key(seed)
    ks = jax.random.split(key, 24)
    f32 = jnp.float32
    nrm = lambda k, shape, s: jax.random.normal(k, shape, f32) * s
    gain = lambda k, shape: 1.0 + 0.02 * jax.random.normal(k, shape, f32)
    return {
        "x": jax.random.normal(ks[0], (BATCH, SEQ, D_MODEL), f32),
        "meta_tokens": nrm(ks[1], (N_META, D_MODEL), 1.0),
        "ln_emb_g": gain(ks[2], (D_MODEL,)),
        "ln_emb_b": nrm(ks[3], (D_MODEL,), 0.02),
        "t5_table": nrm(ks[4], (N_BUCKETS, A_HEADS), 0.1),
        "w_in": nrm(ks[5], (DEPTH, D_MODEL, IN_WIDTH), D_MODEL ** -0.5),
        "a_sink": nrm(ks[6], (DEPTH, A_HEADS), 0.1),
        "na_rpb": nrm(ks[7], (DEPTH, B_HEADS, 2 * NA_ROWS - 1, 2 * NA_COLS - 1), 0.1),
        "g_norm_a": gain(ks[8], (DEPTH, A_WIDTH)),
        "g_norm_b": gain(ks[9], (DEPTH, B_WIDTH)),
        "w_out": nrm(ks[10], (DEPTH, MIX_WIDTH, D_MODEL), MIX_WIDTH ** -0.5 * DEEPNORM_BETA),
        "ln1_g": gain(ks[11], (DEPTH, D_MODEL)),
        "ln1_b": nrm(ks[12], (DEPTH, D_MODEL), 0.02),
        "w_router": nrm(ks[13], (DEPTH, D_MODEL, N_EXPERTS), D_MODEL ** -0.5),
        "router_bias": nrm(ks[14], (DEPTH, N_EXPERTS), 0.01),
        "w_gate": nrm(ks[15], (DEPTH, N_EXPERTS, D_MODEL, D_EXPERT), D_MODEL ** -0.5),
        "w_up": nrm(ks[16], (DEPTH, N_EXPERTS, D_MODEL, D_EXPERT), D_MODEL ** -0.5),
        "w_down": nrm(ks[17], (DEPTH, N_EXPERTS, D_EXPERT, D_MODEL), D_EXPERT ** -0.5 * DEEPNORM_BETA),
        "ws_gate": nrm(ks[18], (DEPTH, D_MODEL, D_SHARED), D_MODEL ** -0.5),
        "ws_up": nrm(ks[19], (DEPTH, D_MODEL, D_SHARED), D_MODEL ** -0.5),
        "ws_down": nrm(ks[20], (DEPTH, D_SHARED, D_MODEL), D_SHARED ** -0.5 * DEEPNORM_BETA),
        "ln2_g": gain(ks[21], (DEPTH, D_MODEL)),
        "ln2_b": nrm(ks[22], (DEPTH, D_MODEL), 0.02),
    }


def reference(x, meta_tokens, ln_emb_g, ln_emb_b, t5_table, w_in, a_sink, na_rpb,
              g_norm_a, g_norm_b, w_out, ln1_g, ln1_b, w_router, router_bias,
              w_gate, w_up, w_down, ws_gate, ws_up, ws_down, ln2_g, ln2_b):
    bsz = x.shape[0]
    meta = jnp.broadcast_to(meta_tokens[None].astype(x.dtype), (bsz, N_META, x.shape[-1]))
    h = layer_norm(jnp.concatenate([meta, x], axis=1), ln_emb_g, ln_emb_b)
    for l in range(DEPTH):
        mix = mixer(h, w_in[l], a_sink[l], na_rpb[l], g_norm_a[l], g_norm_b[l], w_out[l], t5_table)
        h = layer_norm(DEEPNORM_ALPHA * h + mix, ln1_g[l], ln1_b[l])
        ffn = moe(h, w_router[l], router_bias[l], w_gate[l], w_up[l], w_down[l],
                  ws_gate[l], ws_up[l], ws_down[l])
        h = layer_norm(DEEPNORM_ALPHA * h + ffn, ln2_g[l], ln2_b[l])
    return h[:, N_META:]
```

```python
import functools
import math

import jax
import jax.numpy as jnp
from jax import lax
from jax.experimental import pallas as pl
from jax.experimental.pallas import tpu as pltpu

F32 = jnp.float32
BF16 = jnp.bfloat16

D_MODEL = 2048
HEAD_DIM = 64
N_META = 16
GRID_W = 64
A_HEADS = 16
A_KV_HEADS = 4
A_WINDOW = 128
A_BLOCK = 128
N_BUCKETS = 32
MAX_DISTANCE = 128
B_HEADS = 16
NA_ROWS = 8
NA_COLS = 16
A_WIDTH = A_HEADS * HEAD_DIM
KV_WIDTH = A_KV_HEADS * HEAD_DIM
B_WIDTH = B_HEADS * HEAD_DIM
MIX_WIDTH = A_WIDTH + B_WIDTH
IN_WIDTH = A_WIDTH + 2 * KV_WIDTH + 3 * B_WIDTH
N_EXPERTS = 64
TOP_K = 8
N_GROUPS = 8
GROUP_SIZE = N_EXPERTS // N_GROUPS
TOPK_GROUPS = 4
D_EXPERT = 512
D_SHARED = 512
ROUTED_SCALE = 2.5
DEPTH = 1
DEEPNORM_ALPHA = (2 * DEPTH) ** 0.25
LN_EPS = 1e-5
RMS_EPS = 1e-6
NEG = -1e30
SCALE = HEAD_DIM ** -0.5

LANES = 128
VMEM_LIMIT = 56 * 1024 * 1024

COL_QA = 0
COL_KA = A_WIDTH // LANES
COL_VA = (A_WIDTH + KV_WIDTH) // LANES
COL_QB = (A_WIDTH + 2 * KV_WIDTH) // LANES
COL_KB = COL_QB + B_WIDTH // LANES
COL_VB = COL_KB + B_WIDTH // LANES

MOE_BLOCK = 256


def _cparams(sem):
    return pltpu.CompilerParams(dimension_semantics=sem, vmem_limit_bytes=VMEM_LIMIT)


def _layer_norm(x, g, b):
    mu = jnp.mean(x, axis=-1, keepdims=True)
    xc = x - mu
    var = jnp.mean(xc * xc, axis=-1, keepdims=True)
    return xc * lax.rsqrt(var + LN_EPS) * g + b


def _dot(a, b):
    return jnp.dot(a, b, preferred_element_type=F32)


def _dot_nt(a, b):
    return lax.dot_general(a, b, (((1,), (1,)), ((), ())), preferred_element_type=F32)


def _silu(g):
    return g / (1.0 + jnp.exp(-g))


def _ln_inproj_kernel(x_ref, g_ref, b_ref, w_ref, o_ref, h_scr):
    @pl.when(pl.program_id(1) == 0)
    def _():
        h_scr[...] = _layer_norm(x_ref[...], g_ref[...], b_ref[...]).astype(BF16)

    o_ref[...] = _dot(h_scr[...], w_ref[...]).astype(o_ref.dtype)


def _ln_inproj(x2, g, b, w_bf16, tm, tn):
    m = x2.shape[0]
    n = w_bf16.shape[1]
    return pl.pallas_call(
        _ln_inproj_kernel,
        grid=(m // tm, n // tn),
        in_specs=[
            pl.BlockSpec((tm, D_MODEL), lambda i, j: (i, 0)),
            pl.BlockSpec((1, D_MODEL), lambda i, j: (0, 0)),
            pl.BlockSpec((1, D_MODEL), lambda i, j: (0, 0)),
            pl.BlockSpec((D_MODEL, tn), lambda i, j: (0, j)),
        ],
        out_specs=pl.BlockSpec((tm, tn), lambda i, j: (i, j)),
        out_shape=jax.ShapeDtypeStruct((m, n), BF16),
        scratch_shapes=[pltpu.VMEM((tm, D_MODEL), BF16)],
        compiler_params=_cparams(("parallel", "arbitrary")),
        name="ln_inproj",
    )(x2, g, b, w_bf16)


def _t5_bucket(rel):
    nb = N_BUCKETS // 2
    max_exact = nb // 2
    ret = jnp.where(rel > 0, nb, 0)
    n = jnp.abs(rel)
    nf = jnp.maximum(n, 1).astype(F32)
    large = max_exact + (jnp.log(nf / max_exact) / math.log(MAX_DISTANCE / max_exact)
                         * (nb - max_exact)).astype(jnp.int32)
    large = jnp.minimum(large, nb - 1)
    return ret + jnp.where(n < max_exact, n, large)


def _window_bias(t5_table, a_sink, seq):
    assert N_META + A_BLOCK - (N_META - 1) > MAX_DISTANCE
    nblk = seq // A_BLOCK
    t5 = t5_table.astype(F32)
    q_loc = jnp.arange(A_BLOCK, dtype=jnp.int32)
    k_loc = jnp.arange(3 * A_BLOCK, dtype=jnp.int32) - A_BLOCK
    rel = k_loc[None, :] - q_loc[:, None]
    band = jnp.moveaxis(t5[_t5_bucket(rel)], -1, 0)
    win = jnp.abs(rel) <= A_WINDOW
    sink = jnp.broadcast_to(a_sink.astype(F32)[:, None, None], (A_HEADS, A_BLOCK, 1))
    pad = jnp.full((A_HEADS, A_BLOCK, LANES - N_META - 1), NEG, F32)
    out = []
    for blk in (0, min(1, nblk - 1), nblk - 1):
        gk = blk * A_BLOCK + k_loc
        valid = win & (gk >= 0)[None, :] & (gk < seq)[None, :]
        band_v = jnp.where(valid[None], band, NEG)
        q_pos = N_META + blk * A_BLOCK + q_loc
        rel_m = jnp.arange(N_META, dtype=jnp.int32)[None, :] - q_pos[:, None]
        bias_m = jnp.moveaxis(t5[_t5_bucket(rel_m)], -1, 0)
        out.append(jnp.concatenate([bias_m, sink, pad, band_v], axis=-1))
    return jnp.stack(out, axis=0)


def _window_kernel(q_ref, kp_ref, kc_ref, kn_ref, vp_ref, vc_ref, vn_ref, km_ref, vm_ref,
                   bias_ref, o_ref):
    kall = jnp.concatenate([km_ref[...], kp_ref[0], kc_ref[0], kn_ref[0]], axis=0)
    vall = jnp.concatenate([vm_ref[...], vp_ref[0], vc_ref[0], vn_ref[0]], axis=0)
    lane = lax.broadcasted_iota(jnp.int32, (A_BLOCK, LANES), 1)
    in_lo = lane < HEAD_DIM
    group = A_HEADS // A_KV_HEADS
    pair_out = []
    for p in range(4):
        qp = q_ref[0, :, p * LANES:(p + 1) * LANES].astype(F32) * SCALE
        qr = pltpu.roll(qp, HEAD_DIM, axis=1)
        res = []
        for half in range(2):
            hl = 2 * p + half
            kv_half = hl // group
            src = qp if half == kv_half else qr
            keep = in_lo if kv_half == 0 else jnp.logical_not(in_lo)
            qm = jnp.where(keep, src, 0.0).astype(BF16)
            s = _dot_nt(qm, kall) + bias_ref[0, hl]
            m = jnp.max(s, axis=-1, keepdims=True)
            e = jnp.exp(s - m)
            l = jnp.sum(e, axis=-1, keepdims=True)
            o = _dot(e.astype(BF16), vall) / l
            if half != kv_half:
                o = pltpu.roll(o, HEAD_DIM, axis=1)
            res.append(o)
        pair_out.append(jnp.where(in_lo, res[0], res[1]))
    o_ref[0] = jnp.concatenate(pair_out, axis=1).astype(o_ref.dtype)


def _window_attention(u, kmeta, vmeta, bias, bsz, seq):
    nblk = seq // A_BLOCK
    n_pairs = A_KV_HEADS // 2
    qw = A_WIDTH // n_pairs

    def variant(n):
        return jnp.where(n == 0, 0, jnp.where(n == nblk - 1, 2, 1))

    def kv_spec(col0, shift):
        return pl.BlockSpec(
            (1, A_BLOCK, LANES),
            lambda b, j, n: (b, jnp.clip(n + shift, 0, nblk - 1), col0 + j))

    return pl.pallas_call(
        _window_kernel,
        grid=(bsz, n_pairs, nblk),
        in_specs=[
            pl.BlockSpec((1, A_BLOCK, qw), lambda b, j, n: (b, n, j)),
            kv_spec(COL_KA, -1), kv_spec(COL_KA, 0), kv_spec(COL_KA, 1),
            kv_spec(COL_VA, -1), kv_spec(COL_VA, 0), kv_spec(COL_VA, 1),
            pl.BlockSpec((LANES, LANES), lambda b, j, n: (0, j)),
            pl.BlockSpec((LANES, LANES), lambda b, j, n: (0, j)),
            pl.BlockSpec((1, A_HEADS // n_pairs, A_BLOCK, 4 * LANES),
                         lambda b, j, n: (variant(n), j, 0, 0)),
        ],
        out_specs=pl.BlockSpec((1, A_BLOCK, qw), lambda b, j, n: (b, n, j)),
        out_shape=jax.ShapeDtypeStruct((bsz, seq, A_WIDTH), BF16),
        compiler_params=_cparams(("parallel", "parallel", "arbitrary")),
        name="window_attention",
    )(u, u, u, u, u, u, u, kmeta, vmeta, bias)


def _na_bias(rpb):
    qc = jnp.arange(GRID_W, dtype=jnp.int32)
    kc = jnp.arange(GRID_W, dtype=jnp.int32)
    cs = jnp.clip(qc - NA_COLS // 2, 0, GRID_W - NA_COLS)
    cmask = (kc[None, :] >= cs[:, None]) & (kc[None, :] < cs[:, None] + NA_COLS)
    dc = jnp.clip(kc[None, :] - qc[:, None] + NA_COLS - 1, 0, 2 * NA_COLS - 2)
    t = jnp.where(cmask[None, None], rpb.astype(F32)[:, :, dc], NEG)
    variants = []
    for oi in range(NA_ROWS):
        variants.append(jnp.concatenate(
            [t[:, i - oi + NA_ROWS - 1] for i in range(NA_ROWS)], axis=-1))
    return jnp.stack(variants, axis=1)


def _na_kernel(q_ref, k_ref, v_ref, km_ref, vm_ref, bias_ref, o_ref, *, rows):
    span = NA_ROWS * GRID_W
    lane = lax.broadcasted_iota(jnp.int32, (GRID_W, LANES), 1)
    in_lo = lane < HEAD_DIM
    meta_bias = jnp.where(lane < N_META, 0.0, NEG).astype(F32)
    km = km_ref[...]
    vm = vm_ref[...]

    def body(r, carry):
        rs = jnp.clip(r - NA_ROWS // 2, 0, rows - NA_ROWS)
        oi = r - rs
        q = q_ref[0, pl.ds(pl.multiple_of(r * GRID_W, GRID_W), GRID_W), :]
        k0 = pl.multiple_of(rs * GRID_W, GRID_W)
        ks = k_ref[0, pl.ds(k0, span), :]
        vs = v_ref[0, pl.ds(k0, span), :]
        res = []
        for hl in range(2):
            keep = in_lo if hl == 0 else jnp.logical_not(in_lo)
            qm = jnp.where(keep, q, jnp.zeros_like(q))
            sw = _dot_nt(qm, ks) * SCALE + bias_ref[hl, oi]
            sm = _dot_nt(qm, km) * SCALE + meta_bias
            m = jnp.maximum(jnp.max(sw, axis=-1, keepdims=True),
                            jnp.max(sm, axis=-1, keepdims=True))
            ew = jnp.exp(sw - m)
            em = jnp.exp(sm - m)
            l = jnp.sum(ew, axis=-1, keepdims=True) + jnp.sum(em, axis=-1, keepdims=True)
            o = _dot(ew.astype(BF16), vs) + _dot(em.astype(BF16), vm)
            res.append(o / l)
        out = jnp.where(in_lo, res[0], res[1])
        o_ref[0, pl.ds(pl.multiple_of(r * GRID_W, GRID_W), GRID_W), :] = out.astype(o_ref.dtype)
        return carry

    lax.fori_loop(0, rows, body, 0)


def _neighbourhood_attention(u, kmeta, vmeta, bias, bsz, seq):
    rows = seq // GRID_W
    assert rows >= NA_ROWS
    n_pairs = B_HEADS // 2

    def col_spec(col0):
        return pl.BlockSpec((1, seq, LANES), lambda b, j: (b, 0, col0 + j))

    return pl.pallas_call(
        functools.partial(_na_kernel, rows=rows),
        grid=(bsz, n_pairs),
        in_specs=[
            col_spec(COL_QB), col_spec(COL_KB), col_spec(COL_VB),
            pl.BlockSpec((LANES, LANES), lambda b, j: (0, j)),
            pl.BlockSpec((LANES, LANES), lambda b, j: (0, j)),
            pl.BlockSpec((2, NA_ROWS, GRID_W, NA_ROWS * GRID_W), lambda b, j: (j, 0, 0, 0)),
        ],
        out_specs=pl.BlockSpec((1, seq, LANES), lambda b, j: (b, 0, j)),
        out_shape=jax.ShapeDtypeStruct((bsz, seq, B_WIDTH), BF16),
        compiler_params=_cparams(("parallel", "parallel")),
        name="neighbourhood_attention",
    )(u, u, u, kmeta, vmeta, bias)


def _route(scores, rbias, prefix_fn):
    t = scores.shape[-1]
    ninf = -jnp.inf
    biased = scores + rbias
    b3 = biased.reshape(N_GROUPS, GROUP_SIZE, t)
    s3 = scores.reshape(N_GROUPS, GROUP_SIZE, t)
    io_in = lax.broadcasted_iota(jnp.int32, b3.shape, 1)
    io_g3 = lax.broadcasted_iota(jnp.int32, b3.shape, 0)
    io_e = io_g3 * GROUP_SIZE + io_in
    m1 = jnp.max(b3, axis=1, keepdims=True)
    i1 = jnp.min(jnp.where(b3 == m1, io_in, GROUP_SIZE), axis=1, keepdims=True)
    m2 = jnp.max(jnp.where(io_in == i1, ninf, b3), axis=1, keepdims=True)
    gs = m1 + m2
    io_g = lax.broadcasted_iota(jnp.int32, gs.shape, 0)
    gmask = jnp.zeros(gs.shape, jnp.bool_)
    cur = gs
    for _ in range(TOPK_GROUPS):
        m = jnp.max(cur, axis=0, keepdims=True)
        i = jnp.min(jnp.where(cur == m, io_g, N_GROUPS), axis=0, keepdims=True)
        pick = io_g == i
        gmask = jnp.logical_or(gmask, pick)
        cur = jnp.where(pick, ninf, cur)
    cur = jnp.where(gmask, b3, ninf)
    picks, top_e, top_s = [], [], []
    for _ in range(TOP_K):
        m = jnp.max(jnp.max(cur, axis=1, keepdims=True), axis=0, keepdims=True)
        i = jnp.min(jnp.min(jnp.where(cur == m, io_e, N_EXPERTS), axis=1, keepdims=True),
                    axis=0, keepdims=True)
        pick = io_e == i
        picks.append(pick)
        top_e.append(i.reshape(1, t))
        w = jnp.sum(jnp.sum(jnp.where(pick, s3, 0.0), axis=1, keepdims=True), axis=0, keepdims=True)
        top_s.append(w.reshape(1, t))
        cur = jnp.where(pick, ninf, cur)
    top_e = jnp.concatenate(top_e, axis=0)
    top_s = jnp.concatenate(top_s, axis=0)
    denom = top_s[0:1]
    for k in range(1, TOP_K):
        denom = denom + top_s[k:k + 1]
    top_w = top_s / (denom + 1e-20) * ROUTED_SCALE
    sel = picks[0]
    for k in range(1, TOP_K):
        sel = jnp.logical_or(sel, picks[k])
    sel = jnp.where(sel, 1.0, 0.0).astype(F32).reshape(N_EXPERTS, t)
    return top_e, top_w, sel, picks


def _outproj_router_kernel(ya_ref, yb_ref, x_ref, wo_ref, ga_ref, gb_ref, eg_ref, eb_ref,
                           g1_ref, b1_ref, wrh_ref, wrl_ref, rb_ref,
                           h1_ref, h1b_ref, te_ref, tw_ref, tr_ref, cnt_ref, carry):
    tm = x_ref.shape[0]

    @pl.when(pl.program_id(0) == 0)
    def _():
        carry[...] = jnp.zeros_like(carry)

    def rms(y_ref, g_ref):
        y = y_ref[...].astype(F32)
        inv = lax.rsqrt(jnp.mean(y * y, axis=-1, keepdims=True) + RMS_EPS)
        return (y * inv * g_ref[...]).astype(BF16)

    mix = (_dot(rms(ya_ref, ga_ref), wo_ref[0:A_WIDTH, :])
           + _dot(rms(yb_ref, gb_ref), wo_ref[A_WIDTH:MIX_WIDTH, :]))
    h = _layer_norm(x_ref[...], eg_ref[...], eb_ref[...])
    h1 = _layer_norm(DEEPNORM_ALPHA * h + mix, g1_ref[...], b1_ref[...])
    h1_ref[...] = h1
    hb = h1.astype(BF16)
    h1b_ref[...] = hb

    hlo = (h1 - hb.astype(F32)).astype(BF16)
    logits = (_dot_nt(wrh_ref[...], hb) + _dot_nt(wrh_ref[...], hlo)
              + _dot_nt(wrl_ref[...], hb))
    scores = 1.0 / (1.0 + jnp.exp(-logits))
    top_e, top_w, sel, picks = _route(scores, rb_ref[...], None)

    row = lax.broadcasted_iota(jnp.int32, (tm, tm), 0)
    col = lax.broadcasted_iota(jnp.int32, (tm, tm), 1)
    before = jnp.where(row < col, 1.0, 0.0).astype(BF16)
    rank = _dot(sel.astype(BF16), before) + carry[...]
    rank3 = rank.reshape(N_GROUPS, GROUP_SIZE, tm)
    ranks = []
    for k in range(TOP_K):
        rk = jnp.sum(jnp.sum(jnp.where(picks[k], rank3, 0.0), axis=1, keepdims=True),
                     axis=0, keepdims=True)
        ranks.append(rk.reshape(1, tm))
    te_ref[...] = top_e
    tw_ref[...] = top_w
    tr_ref[...] = jnp.concatenate(ranks, axis=0).astype(jnp.int32)
    carry[...] = carry[...] + jnp.sum(sel, axis=-1, keepdims=True)
    cnt_ref[...] = jnp.broadcast_to(carry[...], cnt_ref.shape)


def _outproj_router(ya, yb, x2, wo_bf16, ga, gb, eg, eb, g1, b1, wrh, wrl, rb, tm):
    m = x2.shape[0]
    row = lambda w: pl.BlockSpec((tm, w), lambda i: (i, 0))
    full = lambda a: pl.BlockSpec(a.shape, lambda i: (0,) * a.ndim)
    tok = pl.BlockSpec((TOP_K, tm), lambda i: (0, i))
    return pl.pallas_call(
        _outproj_router_kernel,
        grid=(m // tm,),
        in_specs=[row(A_WIDTH), row(B_WIDTH), row(D_MODEL), full(wo_bf16), full(ga), full(gb),
                  full(eg), full(eb), full(g1), full(b1), full(wrh), full(wrl), full(rb)],
        out_specs=[row(D_MODEL), row(D_MODEL), tok, tok, tok,
                   pl.BlockSpec((N_EXPERTS, LANES), lambda i: (0, 0))],
        out_shape=[jax.ShapeDtypeStruct((m, D_MODEL), F32),
                   jax.ShapeDtypeStruct((m, D_MODEL), BF16),
                   jax.ShapeDtypeStruct((TOP_K, m), jnp.int32),
                   jax.ShapeDtypeStruct((TOP_K, m), F32),
                   jax.ShapeDtypeStruct((TOP_K, m), jnp.int32),
                   jax.ShapeDtypeStruct((N_EXPERTS, LANES), F32)],
        scratch_shapes=[pltpu.VMEM((N_EXPERTS, 1), F32)],
        compiler_params=_cparams(("arbitrary",)),
        name="outproj_router",
    )(ya, yb, x2, wo_bf16, ga, gb, eg, eb, g1, b1, wrh, wrl, rb)


def _moe_kernel(be_ref, nu_ref, x_ref, wg_ref, wu_ref, wd_ref, y_ref, wg_b, wu_b, wd_b):
    i = pl.program_id(0)
    e = be_ref[i]
    prev = be_ref[jnp.maximum(i - 1, 0)]

    @pl.when(jnp.logical_or(i == 0, e != prev))
    def _():
        wg_b[...] = wg_ref[0].astype(BF16)
        wu_b[...] = wu_ref[0].astype(BF16)
        wd_b[...] = wd_ref[0].astype(BF16)

    @pl.when(i < nu_ref[0])
    def _():
        x = x_ref[...]
        g = _dot(x, wg_b[...])
        u = _dot(x, wu_b[...])
        y_ref[...] = _dot((_silu(g) * u).astype(BF16), wd_b[...]).astype(y_ref.dtype)


def _moe_experts(block_e, n_used, x_sorted, w_gate, w_up, w_down):
    p = x_sorted.shape[0]
    n_blocks = p // MOE_BLOCK
    last = lambda i, be, nu: jnp.minimum(i, nu[0] - 1)
    return pl.pallas_call(
        _moe_kernel,
        grid_spec=pltpu.PrefetchScalarGridSpec(
            num_scalar_prefetch=2,
            grid=(n_blocks,),
            in_specs=[
                pl.BlockSpec((MOE_BLOCK, D_MODEL), lambda i, be, nu: (last(i, be, nu), 0)),
                pl.BlockSpec((1, D_MODEL, D_EXPERT), lambda i, be, nu: (be[i], 0, 0)),
                pl.BlockSpec((1, D_MODEL, D_EXPERT), lambda i, be, nu: (be[i], 0, 0)),
                pl.BlockSpec((1, D_EXPERT, D_MODEL), lambda i, be, nu: (be[i], 0, 0)),
            ],
            out_specs=pl.BlockSpec((MOE_BLOCK, D_MODEL), lambda i, be, nu: (last(i, be, nu), 0)),
            scratch_shapes=[pltpu.VMEM((D_MODEL, D_EXPERT), BF16),
                            pltpu.VMEM((D_MODEL, D_EXPERT), BF16),
                            pltpu.VMEM((D_EXPERT, D_MODEL), BF16)],
        ),
        out_shape=jax.ShapeDtypeStruct((p, D_MODEL), BF16),
        compiler_params=_cparams(("arbitrary",)),
        name="moe_experts",
    )(block_e, n_used, x_sorted, w_gate, w_up, w_down)


def _final_kernel(h1_ref, h1b_ref, ys_ref, tw_ref, wsg_ref, wsu_ref, wsd_ref, g2_ref, b2_ref,
                  o_ref):
    hb = h1b_ref[...]
    g = _dot(hb, wsg_ref[...])
    u = _dot(hb, wsu_ref[...])
    ffn = _dot((_silu(g) * u).astype(BF16), wsd_ref[...])
    for k in range(TOP_K):
        ffn = ffn + ys_ref[k].astype(F32) * tw_ref[:, k:k + 1]
    o_ref[...] = _layer_norm(DEEPNORM_ALPHA * h1_ref[...] + ffn, g2_ref[...], b2_ref[...])


def _final(h1, h1b, ysel, tw_t, wsg, wsu, wsd, g2, b2, tm):
    m = h1.shape[0]
    row = lambda w: pl.BlockSpec((tm, w), lambda i: (i, 0))
    full = lambda a: pl.BlockSpec(a.shape, lambda i: (0,) * a.ndim)
    return pl.pallas_call(
        _final_kernel,
        grid=(m // tm,),
        in_specs=[row(D_MODEL), row(D_MODEL),
                  pl.BlockSpec((TOP_K, tm, D_MODEL), lambda i: (0, i, 0)),
                  row(TOP_K), full(wsg), full(wsu), full(wsd), full(g2), full(b2)],
        out_specs=row(D_MODEL),
        out_shape=jax.ShapeDtypeStruct((m, D_MODEL), F32),
        compiler_params=_cparams(("parallel",)),
        name="shared_combine_ln2",
    )(h1, h1b, ysel, tw_t, wsg, wsu, wsd, g2, b2)


def _pad_rows(a, rows):
    return jnp.concatenate([a, jnp.zeros((rows - a.shape[0],) + a.shape[1:], a.dtype)], axis=0)


def kernel(x, meta_tokens, ln_emb_g, ln_emb_b, t5_table, w_in, a_sink, na_rpb, g_norm_a, g_norm_b, w_out, ln1_g, ln1_b, w_router, router_bias, w_gate, w_up, w_down, ws_gate, ws_up, ws_down, ln2_g, ln2_b):
    bsz, seq, _ = x.shape
    m = bsz * seq
    r2 = lambda a: a.reshape(1, -1).astype(F32)
    x2 = x.reshape(m, D_MODEL)
    eg, eb = r2(ln_emb_g), r2(ln_emb_b)

    w_in_b = w_in[0].astype(BF16)
    tm = 512 if m % 512 == 0 else 128
    u = _ln_inproj(x2, eg, eb, w_in_b, tm, 1152).reshape(bsz, seq, IN_WIDTH)
    um = _ln_inproj(meta_tokens.astype(F32), eg, eb, w_in_b, N_META, 1152)
    cut = lambda c0, width: _pad_rows(um[:, c0 * LANES:c0 * LANES + width], LANES)

    ya = _window_attention(u, cut(COL_KA, KV_WIDTH), cut(COL_VA, KV_WIDTH),
                           _window_bias(t5_table, a_sink[0], seq), bsz, seq)
    yb = _neighbourhood_attention(u, cut(COL_KB, B_WIDTH), cut(COL_VB, B_WIDTH),
                                  _na_bias(na_rpb[0]), bsz, seq)

    wr_t = w_router[0].astype(F32).T
    wr_hi = wr_t.astype(BF16)
    wr_lo = (wr_t - wr_hi.astype(F32)).astype(BF16)
    tm4 = 256 if m % 256 == 0 else 128
    h1, h1b, top_e, top_w, top_r, cnt = _outproj_router(
        ya.reshape(m, A_WIDTH), yb.reshape(m, B_WIDTH), x2, w_out[0].astype(BF16),
        r2(g_norm_a), r2(g_norm_b), eg, eb, r2(ln1_g), r2(ln1_b), wr_hi, wr_lo,
        router_bias[0].astype(F32).reshape(N_EXPERTS, 1), tm4)

    counts = cnt[:, 0].astype(jnp.int32)
    nb_e = (counts + MOE_BLOCK - 1) // MOE_BLOCK
    bend = jnp.cumsum(nb_e)
    pstart = (bend - nb_e) * MOE_BLOCK
    n_blocks = (m * TOP_K) // MOE_BLOCK + N_EXPERTS
    p_rows = n_blocks * MOE_BLOCK
    block_e = jnp.minimum(jnp.searchsorted(bend, jnp.arange(n_blocks, dtype=jnp.int32),
                                           side='right'), N_EXPERTS - 1).astype(jnp.int32)
    n_used = bend[-1:].astype(jnp.int32)
    dest = pstart[top_e] + top_r
    tok = jnp.broadcast_to(jnp.arange(m, dtype=jnp.int32)[None, :], (TOP_K, m))
    slot_tok = jnp.zeros((p_rows,), jnp.int32).at[dest.reshape(-1)].set(tok.reshape(-1))
    x_sorted = jnp.take(h1b, slot_tok, axis=0)

    y_sorted = _moe_experts(block_e, n_used, x_sorted, w_gate[0], w_up[0], w_down[0])
    ysel = jnp.take(y_sorted, dest.reshape(-1), axis=0).reshape(TOP_K, m, D_MODEL)

    out = _final(h1, h1b, ysel, top_w.T, ws_gate[0].astype(BF16), ws_up[0].astype(BF16),
                 ws_down[0].astype(BF16), r2(ln2_g), r2(ln2_b), tm4)
    return out.reshape(bsz, seq, D_MODEL)
```

```python
import functools
import math

import jax
import jax.numpy as jnp
from jax import lax
from jax.experimental import pallas as pl
from jax.experimental.pallas import tpu as pltpu

F32 = jnp.float32
BF16 = jnp.bfloat16

D_MODEL = 2048
HEAD_DIM = 64
N_META = 16
GRID_W = 64
A_HEADS = 16
A_KV_HEADS = 4
A_WINDOW = 128
A_BLOCK = 128
N_BUCKETS = 32
MAX_DISTANCE = 128
B_HEADS = 16
NA_ROWS = 8
NA_COLS = 16
A_WIDTH = A_HEADS * HEAD_DIM
KV_WIDTH = A_KV_HEADS * HEAD_DIM
B_WIDTH = B_HEADS * HEAD_DIM
MIX_WIDTH = A_WIDTH + B_WIDTH
IN_WIDTH = A_WIDTH + 2 * KV_WIDTH + 3 * B_WIDTH
N_EXPERTS = 64
TOP_K = 8
N_GROUPS = 8
GROUP_SIZE = N_EXPERTS // N_GROUPS
TOPK_GROUPS = 4
D_EXPERT = 512
D_SHARED = 512
ROUTED_SCALE = 2.5
DEPTH = 1
DEEPNORM_ALPHA = (2 * DEPTH) ** 0.25
LN_EPS = 1e-5
RMS_EPS = 1e-6
NEG = -1e30
SCALE = HEAD_DIM ** -0.5

LANES = 128
VMEM_LIMIT = 56 * 1024 * 1024

COL_QA = 0
COL_KA = A_WIDTH // LANES
COL_VA = (A_WIDTH + KV_WIDTH) // LANES
COL_QB = (A_WIDTH + 2 * KV_WIDTH) // LANES
COL_KB = COL_QB + B_WIDTH // LANES
COL_VB = COL_KB + B_WIDTH // LANES

MOE_BLOCK = 256


def _cparams(sem):
    return pltpu.CompilerParams(dimension_semantics=sem, vmem_limit_bytes=VMEM_LIMIT)


def _layer_norm(x, g, b):
    mu = jnp.mean(x, axis=-1, keepdims=True)
    xc = x - mu
    var = jnp.mean(xc * xc, axis=-1, keepdims=True)
    return xc * lax.rsqrt(var + LN_EPS) * g + b


def _dot(a, b):
    return jnp.dot(a, b, preferred_element_type=F32)


def _dot_nt(a, b):
    return lax.dot_general(a, b, (((1,), (1,)), ((), ())), preferred_element_type=F32)


def _silu(g):
    return g / (1.0 + jnp.exp(-g))


def _ln_inproj_kernel(x_ref, g_ref, b_ref, w_ref, o_ref, h_scr):
    @pl.when(pl.program_id(1) == 0)
    def _():
        h_scr[...] = _layer_norm(x_ref[...], g_ref[...], b_ref[...]).astype(BF16)

    o_ref[...] = _dot(h_scr[...], w_ref[...]).astype(o_ref.dtype)


def _ln_inproj(x2, g, b, w_bf16, tm, tn):
    m = x2.shape[0]
    n = w_bf16.shape[1]
    return pl.pallas_call(
        _ln_inproj_kernel,
        grid=(m // tm, n // tn),
        in_specs=[
            pl.BlockSpec((tm, D_MODEL), lambda i, j: (i, 0)),
            pl.BlockSpec((1, D_MODEL), lambda i, j: (0, 0)),
            pl.BlockSpec((1, D_MODEL), lambda i, j: (0, 0)),
            pl.BlockSpec((D_MODEL, tn), lambda i, j: (0, j)),
        ],
        out_specs=pl.BlockSpec((tm, tn), lambda i, j: (i, j)),
        out_shape=jax.ShapeDtypeStruct((m, n), BF16),
        scratch_shapes=[pltpu.VMEM((tm, D_MODEL), BF16)],
        compiler_params=_cparams(("parallel", "arbitrary")),
        name="ln_inproj",
    )(x2, g, b, w_bf16)


def _t5_bucket(rel):
    nb = N_BUCKETS // 2
    max_exact = nb // 2
    ret = jnp.where(rel > 0, nb, 0)
    n = jnp.abs(rel)
    nf = jnp.maximum(n, 1).astype(F32)
    large = max_exact + (jnp.log(nf / max_exact) / math.log(MAX_DISTANCE / max_exact)
                         * (nb - max_exact)).astype(jnp.int32)
    large = jnp.minimum(large, nb - 1)
    return ret + jnp.where(n < max_exact, n, large)


def _lookup(table_t, idx, n):
    onehot = (idx[None] == jnp.arange(n, dtype=jnp.int32).reshape((n,) + (1,) * idx.ndim))
    return jnp.einsum('hb,b...->h...', table_t, onehot.astype(F32),
                      precision=lax.Precision.HIGHEST)


def _window_bias(t5_table, a_sink, seq):
    assert N_META + A_BLOCK - (N_META - 1) > MAX_DISTANCE
    nblk = seq // A_BLOCK
    t5_t = t5_table.astype(F32).T
    q_loc = jnp.arange(A_BLOCK, dtype=jnp.int32)
    k_loc = jnp.arange(3 * A_BLOCK, dtype=jnp.int32) - A_BLOCK
    rel = k_loc[None, :] - q_loc[:, None]
    band = _lookup(t5_t, _t5_bucket(rel), N_BUCKETS)
    win = jnp.abs(rel) <= A_WINDOW
    sink = jnp.broadcast_to(a_sink.astype(F32)[:, None, None], (A_HEADS, A_BLOCK, 1))
    pad = jnp.full((A_HEADS, A_BLOCK, LANES - N_META - 1), NEG, F32)
    out = []
    for blk in (0, min(1, nblk - 1), nblk - 1):
        gk = blk * A_BLOCK + k_loc
        valid = win & (gk >= 0)[None, :] & (gk < seq)[None, :]
        band_v = jnp.where(valid[None], band, NEG)
        q_pos = N_META + blk * A_BLOCK + q_loc
        rel_m = jnp.arange(N_META, dtype=jnp.int32)[None, :] - q_pos[:, None]
        bias_m = _lookup(t5_t, _t5_bucket(rel_m), N_BUCKETS)
        out.append(jnp.concatenate([bias_m, sink, pad, band_v], axis=-1))
    return jnp.stack(out, axis=0)


def _window_kernel(q_ref, kp_ref, kc_ref, kn_ref, vp_ref, vc_ref, vn_ref, km_ref, vm_ref,
                   bias_ref, o_ref):
    kall = jnp.concatenate([km_ref[...], kp_ref[0], kc_ref[0], kn_ref[0]], axis=0)
    vall = jnp.concatenate([vm_ref[...], vp_ref[0], vc_ref[0], vn_ref[0]], axis=0)
    lane = lax.broadcasted_iota(jnp.int32, (A_BLOCK, LANES), 1)
    in_lo = lane < HEAD_DIM
    group = A_HEADS // A_KV_HEADS
    pair_out = []
    for p in range(4):
        qp = q_ref[0, :, p * LANES:(p + 1) * LANES].astype(F32) * SCALE
        qr = pltpu.roll(qp, HEAD_DIM, axis=1)
        res = []
        for half in range(2):
            hl = 2 * p + half
            kv_half = hl // group
            src = qp if half == kv_half else qr
            keep = in_lo if kv_half == 0 else jnp.logical_not(in_lo)
            qm = jnp.where(keep, src, 0.0).astype(BF16)
            s = _dot_nt(qm, kall) + bias_ref[0, hl]
            m = jnp.max(s, axis=-1, keepdims=True)
            e = jnp.exp(s - m)
            l = jnp.sum(e, axis=-1, keepdims=True)
            o = _dot(e.astype(BF16), vall) / l
            if half != kv_half:
                o = pltpu.roll(o, HEAD_DIM, axis=1)
            res.append(o)
        pair_out.append(jnp.where(in_lo, res[0], res[1]))
    o_ref[0] = jnp.concatenate(pair_out, axis=1).astype(o_ref.dtype)


def _window_attention(u, kmeta, vmeta, bias, bsz, seq):
    nblk = seq // A_BLOCK
    n_pairs = A_KV_HEADS // 2
    qw = A_WIDTH // n_pairs

    def variant(n):
        return jnp.where(n == 0, 0, jnp.where(n == nblk - 1, 2, 1))

    def kv_spec(col0, shift):
        return pl.BlockSpec(
            (1, A_BLOCK, LANES),
            lambda b, j, n: (b, jnp.clip(n + shift, 0, nblk - 1), col0 + j))

    return pl.pallas_call(
        _window_kernel,
        grid=(bsz, n_pairs, nblk),
        in_specs=[
            pl.BlockSpec((1, A_BLOCK, qw), lambda b, j, n: (b, n, j)),
            kv_spec(COL_KA, -1), kv_spec(COL_KA, 0), kv_spec(COL_KA, 1),
            kv_spec(COL_VA, -1), kv_spec(COL_VA, 0), kv_spec(COL_VA, 1),
            pl.BlockSpec((LANES, LANES), lambda b, j, n: (0, j)),
            pl.BlockSpec((LANES, LANES), lambda b, j, n: (0, j)),
            pl.BlockSpec((1, A_HEADS // n_pairs, A_BLOCK, 4 * LANES),
                         lambda b, j, n: (variant(n), j, 0, 0)),
        ],
        out_specs=pl.BlockSpec((1, A_BLOCK, qw), lambda b, j, n: (b, n, j)),
        out_shape=jax.ShapeDtypeStruct((bsz, seq, A_WIDTH), BF16),
        compiler_params=_cparams(("parallel", "parallel", "arbitrary")),
        name="window_attention",
    )(u, u, u, u, u, u, u, kmeta, vmeta, bias)


def _na_bias(rpb):
    qc = jnp.arange(GRID_W, dtype=jnp.int32)
    kc = jnp.arange(GRID_W, dtype=jnp.int32)
    cs = jnp.clip(qc - NA_COLS // 2, 0, GRID_W - NA_COLS)
    cmask = (kc[None, :] >= cs[:, None]) & (kc[None, :] < cs[:, None] + NA_COLS)
    dc = jnp.clip(kc[None, :] - qc[:, None] + NA_COLS - 1, 0, 2 * NA_COLS - 2)
    n_dc = 2 * NA_COLS - 1
    onehot = (dc[None] == jnp.arange(n_dc, dtype=jnp.int32)[:, None, None]).astype(F32)
    t = jnp.einsum('hrd,dqk->hrqk', rpb.astype(F32), onehot, precision=lax.Precision.HIGHEST)
    t = jnp.where(cmask[None, None], t, NEG)
    variants = []
    for oi in range(NA_ROWS):
        variants.append(jnp.concatenate(
            [t[:, i - oi + NA_ROWS - 1] for i in range(NA_ROWS)], axis=-1))
    return jnp.stack(variants, axis=1)


def _na_kernel(q_ref, k_ref, v_ref, km_ref, vm_ref, bias_ref, o_ref, *, rows):
    span = NA_ROWS * GRID_W
    lane = lax.broadcasted_iota(jnp.int32, (GRID_W, LANES), 1)
    in_lo = lane < HEAD_DIM
    meta_bias = jnp.where(lane < N_META, 0.0, NEG).astype(F32)
    km = km_ref[...]
    vm = vm_ref[...]

    def body(r, carry):
        rs = jnp.clip(r - NA_ROWS // 2, 0, rows - NA_ROWS)
        oi = r - rs
        q = q_ref[0, pl.ds(pl.multiple_of(r * GRID_W, GRID_W), GRID_W), :]
        k0 = pl.multiple_of(rs * GRID_W, GRID_W)
        ks = k_ref[0, pl.ds(k0, span), :]
        vs = v_ref[0, pl.ds(k0, span), :]
        res = []
        for hl in range(2):
            keep = in_lo if hl == 0 else jnp.logical_not(in_lo)
            qm = jnp.where(keep, q, jnp.zeros_like(q))
            sw = _dot_nt(qm, ks) * SCALE + bias_ref[hl, oi]
            sm = _dot_nt(qm, km) * SCALE + meta_bias
            m = jnp.maximum(jnp.max(sw, axis=-1, keepdims=True),
                            jnp.max(sm, axis=-1, keepdims=True))
            ew = jnp.exp(sw - m)
            em = jnp.exp(sm - m)
            l = jnp.sum(ew, axis=-1, keepdims=True) + jnp.sum(em, axis=-1, keepdims=True)
            o = _dot(ew.astype(BF16), vs) + _dot(em.astype(BF16), vm)
            res.append(o / l)
        out = jnp.where(in_lo, res[0], res[1])
        o_ref[0, pl.ds(pl.multiple_of(r * GRID_W, GRID_W), GRID_W), :] = out.astype(o_ref.dtype)
        return carry

    lax.fori_loop(0, rows, body, 0)


def _neighbourhood_attention(u, kmeta, vmeta, bias, bsz, seq):
    rows = seq // GRID_W
    assert rows >= NA_ROWS
    n_pairs = B_HEADS // 2

    def col_spec(col0):
        return pl.BlockSpec((1, seq, LANES), lambda b, j: (b, 0, col0 + j))

    return pl.pallas_call(
        functools.partial(_na_kernel, rows=rows),
        grid=(bsz, n_pairs),
        in_specs=[
            col_spec(COL_QB), col_spec(COL_KB), col_spec(COL_VB),
            pl.BlockSpec((LANES, LANES), lambda b, j: (0, j)),
            pl.BlockSpec((LANES, LANES), lambda b, j: (0, j)),
            pl.BlockSpec((2, NA_ROWS, GRID_W, NA_ROWS * GRID_W), lambda b, j: (j, 0, 0, 0)),
        ],
        out_specs=pl.BlockSpec((1, seq, LANES), lambda b, j: (b, 0, j)),
        out_shape=jax.ShapeDtypeStruct((bsz, seq, B_WIDTH), BF16),
        compiler_params=_cparams(("parallel", "parallel")),
        name="neighbourhood_attention",
    )(u, u, u, kmeta, vmeta, bias)


def _route(scores, rbias, prefix_fn):
    t = scores.shape[-1]
    ninf = -jnp.inf
    biased = scores + rbias
    b3 = biased.reshape(N_GROUPS, GROUP_SIZE, t)
    s3 = scores.reshape(N_GROUPS, GROUP_SIZE, t)
    io_in = lax.broadcasted_iota(jnp.int32, b3.shape, 1)
    io_g3 = lax.broadcasted_iota(jnp.int32, b3.shape, 0)
    io_e = io_g3 * GROUP_SIZE + io_in
    m1 = jnp.max(b3, axis=1, keepdims=True)
    i1 = jnp.min(jnp.where(b3 == m1, io_in, GROUP_SIZE), axis=1, keepdims=True)
    m2 = jnp.max(jnp.where(io_in == i1, ninf, b3), axis=1, keepdims=True)
    gs = m1 + m2
    io_g = lax.broadcasted_iota(jnp.int32, gs.shape, 0)
    gmask = jnp.zeros(gs.shape, jnp.bool_)
    cur = gs
    for _ in range(TOPK_GROUPS):
        m = jnp.max(cur, axis=0, keepdims=True)
        i = jnp.min(jnp.where(cur == m, io_g, N_GROUPS), axis=0, keepdims=True)
        pick = io_g == i
        gmask = jnp.logical_or(gmask, pick)
        cur = jnp.where(pick, ninf, cur)
    cur = jnp.where(gmask, b3, ninf)
    picks, top_e, top_s = [], [], []
    for _ in range(TOP_K):
        m = jnp.max(jnp.max(cur, axis=1, keepdims=True), axis=0, keepdims=True)
        i = jnp.min(jnp.min(jnp.where(cur == m, io_e, N_EXPERTS), axis=1, keepdims=True),
                    axis=0, keepdims=True)
        pick = io_e == i
        picks.append(pick)
        top_e.append(i.reshape(1, t))
        w = jnp.sum(jnp.sum(jnp.where(pick, s3, 0.0), axis=1, keepdims=True), axis=0, keepdims=True)
        top_s.append(w.reshape(1, t))
        cur = jnp.where(pick, ninf, cur)
    top_e = jnp.concatenate(top_e, axis=0)
    top_s = jnp.concatenate(top_s, axis=0)
    denom = top_s[0:1]
    for k in range(1, TOP_K):
        denom = denom + top_s[k:k + 1]
    top_w = top_s / (denom + 1e-20) * ROUTED_SCALE
    sel = picks[0]
    for k in range(1, TOP_K):
        sel = jnp.logical_or(sel, picks[k])
    sel = jnp.where(sel, 1.0, 0.0).astype(F32).reshape(N_EXPERTS, t)
    return top_e, top_w, sel, picks


def _outproj_router_kernel(ya_ref, yb_ref, x_ref, wo_ref, ga_ref, gb_ref, eg_ref, eb_ref,
                           g1_ref, b1_ref, wrh_ref, wrl_ref, rb_ref,
                           h1_ref, h1b_ref, te_ref, tw_ref, tr_ref, cnt_ref, carry):
    tm = x_ref.shape[0]

    @pl.when(pl.program_id(0) == 0)
    def _():
        carry[...] = jnp.zeros_like(carry)

    def rms(y_ref, g_ref):
        y = y_ref[...].astype(F32)
        inv = lax.rsqrt(jnp.mean(y * y, axis=-1, keepdims=True) + RMS_EPS)
        return (y * inv * g_ref[...]).astype(BF16)

    mix = (_dot(rms(ya_ref, ga_ref), wo_ref[0:A_WIDTH, :])
           + _dot(rms(yb_ref, gb_ref), wo_ref[A_WIDTH:MIX_WIDTH, :]))
    h = _layer_norm(x_ref[...], eg_ref[...], eb_ref[...])
    h1 = _layer_norm(DEEPNORM_ALPHA * h + mix, g1_ref[...], b1_ref[...])
    h1_ref[...] = h1
    hb = h1.astype(BF16)
    h1b_ref[...] = hb

    hlo = (h1 - hb.astype(F32)).astype(BF16)
    logits = (_dot_nt(wrh_ref[...], hb) + _dot_nt(wrh_ref[...], hlo)
              + _dot_nt(wrl_ref[...], hb))
    scores = 1.0 / (1.0 + jnp.exp(-logits))
    top_e, top_w, sel, picks = _route(scores, rb_ref[...], None)

    row = lax.broadcasted_iota(jnp.int32, (tm, tm), 0)
    col = lax.broadcasted_iota(jnp.int32, (tm, tm), 1)
    before = jnp.where(row < col, 1.0, 0.0).astype(BF16)
    rank = _dot(sel.astype(BF16), before) + carry[...]
    rank3 = rank.reshape(N_GROUPS, GROUP_SIZE, tm)
    ranks = []
    for k in range(TOP_K):
        rk = jnp.sum(jnp.sum(jnp.where(picks[k], rank3, 0.0), axis=1, keepdims=True),
                     axis=0, keepdims=True)
        ranks.append(rk.reshape(1, tm))
    te_ref[...] = top_e
    tw_ref[...] = top_w
    tr_ref[...] = jnp.concatenate(ranks, axis=0).astype(jnp.int32)
    carry[...] = carry[...] + jnp.sum(sel, axis=-1, keepdims=True)
    cnt_ref[...] = jnp.broadcast_to(carry[...], cnt_ref.shape)


def _outproj_router(ya, yb, x2, wo_bf16, ga, gb, eg, eb, g1, b1, wrh, wrl, rb, tm):
    m = x2.shape[0]
    row = lambda w: pl.BlockSpec((tm, w), lambda i: (i, 0))
    full = lambda a: pl.BlockSpec(a.shape, lambda i: (0,) * a.ndim)
    tok = pl.BlockSpec((TOP_K, tm), lambda i: (0, i))
    return pl.pallas_call(
        _outproj_router_kernel,
        grid=(m // tm,),
        in_specs=[row(A_WIDTH), row(B_WIDTH), row(D_MODEL), full(wo_bf16), full(ga), full(gb),
                  full(eg), full(eb), full(g1), full(b1), full(wrh), full(wrl), full(rb)],
        out_specs=[row(D_MODEL), row(D_MODEL), tok, tok, tok,
                   pl.BlockSpec((N_EXPERTS, LANES), lambda i: (0, 0))],
        out_shape=[jax.ShapeDtypeStruct((m, D_MODEL), F32),
                   jax.ShapeDtypeStruct((m, D_MODEL), BF16),
                   jax.ShapeDtypeStruct((TOP_K, m), jnp.int32),
                   jax.ShapeDtypeStruct((TOP_K, m), F32),
                   jax.ShapeDtypeStruct((TOP_K, m), jnp.int32),
                   jax.ShapeDtypeStruct((N_EXPERTS, LANES), F32)],
        scratch_shapes=[pltpu.VMEM((N_EXPERTS, 1), F32)],
        compiler_params=_cparams(("arbitrary",)),
        name="outproj_router",
    )(ya, yb, x2, wo_bf16, ga, gb, eg, eb, g1, b1, wrh, wrl, rb)


def _moe_kernel(be_ref, nu_ref, x_ref, wg_ref, wu_ref, wd_ref, y_ref, wg_b, wu_b, wd_b):
    i = pl.program_id(0)
    e = be_ref[i]
    prev = be_ref[jnp.maximum(i - 1, 0)]

    @pl.when(jnp.logical_or(i == 0, e != prev))
    def _():
        wg_b[...] = wg_ref[0].astype(BF16)
        wu_b[...] = wu_ref[0].astype(BF16)
        wd_b[...] = wd_ref[0].astype(BF16)

    @pl.when(i < nu_ref[0])
    def _():
        x = x_ref[...]
        g = _dot(x, wg_b[...])
        u = _dot(x, wu_b[...])
        y_ref[...] = _dot((_silu(g) * u).astype(BF16), wd_b[...]).astype(y_ref.dtype)


def _moe_experts(block_e, n_used, x_sorted, w_gate, w_up, w_down):
    p = x_sorted.shape[0]
    n_blocks = p // MOE_BLOCK
    last = lambda i, be, nu: jnp.minimum(i, nu[0] - 1)
    return pl.pallas_call(
        _moe_kernel,
        grid_spec=pltpu.PrefetchScalarGridSpec(
            num_scalar_prefetch=2,
            grid=(n_blocks,),
            in_specs=[
                pl.BlockSpec((MOE_BLOCK, D_MODEL), lambda i, be, nu: (last(i, be, nu), 0)),
                pl.BlockSpec((1, D_MODEL, D_EXPERT), lambda i, be, nu: (be[i], 0, 0)),
                pl.BlockSpec((1, D_MODEL, D_EXPERT), lambda i, be, nu: (be[i], 0, 0)),
                pl.BlockSpec((1, D_EXPERT, D_MODEL), lambda i, be, nu: (be[i], 0, 0)),
            ],
            out_specs=pl.BlockSpec((MOE_BLOCK, D_MODEL), lambda i, be, nu: (last(i, be, nu), 0)),
            scratch_shapes=[pltpu.VMEM((D_MODEL, D_EXPERT), BF16),
                            pltpu.VMEM((D_MODEL, D_EXPERT), BF16),
                            pltpu.VMEM((D_EXPERT, D_MODEL), BF16)],
        ),
        out_shape=jax.ShapeDtypeStruct((p, D_MODEL), BF16),
        compiler_params=_cparams(("arbitrary",)),
        name="moe_experts",
    )(block_e, n_used, x_sorted, w_gate, w_up, w_down)


def _final_kernel(h1_ref, h1b_ref, ys_ref, tw_ref, wsg_ref, wsu_ref, wsd_ref, g2_ref, b2_ref,
                  o_ref):
    hb = h1b_ref[...]
    g = _dot(hb, wsg_ref[...])
    u = _dot(hb, wsu_ref[...])
    ffn = _dot((_silu(g) * u).astype(BF16), wsd_ref[...])
    for k in range(TOP_K):
        ffn = ffn + ys_ref[k].astype(F32) * tw_ref[:, k:k + 1]
    o_ref[...] = _layer_norm(DEEPNORM_ALPHA * h1_ref[...] + ffn, g2_ref[...], b2_ref[...])


def _final(h1, h1b, ysel, tw_t, wsg, wsu, wsd, g2, b2, tm):
    m = h1.shape[0]
    row = lambda w: pl.BlockSpec((tm, w), lambda i: (i, 0))
    full = lambda a: pl.BlockSpec(a.shape, lambda i: (0,) * a.ndim)
    return pl.pallas_call(
        _final_kernel,
        grid=(m // tm,),
        in_specs=[row(D_MODEL), row(D_MODEL),
                  pl.BlockSpec((TOP_K, tm, D_MODEL), lambda i: (0, i, 0)),
                  row(TOP_K), full(wsg), full(wsu), full(wsd), full(g2), full(b2)],
        out_specs=row(D_MODEL),
        out_shape=jax.ShapeDtypeStruct((m, D_MODEL), F32),
        compiler_params=_cparams(("parallel",)),
        name="shared_combine_ln2",
    )(h1, h1b, ysel, tw_t, wsg, wsu, wsd, g2, b2)


def _pad_rows(a, rows):
    return jnp.concatenate([a, jnp.zeros((rows - a.shape[0],) + a.shape[1:], a.dtype)], axis=0)


def kernel(x, meta_tokens, ln_emb_g, ln_emb_b, t5_table, w_in, a_sink, na_rpb, g_norm_a, g_norm_b, w_out, ln1_g, ln1_b, w_router, router_bias, w_gate, w_up, w_down, ws_gate, ws_up, ws_down, ln2_g, ln2_b):
    bsz, seq, _ = x.shape
    m = bsz * seq
    r2 = lambda a: a.reshape(1, -1).astype(F32)
    x2 = x.reshape(m, D_MODEL)
    eg, eb = r2(ln_emb_g), r2(ln_emb_b)

    w_in_b = w_in[0].astype(BF16)
    tm = 512 if m % 512 == 0 else 128
    u = _ln_inproj(x2, eg, eb, w_in_b, tm, 1152).reshape(bsz, seq, IN_WIDTH)
    um = _ln_inproj(meta_tokens.astype(F32), eg, eb, w_in_b, N_META, 1152)
    cut = lambda c0, width: _pad_rows(um[:, c0 * LANES:c0 * LANES + width], LANES)

    ya = _window_attention(u, cut(COL_KA, KV_WIDTH), cut(COL_VA, KV_WIDTH),
                           _window_bias(t5_table, a_sink[0], seq), bsz, seq)
    yb = _neighbourhood_attention(u, cut(COL_KB, B_WIDTH), cut(COL_VB, B_WIDTH),
                                  _na_bias(na_rpb[0]), bsz, seq)

    wr_t = w_router[0].astype(F32).T
    wr_hi = wr_t.astype(BF16)
    wr_lo = (wr_t - wr_hi.astype(F32)).astype(BF16)
    tm4 = 256 if m % 256 == 0 else 128
    h1, h1b, top_e, top_w, top_r, cnt = _outproj_router(
        ya.reshape(m, A_WIDTH), yb.reshape(m, B_WIDTH), x2, w_out[0].astype(BF16),
        r2(g_norm_a), r2(g_norm_b), eg, eb, r2(ln1_g), r2(ln1_b), wr_hi, wr_lo,
        router_bias[0].astype(F32).reshape(N_EXPERTS, 1), tm4)

    counts = cnt[:, 0].astype(jnp.int32)
    nb_e = (counts + MOE_BLOCK - 1) // MOE_BLOCK
    bend = jnp.cumsum(nb_e)
    pstart = (bend - nb_e) * MOE_BLOCK
    n_blocks = (m * TOP_K) // MOE_BLOCK + N_EXPERTS
    p_rows = n_blocks * MOE_BLOCK
    blk = jnp.arange(n_blocks, dtype=jnp.int32)
    block_e = jnp.minimum(jnp.sum((bend[None, :] <= blk[:, None]).astype(jnp.int32), axis=1),
                          N_EXPERTS - 1)
    n_used = bend[-1:].astype(jnp.int32)
    expert_ids = jnp.arange(N_EXPERTS, dtype=jnp.int32)[:, None, None]
    dest = top_r + jnp.sum(jnp.where(top_e[None] == expert_ids, pstart[:, None, None], 0),
                           axis=0)
    tok = jnp.broadcast_to(jnp.arange(m, dtype=jnp.int32)[None, :], (TOP_K, m))
    slot_tok = jnp.zeros((p_rows,), jnp.int32).at[dest.reshape(-1)].set(tok.reshape(-1))
    x_sorted = jnp.take(h1b, slot_tok, axis=0)

    y_sorted = _moe_experts(block_e, n_used, x_sorted, w_gate[0], w_up[0], w_down[0])
    ysel = jnp.take(y_sorted, dest.reshape(-1), axis=0).reshape(TOP_K, m, D_MODEL)

    out = _final(h1, h1b, ysel, top_w.T, ws_gate[0].astype(BF16), ws_up[0].astype(BF16),
                 ws_down[0].astype(BF16), r2(ln2_g), r2(ln2_b), tm4)
    return out.reshape(bsz, seq, D_MODEL)
```

```python
import functools
import math

import jax
import jax.numpy as jnp
from jax import lax
from jax.experimental import pallas as pl
from jax.experimental.pallas import tpu as pltpu

F32 = jnp.float32
BF16 = jnp.bfloat16

D_MODEL = 2048
HEAD_DIM = 64
N_META = 16
GRID_W = 64
A_HEADS = 16
A_KV_HEADS = 4
A_WINDOW = 128
A_BLOCK = 128
N_BUCKETS = 32
MAX_DISTANCE = 128
B_HEADS = 16
NA_ROWS = 8
NA_COLS = 16
A_WIDTH = A_HEADS * HEAD_DIM
KV_WIDTH = A_KV_HEADS * HEAD_DIM
B_WIDTH = B_HEADS * HEAD_DIM
MIX_WIDTH = A_WIDTH + B_WIDTH
IN_WIDTH = A_WIDTH + 2 * KV_WIDTH + 3 * B_WIDTH
N_EXPERTS = 64
TOP_K = 8
N_GROUPS = 8
GROUP_SIZE = N_EXPERTS // N_GROUPS
TOPK_GROUPS = 4
D_EXPERT = 512
D_SHARED = 512
ROUTED_SCALE = 2.5
DEPTH = 1
DEEPNORM_ALPHA = (2 * DEPTH) ** 0.25
LN_EPS = 1e-5
RMS_EPS = 1e-6
NEG = -1e30
SCALE = HEAD_DIM ** -0.5

LANES = 128
VMEM_LIMIT = 56 * 1024 * 1024

COL_QA = 0
COL_KA = A_WIDTH // LANES
COL_VA = (A_WIDTH + KV_WIDTH) // LANES
COL_QB = (A_WIDTH + 2 * KV_WIDTH) // LANES
COL_KB = COL_QB + B_WIDTH // LANES
COL_VB = COL_KB + B_WIDTH // LANES

MOE_BLOCK = 256


def _cparams(sem):
    return pltpu.CompilerParams(dimension_semantics=sem, vmem_limit_bytes=VMEM_LIMIT)


def _layer_norm(x, g, b):
    mu = jnp.mean(x, axis=-1, keepdims=True)
    xc = x - mu
    var = jnp.mean(xc * xc, axis=-1, keepdims=True)
    return xc * lax.rsqrt(var + LN_EPS) * g + b


def _dot(a, b):
    return jnp.dot(a, b, preferred_element_type=F32)


def _dot_nt(a, b):
    return lax.dot_general(a, b, (((1,), (1,)), ((), ())), preferred_element_type=F32)


def _silu(g):
    return g / (1.0 + jnp.exp(-g))


def _ln_inproj_kernel(x_ref, g_ref, b_ref, w_ref, o_ref, h_scr):
    @pl.when(pl.program_id(1) == 0)
    def _():
        h_scr[...] = _layer_norm(x_ref[...], g_ref[...], b_ref[...]).astype(BF16)

    o_ref[...] = _dot(h_scr[...], w_ref[...]).astype(o_ref.dtype)


def _ln_inproj(x2, g, b, w_bf16, tm, tn):
    m = x2.shape[0]
    n = w_bf16.shape[1]
    return pl.pallas_call(
        _ln_inproj_kernel,
        grid=(m // tm, n // tn),
        in_specs=[
            pl.BlockSpec((tm, D_MODEL), lambda i, j: (i, 0)),
            pl.BlockSpec((1, D_MODEL), lambda i, j: (0, 0)),
            pl.BlockSpec((1, D_MODEL), lambda i, j: (0, 0)),
            pl.BlockSpec((D_MODEL, tn), lambda i, j: (0, j)),
        ],
        out_specs=pl.BlockSpec((tm, tn), lambda i, j: (i, j)),
        out_shape=jax.ShapeDtypeStruct((m, n), BF16),
        scratch_shapes=[pltpu.VMEM((tm, D_MODEL), BF16)],
        compiler_params=_cparams(("parallel", "arbitrary")),
        name="ln_inproj",
    )(x2, g, b, w_bf16)


def _t5_bucket(rel):
    nb = N_BUCKETS // 2
    max_exact = nb // 2
    ret = jnp.where(rel > 0, nb, 0)
    n = jnp.abs(rel)
    nf = jnp.maximum(n, 1).astype(F32)
    large = max_exact + (jnp.log(nf / max_exact) / math.log(MAX_DISTANCE / max_exact)
                         * (nb - max_exact)).astype(jnp.int32)
    large = jnp.minimum(large, nb - 1)
    return ret + jnp.where(n < max_exact, n, large)


def _lookup(table_t, idx, n):
    onehot = (idx[None] == jnp.arange(n, dtype=jnp.int32).reshape((n,) + (1,) * idx.ndim))
    return jnp.einsum('hb,b...->h...', table_t, onehot.astype(F32),
                      precision=lax.Precision.HIGHEST)


def _window_bias(t5_table, a_sink, seq):
    assert N_META + A_BLOCK - (N_META - 1) > MAX_DISTANCE
    nblk = seq // A_BLOCK
    t5_t = t5_table.astype(F32).T
    q_loc = jnp.arange(A_BLOCK, dtype=jnp.int32)
    k_loc = jnp.arange(3 * A_BLOCK, dtype=jnp.int32) - A_BLOCK
    rel = k_loc[None, :] - q_loc[:, None]
    band = _lookup(t5_t, _t5_bucket(rel), N_BUCKETS)
    win = jnp.abs(rel) <= A_WINDOW
    sink = jnp.broadcast_to(a_sink.astype(F32)[:, None, None], (A_HEADS, A_BLOCK, 1))
    pad = jnp.full((A_HEADS, A_BLOCK, LANES - N_META - 1), NEG, F32)
    out = []
    for blk in (0, min(1, nblk - 1), nblk - 1):
        gk = blk * A_BLOCK + k_loc
        valid = win & (gk >= 0)[None, :] & (gk < seq)[None, :]
        band_v = jnp.where(valid[None], band, NEG)
        q_pos = N_META + blk * A_BLOCK + q_loc
        rel_m = jnp.arange(N_META, dtype=jnp.int32)[None, :] - q_pos[:, None]
        bias_m = _lookup(t5_t, _t5_bucket(rel_m), N_BUCKETS)
        out.append(jnp.concatenate([bias_m, sink, pad, band_v], axis=-1))
    return jnp.stack(out, axis=0)


def _window_kernel(q_ref, kp_ref, kc_ref, kn_ref, vp_ref, vc_ref, vn_ref, km_ref, vm_ref,
                   bias_ref, o_ref):
    kall = jnp.concatenate([km_ref[...], kp_ref[0], kc_ref[0], kn_ref[0]], axis=0)
    vall = jnp.concatenate([vm_ref[...], vp_ref[0], vc_ref[0], vn_ref[0]], axis=0)
    lane = lax.broadcasted_iota(jnp.int32, (A_BLOCK, LANES), 1)
    in_lo = lane < HEAD_DIM
    group = A_HEADS // A_KV_HEADS
    pair_out = []
    for p in range(4):
        qp = q_ref[0, :, p * LANES:(p + 1) * LANES].astype(F32) * SCALE
        qr = pltpu.roll(qp, HEAD_DIM, axis=1)
        res = []
        for half in range(2):
            hl = 2 * p + half
            kv_half = hl // group
            src = qp if half == kv_half else qr
            keep = in_lo if kv_half == 0 else jnp.logical_not(in_lo)
            qm = jnp.where(keep, src, 0.0).astype(BF16)
            s = _dot_nt(qm, kall) + bias_ref[0, hl]
            m = jnp.max(s, axis=-1, keepdims=True)
            e = jnp.exp(s - m)
            l = jnp.sum(e, axis=-1, keepdims=True)
            o = _dot(e.astype(BF16), vall) / l
            if half != kv_half:
                o = pltpu.roll(o, HEAD_DIM, axis=1)
            res.append(o)
        pair_out.append(jnp.where(in_lo, res[0], res[1]))
    o_ref[0] = jnp.concatenate(pair_out, axis=1).astype(o_ref.dtype)


def _window_attention(u, kmeta, vmeta, bias, bsz, seq):
    nblk = seq // A_BLOCK
    n_pairs = A_KV_HEADS // 2
    qw = A_WIDTH // n_pairs

    def variant(n):
        return jnp.where(n == 0, 0, jnp.where(n == nblk - 1, 2, 1))

    def kv_spec(col0, shift):
        return pl.BlockSpec(
            (1, A_BLOCK, LANES),
            lambda b, j, n: (b, jnp.clip(n + shift, 0, nblk - 1), col0 + j))

    return pl.pallas_call(
        _window_kernel,
        grid=(bsz, n_pairs, nblk),
        in_specs=[
            pl.BlockSpec((1, A_BLOCK, qw), lambda b, j, n: (b, n, j)),
            kv_spec(COL_KA, -1), kv_spec(COL_KA, 0), kv_spec(COL_KA, 1),
            kv_spec(COL_VA, -1), kv_spec(COL_VA, 0), kv_spec(COL_VA, 1),
            pl.BlockSpec((LANES, LANES), lambda b, j, n: (0, j)),
            pl.BlockSpec((LANES, LANES), lambda b, j, n: (0, j)),
            pl.BlockSpec((1, A_HEADS // n_pairs, A_BLOCK, 4 * LANES),
                         lambda b, j, n: (variant(n), j, 0, 0)),
        ],
        out_specs=pl.BlockSpec((1, A_BLOCK, qw), lambda b, j, n: (b, n, j)),
        out_shape=jax.ShapeDtypeStruct((bsz, seq, A_WIDTH), BF16),
        compiler_params=_cparams(("parallel", "parallel", "arbitrary")),
        name="window_attention",
    )(u, u, u, u, u, u, u, kmeta, vmeta, bias)


def _na_bias(rpb):
    qc = jnp.arange(GRID_W, dtype=jnp.int32)
    kc = jnp.arange(GRID_W, dtype=jnp.int32)
    cs = jnp.clip(qc - NA_COLS // 2, 0, GRID_W - NA_COLS)
    cmask = (kc[None, :] >= cs[:, None]) & (kc[None, :] < cs[:, None] + NA_COLS)
    dc = jnp.clip(kc[None, :] - qc[:, None] + NA_COLS - 1, 0, 2 * NA_COLS - 2)
    n_dc = 2 * NA_COLS - 1
    onehot = (dc[None] == jnp.arange(n_dc, dtype=jnp.int32)[:, None, None]).astype(F32)
    t = jnp.einsum('hrd,dqk->hrqk', rpb.astype(F32), onehot, precision=lax.Precision.HIGHEST)
    t = jnp.where(cmask[None, None], t, NEG)
    variants = []
    for oi in range(NA_ROWS):
        variants.append(jnp.concatenate(
            [t[:, i - oi + NA_ROWS - 1] for i in range(NA_ROWS)], axis=-1))
    return jnp.stack(variants, axis=1)


def _na_kernel(q_ref, k_ref, v_ref, km_ref, vm_ref, bias_ref, o_ref, *, rows):
    span = NA_ROWS * GRID_W
    lane = lax.broadcasted_iota(jnp.int32, (GRID_W, LANES), 1)
    in_lo = lane < HEAD_DIM
    meta_bias = jnp.where(lane < N_META, 0.0, NEG).astype(F32)
    km = km_ref[...]
    vm = vm_ref[...]

    def body(r, carry):
        rs = jnp.clip(r - NA_ROWS // 2, 0, rows - NA_ROWS)
        oi = r - rs
        q = q_ref[0, pl.ds(pl.multiple_of(r * GRID_W, GRID_W), GRID_W), :]
        k0 = pl.multiple_of(rs * GRID_W, GRID_W)
        ks = k_ref[0, pl.ds(k0, span), :]
        vs = v_ref[0, pl.ds(k0, span), :]
        res = []
        for hl in range(2):
            keep = in_lo if hl == 0 else jnp.logical_not(in_lo)
            qm = jnp.where(keep, q, jnp.zeros_like(q))
            sw = _dot_nt(qm, ks) * SCALE + bias_ref[hl, oi]
            sm = _dot_nt(qm, km) * SCALE + meta_bias
            m = jnp.maximum(jnp.max(sw, axis=-1, keepdims=True),
                            jnp.max(sm, axis=-1, keepdims=True))
            ew = jnp.exp(sw - m)
            em = jnp.exp(sm - m)
            l = jnp.sum(ew, axis=-1, keepdims=True) + jnp.sum(em, axis=-1, keepdims=True)
            o = _dot(ew.astype(BF16), vs) + _dot(em.astype(BF16), vm)
            res.append(o / l)
        out = jnp.where(in_lo, res[0], res[1])
        o_ref[0, pl.ds(pl.multiple_of(r * GRID_W, GRID_W), GRID_W), :] = out.astype(o_ref.dtype)
        return carry

    lax.fori_loop(0, rows, body, 0)


def _neighbourhood_attention(u, kmeta, vmeta, bias, bsz, seq):
    rows = seq // GRID_W
    assert rows >= NA_ROWS
    n_pairs = B_HEADS // 2

    def col_spec(col0):
        return pl.BlockSpec((1, seq, LANES), lambda b, j: (b, 0, col0 + j))

    return pl.pallas_call(
        functools.partial(_na_kernel, rows=rows),
        grid=(bsz, n_pairs),
        in_specs=[
            col_spec(COL_QB), col_spec(COL_KB), col_spec(COL_VB),
            pl.BlockSpec((LANES, LANES), lambda b, j: (0, j)),
            pl.BlockSpec((LANES, LANES), lambda b, j: (0, j)),
            pl.BlockSpec((2, NA_ROWS, GRID_W, NA_ROWS * GRID_W), lambda b, j: (j, 0, 0, 0)),
        ],
        out_specs=pl.BlockSpec((1, seq, LANES), lambda b, j: (b, 0, j)),
        out_shape=jax.ShapeDtypeStruct((bsz, seq, B_WIDTH), BF16),
        compiler_params=_cparams(("parallel", "parallel")),
        name="neighbourhood_attention",
    )(u, u, u, kmeta, vmeta, bias)


def _route(scores, rbias, prefix_fn):
    t = scores.shape[-1]
    ninf = -jnp.inf
    biased = scores + rbias
    b3 = biased.reshape(N_GROUPS, GROUP_SIZE, t)
    s3 = scores.reshape(N_GROUPS, GROUP_SIZE, t)
    io_in = lax.broadcasted_iota(jnp.int32, b3.shape, 1)
    io_g3 = lax.broadcasted_iota(jnp.int32, b3.shape, 0)
    io_e = io_g3 * GROUP_SIZE + io_in
    m1 = jnp.max(b3, axis=1, keepdims=True)
    i1 = jnp.min(jnp.where(b3 == m1, io_in, GROUP_SIZE), axis=1, keepdims=True)
    m2 = jnp.max(jnp.where(io_in == i1, ninf, b3), axis=1, keepdims=True)
    gs = m1 + m2
    io_g = lax.broadcasted_iota(jnp.int32, gs.shape, 0)
    gmask = jnp.zeros(gs.shape, jnp.bool_)
    cur = gs
    for _ in range(TOPK_GROUPS):
        m = jnp.max(cur, axis=0, keepdims=True)
        i = jnp.min(jnp.where(cur == m, io_g, N_GROUPS), axis=0, keepdims=True)
        pick = io_g == i
        gmask = jnp.logical_or(gmask, pick)
        cur = jnp.where(pick, ninf, cur)
    cur = jnp.where(gmask, b3, ninf)
    picks, top_e, top_s = [], [], []
    for _ in range(TOP_K):
        m = jnp.max(jnp.max(cur, axis=1, keepdims=True), axis=0, keepdims=True)
        i = jnp.min(jnp.min(jnp.where(cur == m, io_e, N_EXPERTS), axis=1, keepdims=True),
                    axis=0, keepdims=True)
        pick = io_e == i
        picks.append(pick)
        top_e.append(i.reshape(1, t))
        w = jnp.sum(jnp.sum(jnp.where(pick, s3, 0.0), axis=1, keepdims=True), axis=0, keepdims=True)
        top_s.append(w.reshape(1, t))
        cur = jnp.where(pick, ninf, cur)
    top_e = jnp.concatenate(top_e, axis=0)
    top_s = jnp.concatenate(top_s, axis=0)
    denom = top_s[0:1]
    for k in range(1, TOP_K):
        denom = denom + top_s[k:k + 1]
    top_w = top_s / (denom + 1e-20) * ROUTED_SCALE
    sel = picks[0]
    for k in range(1, TOP_K):
        sel = jnp.logical_or(sel, picks[k])
    sel = jnp.where(sel, 1.0, 0.0).astype(F32).reshape(N_EXPERTS, t)
    return top_e, top_w, sel, picks


def _outproj_router_kernel(ya_ref, yb_ref, x_ref, wo_ref, ga_ref, gb_ref, eg_ref, eb_ref,
                           g1_ref, b1_ref, wrh_ref, wrl_ref, rb_ref,
                           h1_ref, h1b_ref, te_ref, tw_ref, tr_ref, cnt_ref, carry):
    tm = x_ref.shape[0]

    @pl.when(pl.program_id(0) == 0)
    def _():
        carry[...] = jnp.zeros_like(carry)

    def rms(y_ref, g_ref):
        y = y_ref[...].astype(F32)
        inv = lax.rsqrt(jnp.mean(y * y, axis=-1, keepdims=True) + RMS_EPS)
        return (y * inv * g_ref[...]).astype(BF16)

    mix = (_dot(rms(ya_ref, ga_ref), wo_ref[0:A_WIDTH, :])
           + _dot(rms(yb_ref, gb_ref), wo_ref[A_WIDTH:MIX_WIDTH, :]))
    h = _layer_norm(x_ref[...], eg_ref[...], eb_ref[...])
    h1 = _layer_norm(DEEPNORM_ALPHA * h + mix, g1_ref[...], b1_ref[...])
    h1_ref[...] = h1
    hb = h1.astype(BF16)
    h1b_ref[...] = hb

    hlo = (h1 - hb.astype(F32)).astype(BF16)
    logits = (_dot_nt(wrh_ref[...], hb) + _dot_nt(wrh_ref[...], hlo)
              + _dot_nt(wrl_ref[...], hb))
    scores = 1.0 / (1.0 + jnp.exp(-logits))
    top_e, top_w, sel, picks = _route(scores, rb_ref[...], None)

    row = lax.broadcasted_iota(jnp.int32, (tm, tm), 0)
    col = lax.broadcasted_iota(jnp.int32, (tm, tm), 1)
    before = jnp.where(row < col, 1.0, 0.0).astype(BF16)
    rank = _dot(sel.astype(BF16), before) + carry[...]
    rank3 = rank.reshape(N_GROUPS, GROUP_SIZE, tm)
    ranks = []
    for k in range(TOP_K):
        rk = jnp.sum(jnp.sum(jnp.where(picks[k], rank3, 0.0), axis=1, keepdims=True),
                     axis=0, keepdims=True)
        ranks.append(rk.reshape(1, tm))
    te_ref[...] = top_e
    tw_ref[...] = top_w
    tr_ref[...] = jnp.concatenate(ranks, axis=0).astype(jnp.int32)
    carry[...] = carry[...] + jnp.sum(sel, axis=-1, keepdims=True)
    cnt_ref[...] = jnp.broadcast_to(carry[...], cnt_ref.shape)


def _outproj_router(ya, yb, x2, wo_bf16, ga, gb, eg, eb, g1, b1, wrh, wrl, rb, tm):
    m = x2.shape[0]
    row = lambda w: pl.BlockSpec((tm, w), lambda i: (i, 0))
    full = lambda a: pl.BlockSpec(a.shape, lambda i: (0,) * a.ndim)
    tok = pl.BlockSpec((TOP_K, tm), lambda i: (0, i))
    return pl.pallas_call(
        _outproj_router_kernel,
        grid=(m // tm,),
        in_specs=[row(A_WIDTH), row(B_WIDTH), row(D_MODEL), full(wo_bf16), full(ga), full(gb),
                  full(eg), full(eb), full(g1), full(b1), full(wrh), full(wrl), full(rb)],
        out_specs=[row(D_MODEL), row(D_MODEL), tok, tok, tok,
                   pl.BlockSpec((N_EXPERTS, LANES), lambda i: (0, 0))],
        out_shape=[jax.ShapeDtypeStruct((m, D_MODEL), F32),
                   jax.ShapeDtypeStruct((m, D_MODEL), BF16),
                   jax.ShapeDtypeStruct((TOP_K, m), jnp.int32),
                   jax.ShapeDtypeStruct((TOP_K, m), F32),
                   jax.ShapeDtypeStruct((TOP_K, m), jnp.int32),
                   jax.ShapeDtypeStruct((N_EXPERTS, LANES), F32)],
        scratch_shapes=[pltpu.VMEM((N_EXPERTS, 1), F32)],
        compiler_params=_cparams(("arbitrary",)),
        name="outproj_router",
    )(ya, yb, x2, wo_bf16, ga, gb, eg, eb, g1, b1, wrh, wrl, rb)


def _moe_kernel(be_ref, nu_ref, tok_ref, tok_next_ref, h_hbm, wg_ref, wu_ref, wd_ref, y_ref,
                xbuf0, xbuf1, sem, wg_b, wu_b, wd_b):
    i = pl.program_id(0)
    n_used = nu_ref[0]
    even = i % 2 == 0
    e = be_ref[i]
    prev = be_ref[jnp.maximum(i - 1, 0)]
    bufs = (xbuf0, xbuf1)

    def start_gather(idx_ref, b):
        for r in range(MOE_BLOCK):
            pltpu.make_async_copy(h_hbm.at[pl.ds(idx_ref[0, 0, r], 1), :],
                                  bufs[b].at[pl.ds(r, 1), :], sem.at[b]).start()

    def wait_gather(b):
        pltpu.make_async_copy(h_hbm.at[pl.ds(0, MOE_BLOCK), :], bufs[b], sem.at[b]).wait()

    @pl.when(i == 0)
    def _():
        start_gather(tok_ref, 0)

    @pl.when(jnp.logical_or(i == 0, e != prev))
    def _():
        wg_b[...] = wg_ref[0].astype(BF16)
        wu_b[...] = wu_ref[0].astype(BF16)
        wd_b[...] = wd_ref[0].astype(BF16)

    def step(b):
        wait_gather(b)
        start_gather(tok_next_ref, 1 - b)
        x = bufs[b][...].astype(BF16)
        g = _dot(x, wg_b[...])
        u = _dot(x, wu_b[...])
        y_ref[...] = _dot((_silu(g) * u).astype(BF16), wd_b[...]).astype(y_ref.dtype)

    for b in range(2):
        parity = even if b == 0 else jnp.logical_not(even)

        @pl.when(jnp.logical_and(i < n_used, parity))
        def _():
            step(b)

        @pl.when(jnp.logical_and(i == n_used, parity))
        def _():
            wait_gather(b)

    @pl.when(i >= n_used)
    def _():
        y_ref[...] = jnp.zeros_like(y_ref)


def _moe_experts(block_e, n_used, slot_tok, h1, w_gate, w_up, w_down):
    n_blocks = slot_tok.shape[0]
    p = n_blocks * MOE_BLOCK
    assert (h1.shape[0] * TOP_K + N_EXPERTS * (MOE_BLOCK - 1)) // MOE_BLOCK < n_blocks
    idx_spec = lambda shift: pl.BlockSpec(
        (1, 1, MOE_BLOCK), lambda i, be, nu: (jnp.minimum(i + shift, n_blocks - 1), 0, 0),
        memory_space=pltpu.SMEM)
    return pl.pallas_call(
        _moe_kernel,
        grid_spec=pltpu.PrefetchScalarGridSpec(
            num_scalar_prefetch=2,
            grid=(n_blocks,),
            in_specs=[
                idx_spec(0), idx_spec(1),
                pl.BlockSpec(memory_space=pl.ANY),
                pl.BlockSpec((1, D_MODEL, D_EXPERT), lambda i, be, nu: (be[i], 0, 0)),
                pl.BlockSpec((1, D_MODEL, D_EXPERT), lambda i, be, nu: (be[i], 0, 0)),
                pl.BlockSpec((1, D_EXPERT, D_MODEL), lambda i, be, nu: (be[i], 0, 0)),
            ],
            out_specs=pl.BlockSpec((MOE_BLOCK, D_MODEL), lambda i, be, nu: (i, 0)),
            scratch_shapes=[pltpu.VMEM((MOE_BLOCK, D_MODEL), F32),
                            pltpu.VMEM((MOE_BLOCK, D_MODEL), F32),
                            pltpu.SemaphoreType.DMA((2,)),
                            pltpu.VMEM((D_MODEL, D_EXPERT), BF16),
                            pltpu.VMEM((D_MODEL, D_EXPERT), BF16),
                            pltpu.VMEM((D_EXPERT, D_MODEL), BF16)],
        ),
        out_shape=jax.ShapeDtypeStruct((p, D_MODEL), BF16),
        compiler_params=pltpu.CompilerParams(
            dimension_semantics=("arbitrary",), vmem_limit_bytes=VMEM_LIMIT,
            disable_bounds_checks=True),
        name="moe_experts",
    )(block_e, n_used, slot_tok, slot_tok, h1, w_gate, w_up, w_down)


def _final_kernel(h1_ref, h1b_ref, ys_ref, tw_ref, wsg_ref, wsu_ref, wsd_ref, g2_ref, b2_ref,
                  o_ref):
    hb = h1b_ref[...]
    g = _dot(hb, wsg_ref[...])
    u = _dot(hb, wsu_ref[...])
    ffn = _dot((_silu(g) * u).astype(BF16), wsd_ref[...])
    for k in range(TOP_K):
        ffn = ffn + ys_ref[k].astype(F32) * tw_ref[:, k:k + 1]
    o_ref[...] = _layer_norm(DEEPNORM_ALPHA * h1_ref[...] + ffn, g2_ref[...], b2_ref[...])


def _final(h1, h1b, ysel, tw_t, wsg, wsu, wsd, g2, b2, tm):
    m = h1.shape[0]
    row = lambda w: pl.BlockSpec((tm, w), lambda i: (i, 0))
    full = lambda a: pl.BlockSpec(a.shape, lambda i: (0,) * a.ndim)
    return pl.pallas_call(
        _final_kernel,
        grid=(m // tm,),
        in_specs=[row(D_MODEL), row(D_MODEL),
                  pl.BlockSpec((TOP_K, tm, D_MODEL), lambda i: (0, i, 0)),
                  row(TOP_K), full(wsg), full(wsu), full(wsd), full(g2), full(b2)],
        out_specs=row(D_MODEL),
        out_shape=jax.ShapeDtypeStruct((m, D_MODEL), F32),
        compiler_params=_cparams(("parallel",)),
        name="shared_combine_ln2",
    )(h1, h1b, ysel, tw_t, wsg, wsu, wsd, g2, b2)


def _pad_rows(a, rows):
    return jnp.concatenate([a, jnp.zeros((rows - a.shape[0],) + a.shape[1:], a.dtype)], axis=0)


def kernel(x, meta_tokens, ln_emb_g, ln_emb_b, t5_table, w_in, a_sink, na_rpb, g_norm_a, g_norm_b, w_out, ln1_g, ln1_b, w_router, router_bias, w_gate, w_up, w_down, ws_gate, ws_up, ws_down, ln2_g, ln2_b):
    bsz, seq, _ = x.shape
    m = bsz * seq
    r2 = lambda a: a.reshape(1, -1).astype(F32)
    x2 = x.reshape(m, D_MODEL)
    eg, eb = r2(ln_emb_g), r2(ln_emb_b)

    w_in_b = w_in[0].astype(BF16)
    tm = 512 if m % 512 == 0 else 128
    u = _ln_inproj(x2, eg, eb, w_in_b, tm, 1152).reshape(bsz, seq, IN_WIDTH)
    um = _ln_inproj(meta_tokens.astype(F32), eg, eb, w_in_b, N_META, 1152)
    cut = lambda c0, width: _pad_rows(um[:, c0 * LANES:c0 * LANES + width], LANES)

    ya = _window_attention(u, cut(COL_KA, KV_WIDTH), cut(COL_VA, KV_WIDTH),
                           _window_bias(t5_table, a_sink[0], seq), bsz, seq)
    yb = _neighbourhood_attention(u, cut(COL_KB, B_WIDTH), cut(COL_VB, B_WIDTH),
                                  _na_bias(na_rpb[0]), bsz, seq)

    wr_t = w_router[0].astype(F32).T
    wr_hi = wr_t.astype(BF16)
    wr_lo = (wr_t - wr_hi.astype(F32)).astype(BF16)
    tm4 = 256 if m % 256 == 0 else 128
    h1, h1b, top_e, top_w, top_r, cnt = _outproj_router(
        ya.reshape(m, A_WIDTH), yb.reshape(m, B_WIDTH), x2, w_out[0].astype(BF16),
        r2(g_norm_a), r2(g_norm_b), eg, eb, r2(ln1_g), r2(ln1_b), wr_hi, wr_lo,
        router_bias[0].astype(F32).reshape(N_EXPERTS, 1), tm4)

    counts = cnt[:, 0].astype(jnp.int32)
    nb_e = (counts + MOE_BLOCK - 1) // MOE_BLOCK
    bend = jnp.cumsum(nb_e)
    pstart = (bend - nb_e) * MOE_BLOCK
    n_blocks = (m * TOP_K) // MOE_BLOCK + N_EXPERTS
    p_rows = n_blocks * MOE_BLOCK
    blk = jnp.arange(n_blocks, dtype=jnp.int32)
    block_e = jnp.minimum(jnp.sum((bend[None, :] <= blk[:, None]).astype(jnp.int32), axis=1),
                          N_EXPERTS - 1)
    n_used = bend[-1:].astype(jnp.int32)
    expert_ids = jnp.arange(N_EXPERTS, dtype=jnp.int32)[:, None, None]
    dest = top_r + jnp.sum(jnp.where(top_e[None] == expert_ids, pstart[:, None, None], 0),
                           axis=0)
    tok = jnp.broadcast_to(jnp.arange(m, dtype=jnp.int32)[None, :], (TOP_K, m))
    slot_tok = jnp.zeros((p_rows,), jnp.int32).at[dest.reshape(-1)].set(tok.reshape(-1))

    y_sorted = _moe_experts(block_e, n_used, slot_tok.reshape(n_blocks, 1, MOE_BLOCK), h1,
                            w_gate[0], w_up[0], w_down[0])
    ysel = jnp.take(y_sorted, dest.reshape(-1), axis=0).reshape(TOP_K, m, D_MODEL)

    out = _final(h1, h1b, ysel, top_w.T, ws_gate[0].astype(BF16), ws_up[0].astype(BF16),
                 ws_down[0].astype(BF16), r2(ln2_g), r2(ln2_b), tm4)
    return out.reshape(bsz, seq, D_MODEL)
```

```python
import functools
import math

import jax
import jax.numpy as jnp
from jax import lax
from jax.experimental import pallas as pl
from jax.experimental.pallas import tpu as pltpu

F32 = jnp.float32
BF16 = jnp.bfloat16

D_MODEL = 2048
HEAD_DIM = 64
N_META = 16
GRID_W = 64
A_HEADS = 16
A_KV_HEADS = 4
A_WINDOW = 128
A_BLOCK = 128
N_BUCKETS = 32
MAX_DISTANCE = 128
B_HEADS = 16
NA_ROWS = 8
NA_COLS = 16
A_WIDTH = A_HEADS * HEAD_DIM
KV_WIDTH = A_KV_HEADS * HEAD_DIM
B_WIDTH = B_HEADS * HEAD_DIM
MIX_WIDTH = A_WIDTH + B_WIDTH
IN_WIDTH = A_WIDTH + 2 * KV_WIDTH + 3 * B_WIDTH
N_EXPERTS = 64
TOP_K = 8
N_GROUPS = 8
GROUP_SIZE = N_EXPERTS // N_GROUPS
TOPK_GROUPS = 4
D_EXPERT = 512
D_SHARED = 512
ROUTED_SCALE = 2.5
DEPTH = 1
DEEPNORM_ALPHA = (2 * DEPTH) ** 0.25
LN_EPS = 1e-5
RMS_EPS = 1e-6
NEG = -1e30
SCALE = HEAD_DIM ** -0.5

LANES = 128
VMEM_LIMIT = 56 * 1024 * 1024

COL_QA = 0
COL_KA = A_WIDTH // LANES
COL_VA = (A_WIDTH + KV_WIDTH) // LANES
COL_QB = (A_WIDTH + 2 * KV_WIDTH) // LANES
COL_KB = COL_QB + B_WIDTH // LANES
COL_VB = COL_KB + B_WIDTH // LANES

MOE_BLOCK = 256
D_CHUNKS = D_MODEL // LANES


def _to_chunks(ref, x):
    for s in range(D_CHUNKS):
        ref[:, s, :] = x[:, s * LANES:(s + 1) * LANES]


def _from_chunks(ref):
    return jnp.concatenate([ref[:, s, :] for s in range(D_CHUNKS)], axis=1)


def _cparams(sem):
    return pltpu.CompilerParams(dimension_semantics=sem, vmem_limit_bytes=VMEM_LIMIT)


def _layer_norm(x, g, b):
    mu = jnp.mean(x, axis=-1, keepdims=True)
    xc = x - mu
    var = jnp.mean(xc * xc, axis=-1, keepdims=True)
    return xc * lax.rsqrt(var + LN_EPS) * g + b


def _dot(a, b):
    return jnp.dot(a, b, preferred_element_type=F32)


def _dot_nt(a, b):
    return lax.dot_general(a, b, (((1,), (1,)), ((), ())), preferred_element_type=F32)


def _silu(g):
    return g / (1.0 + jnp.exp(-g))


def _ln_inproj_kernel(x_ref, g_ref, b_ref, w_ref, o_ref, h_scr):
    @pl.when(pl.program_id(1) == 0)
    def _():
        h_scr[...] = _layer_norm(x_ref[...], g_ref[...], b_ref[...]).astype(BF16)

    o_ref[...] = _dot(h_scr[...], w_ref[...]).astype(o_ref.dtype)


def _ln_inproj(x2, g, b, w_bf16, tm, tn):
    m = x2.shape[0]
    n = w_bf16.shape[1]
    return pl.pallas_call(
        _ln_inproj_kernel,
        grid=(m // tm, n // tn),
        in_specs=[
            pl.BlockSpec((tm, D_MODEL), lambda i, j: (i, 0)),
            pl.BlockSpec((1, D_MODEL), lambda i, j: (0, 0)),
            pl.BlockSpec((1, D_MODEL), lambda i, j: (0, 0)),
            pl.BlockSpec((D_MODEL, tn), lambda i, j: (0, j)),
        ],
        out_specs=pl.BlockSpec((tm, tn), lambda i, j: (i, j)),
        out_shape=jax.ShapeDtypeStruct((m, n), BF16),
        scratch_shapes=[pltpu.VMEM((tm, D_MODEL), BF16)],
        compiler_params=_cparams(("parallel", "arbitrary")),
        name="ln_inproj",
    )(x2, g, b, w_bf16)


def _t5_bucket(rel):
    nb = N_BUCKETS // 2
    max_exact = nb // 2
    ret = jnp.where(rel > 0, nb, 0)
    n = jnp.abs(rel)
    nf = jnp.maximum(n, 1).astype(F32)
    large = max_exact + (jnp.log(nf / max_exact) / math.log(MAX_DISTANCE / max_exact)
                         * (nb - max_exact)).astype(jnp.int32)
    large = jnp.minimum(large, nb - 1)
    return ret + jnp.where(n < max_exact, n, large)


def _lookup(table_t, idx, n):
    onehot = (idx[None] == jnp.arange(n, dtype=jnp.int32).reshape((n,) + (1,) * idx.ndim))
    return jnp.einsum('hb,b...->h...', table_t, onehot.astype(F32),
                      precision=lax.Precision.HIGHEST)


def _window_bias(t5_table, a_sink, seq):
    assert N_META + A_BLOCK - (N_META - 1) > MAX_DISTANCE
    nblk = seq // A_BLOCK
    t5_t = t5_table.astype(F32).T
    q_loc = jnp.arange(A_BLOCK, dtype=jnp.int32)
    k_loc = jnp.arange(3 * A_BLOCK, dtype=jnp.int32) - A_BLOCK
    rel = k_loc[None, :] - q_loc[:, None]
    band = _lookup(t5_t, _t5_bucket(rel), N_BUCKETS)
    win = jnp.abs(rel) <= A_WINDOW
    sink = jnp.broadcast_to(a_sink.astype(F32)[:, None, None], (A_HEADS, A_BLOCK, 1))
    pad = jnp.full((A_HEADS, A_BLOCK, LANES - N_META - 1), NEG, F32)
    out = []
    for blk in (0, min(1, nblk - 1), nblk - 1):
        gk = blk * A_BLOCK + k_loc
        valid = win & (gk >= 0)[None, :] & (gk < seq)[None, :]
        band_v = jnp.where(valid[None], band, NEG)
        q_pos = N_META + blk * A_BLOCK + q_loc
        rel_m = jnp.arange(N_META, dtype=jnp.int32)[None, :] - q_pos[:, None]
        bias_m = _lookup(t5_t, _t5_bucket(rel_m), N_BUCKETS)
        out.append(jnp.concatenate([bias_m, sink, pad, band_v], axis=-1))
    return jnp.stack(out, axis=0)


def _window_kernel(q_ref, kp_ref, kc_ref, kn_ref, vp_ref, vc_ref, vn_ref, km_ref, vm_ref,
                   bias_ref, o_ref):
    kall = jnp.concatenate([km_ref[...], kp_ref[0], kc_ref[0], kn_ref[0]], axis=0)
    vall = jnp.concatenate([vm_ref[...], vp_ref[0], vc_ref[0], vn_ref[0]], axis=0)
    lane = lax.broadcasted_iota(jnp.int32, (A_BLOCK, LANES), 1)
    in_lo = lane < HEAD_DIM
    group = A_HEADS // A_KV_HEADS
    pair_out = []
    for p in range(4):
        qp = q_ref[0, :, p * LANES:(p + 1) * LANES].astype(F32) * SCALE
        qr = pltpu.roll(qp, HEAD_DIM, axis=1)
        res = []
        for half in range(2):
            hl = 2 * p + half
            kv_half = hl // group
            src = qp if half == kv_half else qr
            keep = in_lo if kv_half == 0 else jnp.logical_not(in_lo)
            qm = jnp.where(keep, src, 0.0).astype(BF16)
            s = _dot_nt(qm, kall) + bias_ref[0, hl]
            m = jnp.max(s, axis=-1, keepdims=True)
            e = jnp.exp(s - m)
            l = jnp.sum(e, axis=-1, keepdims=True)
            o = _dot(e.astype(BF16), vall) / l
            if half != kv_half:
                o = pltpu.roll(o, HEAD_DIM, axis=1)
            res.append(o)
        pair_out.append(jnp.where(in_lo, res[0], res[1]))
    o_ref[0] = jnp.concatenate(pair_out, axis=1).astype(o_ref.dtype)


def _window_attention(u, kmeta, vmeta, bias, bsz, seq):
    nblk = seq // A_BLOCK
    n_pairs = A_KV_HEADS // 2
    qw = A_WIDTH // n_pairs

    def variant(n):
        return jnp.where(n == 0, 0, jnp.where(n == nblk - 1, 2, 1))

    def kv_spec(col0, shift):
        return pl.BlockSpec(
            (1, A_BLOCK, LANES),
            lambda b, j, n: (b, jnp.clip(n + shift, 0, nblk - 1), col0 + j))

    return pl.pallas_call(
        _window_kernel,
        grid=(bsz, n_pairs, nblk),
        in_specs=[
            pl.BlockSpec((1, A_BLOCK, qw), lambda b, j, n: (b, n, j)),
            kv_spec(COL_KA, -1), kv_spec(COL_KA, 0), kv_spec(COL_KA, 1),
            kv_spec(COL_VA, -1), kv_spec(COL_VA, 0), kv_spec(COL_VA, 1),
            pl.BlockSpec((LANES, LANES), lambda b, j, n: (0, j)),
            pl.BlockSpec((LANES, LANES), lambda b, j, n: (0, j)),
            pl.BlockSpec((1, A_HEADS // n_pairs, A_BLOCK, 4 * LANES),
                         lambda b, j, n: (variant(n), j, 0, 0)),
        ],
        out_specs=pl.BlockSpec((1, A_BLOCK, qw), lambda b, j, n: (b, n, j)),
        out_shape=jax.ShapeDtypeStruct((bsz, seq, A_WIDTH), BF16),
        compiler_params=_cparams(("parallel", "parallel", "arbitrary")),
        name="window_attention",
    )(u, u, u, u, u, u, u, kmeta, vmeta, bias)


def _na_bias(rpb):
    qc = jnp.arange(GRID_W, dtype=jnp.int32)
    kc = jnp.arange(GRID_W, dtype=jnp.int32)
    cs = jnp.clip(qc - NA_COLS // 2, 0, GRID_W - NA_COLS)
    cmask = (kc[None, :] >= cs[:, None]) & (kc[None, :] < cs[:, None] + NA_COLS)
    dc = jnp.clip(kc[None, :] - qc[:, None] + NA_COLS - 1, 0, 2 * NA_COLS - 2)
    n_dc = 2 * NA_COLS - 1
    onehot = (dc[None] == jnp.arange(n_dc, dtype=jnp.int32)[:, None, None]).astype(F32)
    t = jnp.einsum('hrd,dqk->hrqk', rpb.astype(F32), onehot, precision=lax.Precision.HIGHEST)
    t = jnp.where(cmask[None, None], t, NEG)
    variants = []
    for oi in range(NA_ROWS):
        variants.append(jnp.concatenate(
            [t[:, i - oi + NA_ROWS - 1] for i in range(NA_ROWS)], axis=-1))
    return jnp.stack(variants, axis=1)


def _na_kernel(q_ref, k_ref, v_ref, km_ref, vm_ref, bias_ref, o_ref, *, rows):
    span = NA_ROWS * GRID_W
    lane = lax.broadcasted_iota(jnp.int32, (GRID_W, LANES), 1)
    in_lo = lane < HEAD_DIM
    meta_bias = jnp.where(lane < N_META, 0.0, NEG).astype(F32)
    km = km_ref[...]
    vm = vm_ref[...]

    def body(r, carry):
        rs = jnp.clip(r - NA_ROWS // 2, 0, rows - NA_ROWS)
        oi = r - rs
        q = q_ref[0, pl.ds(pl.multiple_of(r * GRID_W, GRID_W), GRID_W), :]
        k0 = pl.multiple_of(rs * GRID_W, GRID_W)
        ks = k_ref[0, pl.ds(k0, span), :]
        vs = v_ref[0, pl.ds(k0, span), :]
        res = []
        for hl in range(2):
            keep = in_lo if hl == 0 else jnp.logical_not(in_lo)
            qm = jnp.where(keep, q, jnp.zeros_like(q))
            sw = _dot_nt(qm, ks) * SCALE + bias_ref[hl, oi]
            sm = _dot_nt(qm, km) * SCALE + meta_bias
            m = jnp.maximum(jnp.max(sw, axis=-1, keepdims=True),
                            jnp.max(sm, axis=-1, keepdims=True))
            ew = jnp.exp(sw - m)
            em = jnp.exp(sm - m)
            l = jnp.sum(ew, axis=-1, keepdims=True) + jnp.sum(em, axis=-1, keepdims=True)
            o = _dot(ew.astype(BF16), vs) + _dot(em.astype(BF16), vm)
            res.append(o / l)
        out = jnp.where(in_lo, res[0], res[1])
        o_ref[0, pl.ds(pl.multiple_of(r * GRID_W, GRID_W), GRID_W), :] = out.astype(o_ref.dtype)
        return carry

    lax.fori_loop(0, rows, body, 0)


def _neighbourhood_attention(u, kmeta, vmeta, bias, bsz, seq):
    rows = seq // GRID_W
    assert rows >= NA_ROWS
    n_pairs = B_HEADS // 2

    def col_spec(col0):
        return pl.BlockSpec((1, seq, LANES), lambda b, j: (b, 0, col0 + j))

    return pl.pallas_call(
        functools.partial(_na_kernel, rows=rows),
        grid=(bsz, n_pairs),
        in_specs=[
            col_spec(COL_QB), col_spec(COL_KB), col_spec(COL_VB),
            pl.BlockSpec((LANES, LANES), lambda b, j: (0, j)),
            pl.BlockSpec((LANES, LANES), lambda b, j: (0, j)),
            pl.BlockSpec((2, NA_ROWS, GRID_W, NA_ROWS * GRID_W), lambda b, j: (j, 0, 0, 0)),
        ],
        out_specs=pl.BlockSpec((1, seq, LANES), lambda b, j: (b, 0, j)),
        out_shape=jax.ShapeDtypeStruct((bsz, seq, B_WIDTH), BF16),
        compiler_params=_cparams(("parallel", "parallel")),
        name="neighbourhood_attention",
    )(u, u, u, kmeta, vmeta, bias)


def _route(scores, rbias, prefix_fn):
    t = scores.shape[-1]
    ninf = -jnp.inf
    biased = scores + rbias
    b3 = biased.reshape(N_GROUPS, GROUP_SIZE, t)
    s3 = scores.reshape(N_GROUPS, GROUP_SIZE, t)
    io_in = lax.broadcasted_iota(jnp.int32, b3.shape, 1)
    io_g3 = lax.broadcasted_iota(jnp.int32, b3.shape, 0)
    io_e = io_g3 * GROUP_SIZE + io_in
    m1 = jnp.max(b3, axis=1, keepdims=True)
    i1 = jnp.min(jnp.where(b3 == m1, io_in, GROUP_SIZE), axis=1, keepdims=True)
    m2 = jnp.max(jnp.where(io_in == i1, ninf, b3), axis=1, keepdims=True)
    gs = m1 + m2
    io_g = lax.broadcasted_iota(jnp.int32, gs.shape, 0)
    gmask = jnp.zeros(gs.shape, jnp.bool_)
    cur = gs
    for _ in range(TOPK_GROUPS):
        m = jnp.max(cur, axis=0, keepdims=True)
        i = jnp.min(jnp.where(cur == m, io_g, N_GROUPS), axis=0, keepdims=True)
        pick = io_g == i
        gmask = jnp.logical_or(gmask, pick)
        cur = jnp.where(pick, ninf, cur)
    cur = jnp.where(gmask, b3, ninf)
    picks, top_e, top_s = [], [], []
    for _ in range(TOP_K):
        m = jnp.max(jnp.max(cur, axis=1, keepdims=True), axis=0, keepdims=True)
        i = jnp.min(jnp.min(jnp.where(cur == m, io_e, N_EXPERTS), axis=1, keepdims=True),
                    axis=0, keepdims=True)
        pick = io_e == i
        picks.append(pick)
        top_e.append(i.reshape(1, t))
        w = jnp.sum(jnp.sum(jnp.where(pick, s3, 0.0), axis=1, keepdims=True), axis=0, keepdims=True)
        top_s.append(w.reshape(1, t))
        cur = jnp.where(pick, ninf, cur)
    top_e = jnp.concatenate(top_e, axis=0)
    top_s = jnp.concatenate(top_s, axis=0)
    denom = top_s[0:1]
    for k in range(1, TOP_K):
        denom = denom + top_s[k:k + 1]
    top_w = top_s / (denom + 1e-20) * ROUTED_SCALE
    sel = picks[0]
    for k in range(1, TOP_K):
        sel = jnp.logical_or(sel, picks[k])
    sel = jnp.where(sel, 1.0, 0.0).astype(F32).reshape(N_EXPERTS, t)
    return top_e, top_w, sel, picks


def _outproj_router_kernel(ya_ref, yb_ref, x_ref, wo_ref, ga_ref, gb_ref, eg_ref, eb_ref,
                           g1_ref, b1_ref, wrh_ref, wrl_ref, rb_ref,
                           h1_ref, te_ref, tw_ref, tr_ref, cnt_ref, carry):
    tm = x_ref.shape[0]

    @pl.when(pl.program_id(0) == 0)
    def _():
        carry[...] = jnp.zeros_like(carry)

    def rms(y_ref, g_ref):
        y = y_ref[...].astype(F32)
        inv = lax.rsqrt(jnp.mean(y * y, axis=-1, keepdims=True) + RMS_EPS)
        return (y * inv * g_ref[...]).astype(BF16)

    mix = (_dot(rms(ya_ref, ga_ref), wo_ref[0:A_WIDTH, :])
           + _dot(rms(yb_ref, gb_ref), wo_ref[A_WIDTH:MIX_WIDTH, :]))
    h = _layer_norm(x_ref[...], eg_ref[...], eb_ref[...])
    h1 = _layer_norm(DEEPNORM_ALPHA * h + mix, g1_ref[...], b1_ref[...])
    _to_chunks(h1_ref, h1)
    hb = h1.astype(BF16)

    hlo = (h1 - hb.astype(F32)).astype(BF16)
    logits = (_dot_nt(wrh_ref[...], hb) + _dot_nt(wrh_ref[...], hlo)
              + _dot_nt(wrl_ref[...], hb))
    scores = 1.0 / (1.0 + jnp.exp(-logits))
    top_e, top_w, sel, picks = _route(scores, rb_ref[...], None)

    row = lax.broadcasted_iota(jnp.int32, (tm, tm), 0)
    col = lax.broadcasted_iota(jnp.int32, (tm, tm), 1)
    before = jnp.where(row < col, 1.0, 0.0).astype(BF16)
    rank = _dot(sel.astype(BF16), before) + carry[...]
    rank3 = rank.reshape(N_GROUPS, GROUP_SIZE, tm)
    ranks = []
    for k in range(TOP_K):
        rk = jnp.sum(jnp.sum(jnp.where(picks[k], rank3, 0.0), axis=1, keepdims=True),
                     axis=0, keepdims=True)
        ranks.append(rk.reshape(1, tm))
    te_ref[...] = top_e
    tw_ref[...] = top_w
    tr_ref[...] = jnp.concatenate(ranks, axis=0).astype(jnp.int32)
    carry[...] = carry[...] + jnp.sum(sel, axis=-1, keepdims=True)
    cnt_ref[...] = jnp.broadcast_to(carry[...], cnt_ref.shape)


def _outproj_router(ya, yb, x2, wo_bf16, ga, gb, eg, eb, g1, b1, wrh, wrl, rb, tm):
    m = x2.shape[0]
    row = lambda w: pl.BlockSpec((tm, w), lambda i: (i, 0))
    full = lambda a: pl.BlockSpec(a.shape, lambda i: (0,) * a.ndim)
    tok = pl.BlockSpec((TOP_K, tm), lambda i: (0, i))
    return pl.pallas_call(
        _outproj_router_kernel,
        grid=(m // tm,),
        in_specs=[row(A_WIDTH), row(B_WIDTH), row(D_MODEL), full(wo_bf16), full(ga), full(gb),
                  full(eg), full(eb), full(g1), full(b1), full(wrh), full(wrl), full(rb)],
        out_specs=[pl.BlockSpec((tm, D_CHUNKS, LANES), lambda i: (i, 0, 0)), tok, tok, tok,
                   pl.BlockSpec((N_EXPERTS, LANES), lambda i: (0, 0))],
        out_shape=[jax.ShapeDtypeStruct((m, D_CHUNKS, LANES), F32),
                   jax.ShapeDtypeStruct((TOP_K, m), jnp.int32),
                   jax.ShapeDtypeStruct((TOP_K, m), F32),
                   jax.ShapeDtypeStruct((TOP_K, m), jnp.int32),
                   jax.ShapeDtypeStruct((N_EXPERTS, LANES), F32)],
        scratch_shapes=[pltpu.VMEM((N_EXPERTS, 1), F32)],
        compiler_params=_cparams(("arbitrary",)),
        name="outproj_router",
    )(ya, yb, x2, wo_bf16, ga, gb, eg, eb, g1, b1, wrh, wrl, rb)


def _moe_kernel(be_ref, nu_ref, tok_ref, tok_next_ref, h_hbm, wg_ref, wu_ref, wd_ref, y_ref,
                xbuf0, xbuf1, sem, wg_b, wu_b, wd_b):
    i = pl.program_id(0)
    n_used = nu_ref[0]
    even = i % 2 == 0
    e = be_ref[i]
    prev = be_ref[jnp.maximum(i - 1, 0)]
    bufs = (xbuf0, xbuf1)

    def start_gather(idx_ref, b):
        for r in range(MOE_BLOCK):
            pltpu.make_async_copy(h_hbm.at[idx_ref[0, 0, r]], bufs[b].at[r], sem.at[b]).start()

    def wait_gather(b):
        pltpu.make_async_copy(h_hbm.at[pl.ds(0, MOE_BLOCK)], bufs[b], sem.at[b]).wait()

    @pl.when(i == 0)
    def _():
        start_gather(tok_ref, 0)

    @pl.when(jnp.logical_or(i == 0, e != prev))
    def _():
        wg_b[...] = wg_ref[0].astype(BF16)
        wu_b[...] = wu_ref[0].astype(BF16)
        wd_b[...] = wd_ref[0].astype(BF16)

    def step(b):
        wait_gather(b)
        start_gather(tok_next_ref, 1 - b)
        x = _from_chunks(bufs[b]).astype(BF16)
        g = _dot(x, wg_b[...])
        u = _dot(x, wu_b[...])
        y_ref[...] = _dot((_silu(g) * u).astype(BF16), wd_b[...]).astype(y_ref.dtype)

    for b in range(2):
        parity = even if b == 0 else jnp.logical_not(even)

        @pl.when(jnp.logical_and(i < n_used, parity))
        def _():
            step(b)

        @pl.when(jnp.logical_and(i == n_used, parity))
        def _():
            wait_gather(b)

    @pl.when(i >= n_used)
    def _():
        y_ref[...] = jnp.zeros_like(y_ref)


def _moe_experts(block_e, n_used, slot_tok, h1, w_gate, w_up, w_down):
    n_blocks = slot_tok.shape[0]
    p = n_blocks * MOE_BLOCK
    assert (h1.shape[0] * TOP_K + N_EXPERTS * (MOE_BLOCK - 1)) // MOE_BLOCK < n_blocks
    idx_spec = lambda shift: pl.BlockSpec(
        (1, 1, MOE_BLOCK), lambda i, be, nu: (jnp.minimum(i + shift, n_blocks - 1), 0, 0),
        memory_space=pltpu.SMEM)
    return pl.pallas_call(
        _moe_kernel,
        grid_spec=pltpu.PrefetchScalarGridSpec(
            num_scalar_prefetch=2,
            grid=(n_blocks,),
            in_specs=[
                idx_spec(0), idx_spec(1),
                pl.BlockSpec(memory_space=pl.ANY),
                pl.BlockSpec((1, D_MODEL, D_EXPERT), lambda i, be, nu: (be[i], 0, 0)),
                pl.BlockSpec((1, D_MODEL, D_EXPERT), lambda i, be, nu: (be[i], 0, 0)),
                pl.BlockSpec((1, D_EXPERT, D_MODEL), lambda i, be, nu: (be[i], 0, 0)),
            ],
            out_specs=pl.BlockSpec((MOE_BLOCK, D_MODEL), lambda i, be, nu: (i, 0)),
            scratch_shapes=[pltpu.VMEM((MOE_BLOCK, D_CHUNKS, LANES), F32),
                            pltpu.VMEM((MOE_BLOCK, D_CHUNKS, LANES), F32),
                            pltpu.SemaphoreType.DMA((2,)),
                            pltpu.VMEM((D_MODEL, D_EXPERT), BF16),
                            pltpu.VMEM((D_MODEL, D_EXPERT), BF16),
                            pltpu.VMEM((D_EXPERT, D_MODEL), BF16)],
        ),
        out_shape=jax.ShapeDtypeStruct((p, D_MODEL), BF16),
        compiler_params=pltpu.CompilerParams(
            dimension_semantics=("arbitrary",), vmem_limit_bytes=VMEM_LIMIT,
            disable_bounds_checks=True),
        name="moe_experts",
    )(block_e, n_used, slot_tok, slot_tok, h1, w_gate, w_up, w_down)


def _final_kernel(h1_ref, ys_ref, tw_ref, wsg_ref, wsu_ref, wsd_ref, g2_ref, b2_ref, o_ref):
    h1 = _from_chunks(h1_ref)
    hb = h1.astype(BF16)
    g = _dot(hb, wsg_ref[...])
    u = _dot(hb, wsu_ref[...])
    ffn = _dot((_silu(g) * u).astype(BF16), wsd_ref[...])
    for k in range(TOP_K):
        ffn = ffn + ys_ref[k].astype(F32) * tw_ref[:, k:k + 1]
    o_ref[...] = _layer_norm(DEEPNORM_ALPHA * h1 + ffn, g2_ref[...], b2_ref[...])


def _final(h1, ysel, tw_t, wsg, wsu, wsd, g2, b2, tm):
    m = h1.shape[0]
    row = lambda w: pl.BlockSpec((tm, w), lambda i: (i, 0))
    full = lambda a: pl.BlockSpec(a.shape, lambda i: (0,) * a.ndim)
    return pl.pallas_call(
        _final_kernel,
        grid=(m // tm,),
        in_specs=[pl.BlockSpec((tm, D_CHUNKS, LANES), lambda i: (i, 0, 0)),
                  pl.BlockSpec((TOP_K, tm, D_MODEL), lambda i: (0, i, 0)),
                  row(TOP_K), full(wsg), full(wsu), full(wsd), full(g2), full(b2)],
        out_specs=row(D_MODEL),
        out_shape=jax.ShapeDtypeStruct((m, D_MODEL), F32),
        compiler_params=_cparams(("parallel",)),
        name="shared_combine_ln2",
    )(h1, ysel, tw_t, wsg, wsu, wsd, g2, b2)


def _pad_rows(a, rows):
    return jnp.concatenate([a, jnp.zeros((rows - a.shape[0],) + a.shape[1:], a.dtype)], axis=0)


def kernel(x, meta_tokens, ln_emb_g, ln_emb_b, t5_table, w_in, a_sink, na_rpb, g_norm_a, g_norm_b, w_out, ln1_g, ln1_b, w_router, router_bias, w_gate, w_up, w_down, ws_gate, ws_up, ws_down, ln2_g, ln2_b):
    bsz, seq, _ = x.shape
    m = bsz * seq
    r2 = lambda a: a.reshape(1, -1).astype(F32)
    x2 = x.reshape(m, D_MODEL)
    eg, eb = r2(ln_emb_g), r2(ln_emb_b)

    w_in_b = w_in[0].astype(BF16)
    tm = 512 if m % 512 == 0 else 128
    u = _ln_inproj(x2, eg, eb, w_in_b, tm, 1152).reshape(bsz, seq, IN_WIDTH)
    um = _ln_inproj(meta_tokens.astype(F32), eg, eb, w_in_b, N_META, 1152)
    cut = lambda c0, width: _pad_rows(um[:, c0 * LANES:c0 * LANES + width], LANES)

    ya = _window_attention(u, cut(COL_KA, KV_WIDTH), cut(COL_VA, KV_WIDTH),
                           _window_bias(t5_table, a_sink[0], seq), bsz, seq)
    yb = _neighbourhood_attention(u, cut(COL_KB, B_WIDTH), cut(COL_VB, B_WIDTH),
                                  _na_bias(na_rpb[0]), bsz, seq)

    wr_t = w_router[0].astype(F32).T
    wr_hi = wr_t.astype(BF16)
    wr_lo = (wr_t - wr_hi.astype(F32)).astype(BF16)
    tm4 = 256 if m % 256 == 0 else 128
    h1, top_e, top_w, top_r, cnt = _outproj_router(
        ya.reshape(m, A_WIDTH), yb.reshape(m, B_WIDTH), x2, w_out[0].astype(BF16),
        r2(g_norm_a), r2(g_norm_b), eg, eb, r2(ln1_g), r2(ln1_b), wr_hi, wr_lo,
        router_bias[0].astype(F32).reshape(N_EXPERTS, 1), tm4)

    counts = cnt[:, 0].astype(jnp.int32)
    nb_e = (counts + MOE_BLOCK - 1) // MOE_BLOCK
    bend = jnp.cumsum(nb_e)
    pstart = (bend - nb_e) * MOE_BLOCK
    n_blocks = (m * TOP_K) // MOE_BLOCK + N_EXPERTS
    p_rows = n_blocks * MOE_BLOCK
    blk = jnp.arange(n_blocks, dtype=jnp.int32)
    block_e = jnp.minimum(jnp.sum((bend[None, :] <= blk[:, None]).astype(jnp.int32), axis=1),
                          N_EXPERTS - 1)
    n_used = bend[-1:].astype(jnp.int32)
    expert_ids = jnp.arange(N_EXPERTS, dtype=jnp.int32)[:, None, None]
    dest = top_r + jnp.sum(jnp.where(top_e[None] == expert_ids, pstart[:, None, None], 0),
                           axis=0)
    tok = jnp.broadcast_to(jnp.arange(m, dtype=jnp.int32)[None, :], (TOP_K, m))
    slot_tok = jnp.zeros((p_rows,), jnp.int32).at[dest.reshape(-1)].set(
        tok.reshape(-1), unique_indices=True, mode='promise_in_bounds')

    y_sorted = _moe_experts(block_e, n_used, slot_tok.reshape(n_blocks, 1, MOE_BLOCK), h1,
                            w_gate[0], w_up[0], w_down[0])
    ysel = jnp.take(y_sorted, dest.reshape(-1), axis=0).reshape(TOP_K, m, D_MODEL)

    out = _final(h1, ysel, top_w.T, ws_gate[0].astype(BF16), ws_up[0].astype(BF16),
                 ws_down[0].astype(BF16), r2(ln2_g), r2(ln2_b), tm4)
    return out.reshape(bsz, seq, D_MODEL)
```

```python
import functools
import math

import jax
import jax.numpy as jnp
from jax import lax
from jax.experimental import pallas as pl
from jax.experimental.pallas import tpu as pltpu

F32 = jnp.float32
BF16 = jnp.bfloat16

D_MODEL = 2048
HEAD_DIM = 64
N_META = 16
GRID_W = 64
A_HEADS = 16
A_KV_HEADS = 4
A_WINDOW = 128
A_BLOCK = 128
N_BUCKETS = 32
MAX_DISTANCE = 128
B_HEADS = 16
NA_ROWS = 8
NA_COLS = 16
A_WIDTH = A_HEADS * HEAD_DIM
KV_WIDTH = A_KV_HEADS * HEAD_DIM
B_WIDTH = B_HEADS * HEAD_DIM
MIX_WIDTH = A_WIDTH + B_WIDTH
IN_WIDTH = A_WIDTH + 2 * KV_WIDTH + 3 * B_WIDTH
N_EXPERTS = 64
TOP_K = 8
N_GROUPS = 8
GROUP_SIZE = N_EXPERTS // N_GROUPS
TOPK_GROUPS = 4
D_EXPERT = 512
D_SHARED = 512
ROUTED_SCALE = 2.5
DEPTH = 1
DEEPNORM_ALPHA = (2 * DEPTH) ** 0.25
LN_EPS = 1e-5
RMS_EPS = 1e-6
NEG = -1e30
SCALE = HEAD_DIM ** -0.5

LANES = 128
VMEM_LIMIT = 56 * 1024 * 1024

COL_QA = 0
COL_KA = A_WIDTH // LANES
COL_VA = (A_WIDTH + KV_WIDTH) // LANES
COL_QB = (A_WIDTH + 2 * KV_WIDTH) // LANES
COL_KB = COL_QB + B_WIDTH // LANES
COL_VB = COL_KB + B_WIDTH // LANES

WIN_BATCH = 4
NA_UNROLL = 4
MOE_BLOCK = 256
D_CHUNKS = D_MODEL // LANES


def _to_chunks(ref, x):
    for s in range(D_CHUNKS):
        ref[:, s, :] = x[:, s * LANES:(s + 1) * LANES]


def _from_chunks(ref):
    return jnp.concatenate([ref[:, s, :] for s in range(D_CHUNKS)], axis=1)


def _cparams(sem):
    return pltpu.CompilerParams(dimension_semantics=sem, vmem_limit_bytes=VMEM_LIMIT)


def _layer_norm(x, g, b):
    mu = jnp.mean(x, axis=-1, keepdims=True)
    xc = x - mu
    var = jnp.mean(xc * xc, axis=-1, keepdims=True)
    return xc * lax.rsqrt(var + LN_EPS) * g + b


def _dot(a, b):
    return jnp.dot(a, b, preferred_element_type=F32)


def _dot_nt(a, b):
    return lax.dot_general(a, b, (((1,), (1,)), ((), ())), preferred_element_type=F32)


def _silu(g):
    return g / (1.0 + jnp.exp(-g))


def _ln_inproj_kernel(x_ref, g_ref, b_ref, w_ref, o_ref, h_scr):
    @pl.when(pl.program_id(1) == 0)
    def _():
        h_scr[...] = _layer_norm(x_ref[...], g_ref[...], b_ref[...]).astype(BF16)

    o_ref[...] = _dot(h_scr[...], w_ref[...]).astype(o_ref.dtype)


def _ln_inproj(x2, g, b, w_bf16, tm, tn):
    m = x2.shape[0]
    n = w_bf16.shape[1]
    return pl.pallas_call(
        _ln_inproj_kernel,
        grid=(m // tm, n // tn),
        in_specs=[
            pl.BlockSpec((tm, D_MODEL), lambda i, j: (i, 0)),
            pl.BlockSpec((1, D_MODEL), lambda i, j: (0, 0)),
            pl.BlockSpec((1, D_MODEL), lambda i, j: (0, 0)),
            pl.BlockSpec((D_MODEL, tn), lambda i, j: (0, j)),
        ],
        out_specs=pl.BlockSpec((tm, tn), lambda i, j: (i, j)),
        out_shape=jax.ShapeDtypeStruct((m, n), BF16),
        scratch_shapes=[pltpu.VMEM((tm, D_MODEL), BF16)],
        compiler_params=_cparams(("parallel", "arbitrary")),
        name="ln_inproj",
    )(x2, g, b, w_bf16)


def _t5_bucket(rel):
    nb = N_BUCKETS // 2
    max_exact = nb // 2
    ret = jnp.where(rel > 0, nb, 0)
    n = jnp.abs(rel)
    nf = jnp.maximum(n, 1).astype(F32)
    large = max_exact + (jnp.log(nf / max_exact) / math.log(MAX_DISTANCE / max_exact)
                         * (nb - max_exact)).astype(jnp.int32)
    large = jnp.minimum(large, nb - 1)
    return ret + jnp.where(n < max_exact, n, large)


def _lookup(table_t, idx, n):
    onehot = (idx[None] == jnp.arange(n, dtype=jnp.int32).reshape((n,) + (1,) * idx.ndim))
    return jnp.einsum('hb,b...->h...', table_t, onehot.astype(F32),
                      precision=lax.Precision.HIGHEST)


def _window_bias(t5_table, a_sink, seq):
    assert N_META + A_BLOCK - (N_META - 1) > MAX_DISTANCE
    nblk = seq // A_BLOCK
    t5_t = t5_table.astype(F32).T
    q_loc = jnp.arange(A_BLOCK, dtype=jnp.int32)
    k_loc = jnp.arange(3 * A_BLOCK, dtype=jnp.int32) - A_BLOCK
    rel = k_loc[None, :] - q_loc[:, None]
    band = _lookup(t5_t, _t5_bucket(rel), N_BUCKETS)
    win = jnp.abs(rel) <= A_WINDOW
    sink = jnp.broadcast_to(a_sink.astype(F32)[:, None, None], (A_HEADS, A_BLOCK, 1))
    pad = jnp.full((A_HEADS, A_BLOCK, LANES - N_META - 1), NEG, F32)
    out = []
    for blk in (0, min(1, nblk - 1), nblk - 1):
        gk = blk * A_BLOCK + k_loc
        valid = win & (gk >= 0)[None, :] & (gk < seq)[None, :]
        band_v = jnp.where(valid[None], band, NEG)
        q_pos = N_META + blk * A_BLOCK + q_loc
        rel_m = jnp.arange(N_META, dtype=jnp.int32)[None, :] - q_pos[:, None]
        bias_m = _lookup(t5_t, _t5_bucket(rel_m), N_BUCKETS)
        out.append(jnp.concatenate([bias_m, sink, pad, band_v], axis=-1))
    return jnp.stack(out, axis=0)


def _window_kernel(q_ref, kp_ref, kc_ref, kn_ref, vp_ref, vc_ref, vn_ref, km_ref, vm_ref,
                   bias_ref, o_ref):
    kall = jnp.concatenate([km_ref[...], kp_ref[0], kc_ref[0], kn_ref[0]], axis=0)
    vall = jnp.concatenate([vm_ref[...], vp_ref[0], vc_ref[0], vn_ref[0]], axis=0)
    lane = lax.broadcasted_iota(jnp.int32, (A_BLOCK, LANES), 1)
    in_lo = lane < HEAD_DIM
    group = A_HEADS // A_KV_HEADS
    n_heads = 2 * group
    pair_out = []
    for h0 in range(0, n_heads, WIN_BATCH):
        scores = []
        for hl in range(h0, h0 + WIN_BATCH):
            p, half = hl // 2, hl % 2
            kv_half = hl // group
            qp = q_ref[0, :, p * LANES:(p + 1) * LANES].astype(F32) * SCALE
            src = qp if half == kv_half else pltpu.roll(qp, HEAD_DIM, axis=1)
            keep = in_lo if kv_half == 0 else jnp.logical_not(in_lo)
            qm = jnp.where(keep, src, 0.0).astype(BF16)
            scores.append(_dot_nt(qm, kall))
        probs = []
        for j, s in enumerate(scores):
            s = s + bias_ref[0, h0 + j]
            m = jnp.max(s, axis=-1, keepdims=True)
            e = jnp.exp(s - m)
            probs.append((e.astype(BF16), jnp.sum(e, axis=-1, keepdims=True)))
        res = []
        for j, (e, l) in enumerate(probs):
            hl = h0 + j
            o = _dot(e, vall) / l
            if hl % 2 != hl // group:
                o = pltpu.roll(o, HEAD_DIM, axis=1)
            res.append(o)
        for j in range(0, WIN_BATCH, 2):
            pair_out.append(jnp.where(in_lo, res[j], res[j + 1]))
    o_ref[0] = jnp.concatenate(pair_out, axis=1).astype(o_ref.dtype)


def _window_attention(u, kmeta, vmeta, bias, bsz, seq):
    nblk = seq // A_BLOCK
    n_pairs = A_KV_HEADS // 2
    qw = A_WIDTH // n_pairs

    def variant(n):
        return jnp.where(n == 0, 0, jnp.where(n == nblk - 1, 2, 1))

    def kv_spec(col0, shift):
        return pl.BlockSpec(
            (1, A_BLOCK, LANES),
            lambda b, j, n: (b, jnp.clip(n + shift, 0, nblk - 1), col0 + j))

    return pl.pallas_call(
        _window_kernel,
        grid=(bsz, n_pairs, nblk),
        in_specs=[
            pl.BlockSpec((1, A_BLOCK, qw), lambda b, j, n: (b, n, j)),
            kv_spec(COL_KA, -1), kv_spec(COL_KA, 0), kv_spec(COL_KA, 1),
            kv_spec(COL_VA, -1), kv_spec(COL_VA, 0), kv_spec(COL_VA, 1),
            pl.BlockSpec((LANES, LANES), lambda b, j, n: (0, j)),
            pl.BlockSpec((LANES, LANES), lambda b, j, n: (0, j)),
            pl.BlockSpec((1, A_HEADS // n_pairs, A_BLOCK, 4 * LANES),
                         lambda b, j, n: (variant(n), j, 0, 0)),
        ],
        out_specs=pl.BlockSpec((1, A_BLOCK, qw), lambda b, j, n: (b, n, j)),
        out_shape=jax.ShapeDtypeStruct((bsz, seq, A_WIDTH), BF16),
        compiler_params=_cparams(("parallel", "parallel", "arbitrary")),
        name="window_attention",
    )(u, u, u, u, u, u, u, kmeta, vmeta, bias)


def _na_bias(rpb):
    qc = jnp.arange(GRID_W, dtype=jnp.int32)
    kc = jnp.arange(GRID_W, dtype=jnp.int32)
    cs = jnp.clip(qc - NA_COLS // 2, 0, GRID_W - NA_COLS)
    cmask = (kc[None, :] >= cs[:, None]) & (kc[None, :] < cs[:, None] + NA_COLS)
    dc = jnp.clip(kc[None, :] - qc[:, None] + NA_COLS - 1, 0, 2 * NA_COLS - 2)
    n_dc = 2 * NA_COLS - 1
    onehot = (dc[None] == jnp.arange(n_dc, dtype=jnp.int32)[:, None, None]).astype(F32)
    t = jnp.einsum('hrd,dqk->hrqk', rpb.astype(F32), onehot, precision=lax.Precision.HIGHEST)
    t = jnp.where(cmask[None, None], t, NEG)
    variants = []
    for oi in range(NA_ROWS):
        variants.append(jnp.concatenate(
            [t[:, i - oi + NA_ROWS - 1] for i in range(NA_ROWS)], axis=-1))
    return jnp.stack(variants, axis=1)


def _na_kernel(q_ref, k_ref, v_ref, km_ref, vm_ref, bias_ref, o_ref, *, rows):
    span = NA_ROWS * GRID_W
    lane = lax.broadcasted_iota(jnp.int32, (GRID_W, LANES), 1)
    in_lo = lane < HEAD_DIM
    meta_bias = jnp.where(lane < N_META, 0.0, NEG).astype(F32)
    km = km_ref[...]
    vm = vm_ref[...]

    def body(it, carry):
        work = []
        for j in range(NA_UNROLL):
            r = it * NA_UNROLL + j
            rs = jnp.clip(r - NA_ROWS // 2, 0, rows - NA_ROWS)
            oi = r - rs
            q0 = pl.multiple_of(r * GRID_W, GRID_W)
            k0 = pl.multiple_of(rs * GRID_W, GRID_W)
            q = q_ref[0, pl.ds(q0, GRID_W), :]
            ks = k_ref[0, pl.ds(k0, span), :]
            for hl in range(2):
                keep = in_lo if hl == 0 else jnp.logical_not(in_lo)
                qm = jnp.where(keep, q, jnp.zeros_like(q))
                work.append((_dot_nt(qm, ks), _dot_nt(qm, km), hl, oi, k0, q0))
        probs = []
        for sw, sm, hl, oi, k0, q0 in work:
            sw = sw * SCALE + bias_ref[hl, oi]
            sm = sm * SCALE + meta_bias
            m = jnp.maximum(jnp.max(sw, axis=-1, keepdims=True),
                            jnp.max(sm, axis=-1, keepdims=True))
            ew = jnp.exp(sw - m)
            em = jnp.exp(sm - m)
            l = jnp.sum(ew, axis=-1, keepdims=True) + jnp.sum(em, axis=-1, keepdims=True)
            probs.append((ew.astype(BF16), em.astype(BF16), l, k0, q0))
        res = []
        for ew, em, l, k0, q0 in probs:
            vs = v_ref[0, pl.ds(k0, span), :]
            res.append((_dot(ew, vs) + _dot(em, vm)) / l)
        for j in range(NA_UNROLL):
            q0 = probs[2 * j][4]
            out = jnp.where(in_lo, res[2 * j], res[2 * j + 1])
            o_ref[0, pl.ds(q0, GRID_W), :] = out.astype(o_ref.dtype)
        return carry

    lax.fori_loop(0, rows // NA_UNROLL, body, 0)


def _neighbourhood_attention(u, kmeta, vmeta, bias, bsz, seq):
    rows = seq // GRID_W
    assert rows >= NA_ROWS and rows % NA_UNROLL == 0
    n_pairs = B_HEADS // 2

    def col_spec(col0):
        return pl.BlockSpec((1, seq, LANES), lambda b, j: (b, 0, col0 + j))

    return pl.pallas_call(
        functools.partial(_na_kernel, rows=rows),
        grid=(bsz, n_pairs),
        in_specs=[
            col_spec(COL_QB), col_spec(COL_KB), col_spec(COL_VB),
            pl.BlockSpec((LANES, LANES), lambda b, j: (0, j)),
            pl.BlockSpec((LANES, LANES), lambda b, j: (0, j)),
            pl.BlockSpec((2, NA_ROWS, GRID_W, NA_ROWS * GRID_W), lambda b, j: (j, 0, 0, 0)),
        ],
        out_specs=pl.BlockSpec((1, seq, LANES), lambda b, j: (b, 0, j)),
        out_shape=jax.ShapeDtypeStruct((bsz, seq, B_WIDTH), BF16),
        compiler_params=_cparams(("parallel", "parallel")),
        name="neighbourhood_attention",
    )(u, u, u, kmeta, vmeta, bias)


def _route(scores, rbias, prefix_fn):
    t = scores.shape[-1]
    ninf = -jnp.inf
    biased = scores + rbias
    b3 = biased.reshape(N_GROUPS, GROUP_SIZE, t)
    s3 = scores.reshape(N_GROUPS, GROUP_SIZE, t)
    io_in = lax.broadcasted_iota(jnp.int32, b3.shape, 1)
    io_g3 = lax.broadcasted_iota(jnp.int32, b3.shape, 0)
    io_e = io_g3 * GROUP_SIZE + io_in
    m1 = jnp.max(b3, axis=1, keepdims=True)
    i1 = jnp.min(jnp.where(b3 == m1, io_in, GROUP_SIZE), axis=1, keepdims=True)
    m2 = jnp.max(jnp.where(io_in == i1, ninf, b3), axis=1, keepdims=True)
    gs = m1 + m2
    io_g = lax.broadcasted_iota(jnp.int32, gs.shape, 0)
    gmask = jnp.zeros(gs.shape, jnp.bool_)
    cur = gs
    for _ in range(TOPK_GROUPS):
        m = jnp.max(cur, axis=0, keepdims=True)
        i = jnp.min(jnp.where(cur == m, io_g, N_GROUPS), axis=0, keepdims=True)
        pick = io_g == i
        gmask = jnp.logical_or(gmask, pick)
        cur = jnp.where(pick, ninf, cur)
    cur = jnp.where(gmask, b3, ninf)
    picks, top_e, top_s = [], [], []
    for _ in range(TOP_K):
        m = jnp.max(jnp.max(cur, axis=1, keepdims=True), axis=0, keepdims=True)
        i = jnp.min(jnp.min(jnp.where(cur == m, io_e, N_EXPERTS), axis=1, keepdims=True),
                    axis=0, keepdims=True)
        pick = io_e == i
        picks.append(pick)
        top_e.append(i.reshape(1, t))
        w = jnp.sum(jnp.sum(jnp.where(pick, s3, 0.0), axis=1, keepdims=True), axis=0, keepdims=True)
        top_s.append(w.reshape(1, t))
        cur = jnp.where(pick, ninf, cur)
    top_e = jnp.concatenate(top_e, axis=0)
    top_s = jnp.concatenate(top_s, axis=0)
    denom = top_s[0:1]
    for k in range(1, TOP_K):
        denom = denom + top_s[k:k + 1]
    top_w = top_s / (denom + 1e-20) * ROUTED_SCALE
    sel = picks[0]
    for k in range(1, TOP_K):
        sel = jnp.logical_or(sel, picks[k])
    sel = jnp.where(sel, 1.0, 0.0).astype(F32).reshape(N_EXPERTS, t)
    return top_e, top_w, sel, picks


def _outproj_router_kernel(ya_ref, yb_ref, x_ref, wo_ref, ga_ref, gb_ref, eg_ref, eb_ref,
                           g1_ref, b1_ref, wrh_ref, wrl_ref, rb_ref,
                           h1_ref, te_ref, tw_ref, tr_ref, cnt_ref, carry):
    tm = x_ref.shape[0]

    @pl.when(pl.program_id(0) == 0)
    def _():
        carry[...] = jnp.zeros_like(carry)

    def rms(y_ref, g_ref):
        y = y_ref[...].astype(F32)
        inv = lax.rsqrt(jnp.mean(y * y, axis=-1, keepdims=True) + RMS_EPS)
        return (y * inv * g_ref[...]).astype(BF16)

    mix = (_dot(rms(ya_ref, ga_ref), wo_ref[0:A_WIDTH, :])
           + _dot(rms(yb_ref, gb_ref), wo_ref[A_WIDTH:MIX_WIDTH, :]))
    h = _layer_norm(x_ref[...], eg_ref[...], eb_ref[...])
    h1 = _layer_norm(DEEPNORM_ALPHA * h + mix, g1_ref[...], b1_ref[...])
    _to_chunks(h1_ref, h1)
    hb = h1.astype(BF16)

    hlo = (h1 - hb.astype(F32)).astype(BF16)
    logits = (_dot_nt(wrh_ref[...], hb) + _dot_nt(wrh_ref[...], hlo)
              + _dot_nt(wrl_ref[...], hb))
    scores = 1.0 / (1.0 + jnp.exp(-logits))
    top_e, top_w, sel, picks = _route(scores, rb_ref[...], None)

    row = lax.broadcasted_iota(jnp.int32, (tm, tm), 0)
    col = lax.broadcasted_iota(jnp.int32, (tm, tm), 1)
    before = jnp.where(row < col, 1.0, 0.0).astype(BF16)
    rank = _dot(sel.astype(BF16), before) + carry[...]
    rank3 = rank.reshape(N_GROUPS, GROUP_SIZE, tm)
    ranks = []
    for k in range(TOP_K):
        rk = jnp.sum(jnp.sum(jnp.where(picks[k], rank3, 0.0), axis=1, keepdims=True),
                     axis=0, keepdims=True)
        ranks.append(rk.reshape(1, tm))
    te_ref[...] = top_e
    tw_ref[...] = top_w
    tr_ref[...] = jnp.concatenate(ranks, axis=0).astype(jnp.int32)
    carry[...] = carry[...] + jnp.sum(sel, axis=-1, keepdims=True)
    cnt_ref[...] = jnp.broadcast_to(carry[...], cnt_ref.shape)


def _outproj_router(ya, yb, x2, wo_bf16, ga, gb, eg, eb, g1, b1, wrh, wrl, rb, tm):
    m = x2.shape[0]
    row = lambda w: pl.BlockSpec((tm, w), lambda i: (i, 0))
    full = lambda a: pl.BlockSpec(a.shape, lambda i: (0,) * a.ndim)
    tok = pl.BlockSpec((TOP_K, tm), lambda i: (0, i))
    return pl.pallas_call(
        _outproj_router_kernel,
        grid=(m // tm,),
        in_specs=[row(A_WIDTH), row(B_WIDTH), row(D_MODEL), full(wo_bf16), full(ga), full(gb),
                  full(eg), full(eb), full(g1), full(b1), full(wrh), full(wrl), full(rb)],
        out_specs=[pl.BlockSpec((tm, D_CHUNKS, LANES), lambda i: (i, 0, 0)), tok, tok, tok,
                   pl.BlockSpec((N_EXPERTS, LANES), lambda i: (0, 0))],
        out_shape=[jax.ShapeDtypeStruct((m, D_CHUNKS, LANES), F32),
                   jax.ShapeDtypeStruct((TOP_K, m), jnp.int32),
                   jax.ShapeDtypeStruct((TOP_K, m), F32),
                   jax.ShapeDtypeStruct((TOP_K, m), jnp.int32),
                   jax.ShapeDtypeStruct((N_EXPERTS, LANES), F32)],
        scratch_shapes=[pltpu.VMEM((N_EXPERTS, 1), F32)],
        compiler_params=_cparams(("arbitrary",)),
        name="outproj_router",
    )(ya, yb, x2, wo_bf16, ga, gb, eg, eb, g1, b1, wrh, wrl, rb)


def _moe_kernel(be_ref, nu_ref, tok_ref, tok_next_ref, h_hbm, wg_ref, wu_ref, wd_ref, y_ref,
                xbuf0, xbuf1, sem, wg_b, wu_b, wd_b):
    i = pl.program_id(0)
    n_used = nu_ref[0]
    even = i % 2 == 0
    e = be_ref[i]
    prev = be_ref[jnp.maximum(i - 1, 0)]
    bufs = (xbuf0, xbuf1)

    def start_gather(idx_ref, b):
        for r in range(MOE_BLOCK):
            pltpu.make_async_copy(h_hbm.at[idx_ref[0, 0, r]], bufs[b].at[r], sem.at[b]).start()

    def wait_gather(b):
        pltpu.make_async_copy(h_hbm.at[pl.ds(0, MOE_BLOCK)], bufs[b], sem.at[b]).wait()

    @pl.when(i == 0)
    def _():
        start_gather(tok_ref, 0)

    @pl.when(jnp.logical_or(i == 0, e != prev))
    def _():
        wg_b[...] = wg_ref[0].astype(BF16)
        wu_b[...] = wu_ref[0].astype(BF16)
        wd_b[...] = wd_ref[0].astype(BF16)

    def step(b):
        wait_gather(b)
        start_gather(tok_next_ref, 1 - b)
        x = _from_chunks(bufs[b]).astype(BF16)
        g = _dot(x, wg_b[...])
        u = _dot(x, wu_b[...])
        y_ref[...] = _dot((_silu(g) * u).astype(BF16), wd_b[...]).astype(y_ref.dtype)

    for b in range(2):
        parity = even if b == 0 else jnp.logical_not(even)

        @pl.when(jnp.logical_and(i < n_used, parity))
        def _():
            step(b)

        @pl.when(jnp.logical_and(i == n_used, parity))
        def _():
            wait_gather(b)

    @pl.when(i >= n_used)
    def _():
        y_ref[...] = jnp.zeros_like(y_ref)


def _moe_experts(block_e, n_used, slot_tok, h1, w_gate, w_up, w_down):
    n_blocks = slot_tok.shape[0]
    p = n_blocks * MOE_BLOCK
    assert (h1.shape[0] * TOP_K + N_EXPERTS * (MOE_BLOCK - 1)) // MOE_BLOCK < n_blocks
    idx_spec = lambda shift: pl.BlockSpec(
        (1, 1, MOE_BLOCK), lambda i, be, nu: (jnp.minimum(i + shift, n_blocks - 1), 0, 0),
        memory_space=pltpu.SMEM)
    return pl.pallas_call(
        _moe_kernel,
        grid_spec=pltpu.PrefetchScalarGridSpec(
            num_scalar_prefetch=2,
            grid=(n_blocks,),
            in_specs=[
                idx_spec(0), idx_spec(1),
                pl.BlockSpec(memory_space=pl.ANY),
                pl.BlockSpec((1, D_MODEL, D_EXPERT), lambda i, be, nu: (be[i], 0, 0)),
                pl.BlockSpec((1, D_MODEL, D_EXPERT), lambda i, be, nu: (be[i], 0, 0)),
                pl.BlockSpec((1, D_EXPERT, D_MODEL), lambda i, be, nu: (be[i], 0, 0)),
            ],
            out_specs=pl.BlockSpec((MOE_BLOCK, D_MODEL), lambda i, be, nu: (i, 0)),
            scratch_shapes=[pltpu.VMEM((MOE_BLOCK, D_CHUNKS, LANES), F32),
                            pltpu.VMEM((MOE_BLOCK, D_CHUNKS, LANES), F32),
                            pltpu.SemaphoreType.DMA((2,)),
                            pltpu.VMEM((D_MODEL, D_EXPERT), BF16),
                            pltpu.VMEM((D_MODEL, D_EXPERT), BF16),
                            pltpu.VMEM((D_EXPERT, D_MODEL), BF16)],
        ),
        out_shape=jax.ShapeDtypeStruct((p, D_MODEL), BF16),
        compiler_params=pltpu.CompilerParams(
            dimension_semantics=("arbitrary",), vmem_limit_bytes=VMEM_LIMIT,
            disable_bounds_checks=True),
        name="moe_experts",
    )(block_e, n_used, slot_tok, slot_tok, h1, w_gate, w_up, w_down)


def _final_kernel(h1_ref, ys_ref, tw_ref, wsg_ref, wsu_ref, wsd_ref, g2_ref, b2_ref, o_ref):
    h1 = _from_chunks(h1_ref)
    hb = h1.astype(BF16)
    g = _dot(hb, wsg_ref[...])
    u = _dot(hb, wsu_ref[...])
    ffn = _dot((_silu(g) * u).astype(BF16), wsd_ref[...])
    for k in range(TOP_K):
        ffn = ffn + ys_ref[k].astype(F32) * tw_ref[:, k:k + 1]
    o_ref[...] = _layer_norm(DEEPNORM_ALPHA * h1 + ffn, g2_ref[...], b2_ref[...])


def _final(h1, ysel, tw_t, wsg, wsu, wsd, g2, b2, tm):
    m = h1.shape[0]
    row = lambda w: pl.BlockSpec((tm, w), lambda i: (i, 0))
    full = lambda a: pl.BlockSpec(a.shape, lambda i: (0,) * a.ndim)
    return pl.pallas_call(
        _final_kernel,
        grid=(m // tm,),
        in_specs=[pl.BlockSpec((tm, D_CHUNKS, LANES), lambda i: (i, 0, 0)),
                  pl.BlockSpec((TOP_K, tm, D_MODEL), lambda i: (0, i, 0)),
                  row(TOP_K), full(wsg), full(wsu), full(wsd), full(g2), full(b2)],
        out_specs=row(D_MODEL),
        out_shape=jax.ShapeDtypeStruct((m, D_MODEL), F32),
        compiler_params=_cparams(("parallel",)),
        name="shared_combine_ln2",
    )(h1, ysel, tw_t, wsg, wsu, wsd, g2, b2)


def _pad_rows(a, rows):
    return jnp.concatenate([a, jnp.zeros((rows - a.shape[0],) + a.shape[1:], a.dtype)], axis=0)


def kernel(x, meta_tokens, ln_emb_g, ln_emb_b, t5_table, w_in, a_sink, na_rpb, g_norm_a, g_norm_b, w_out, ln1_g, ln1_b, w_router, router_bias, w_gate, w_up, w_down, ws_gate, ws_up, ws_down, ln2_g, ln2_b):
    bsz, seq, _ = x.shape
    m = bsz * seq
    r2 = lambda a: a.reshape(1, -1).astype(F32)
    x2 = x.reshape(m, D_MODEL)
    eg, eb = r2(ln_emb_g), r2(ln_emb_b)

    w_in_b = w_in[0].astype(BF16)
    tm = 512 if m % 512 == 0 else 128
    u = _ln_inproj(x2, eg, eb, w_in_b, tm, 1152).reshape(bsz, seq, IN_WIDTH)
    um = _ln_inproj(meta_tokens.astype(F32), eg, eb, w_in_b, N_META, 1152)
    cut = lambda c0, width: _pad_rows(um[:, c0 * LANES:c0 * LANES + width], LANES)

    ya = _window_attention(u, cut(COL_KA, KV_WIDTH), cut(COL_VA, KV_WIDTH),
                           _window_bias(t5_table, a_sink[0], seq), bsz, seq)
    yb = _neighbourhood_attention(u, cut(COL_KB, B_WIDTH), cut(COL_VB, B_WIDTH),
                                  _na_bias(na_rpb[0]), bsz, seq)

    wr_t = w_router[0].astype(F32).T
    wr_hi = wr_t.astype(BF16)
    wr_lo = (wr_t - wr_hi.astype(F32)).astype(BF16)
    tm4 = 256 if m % 256 == 0 else 128
    h1, top_e, top_w, top_r, cnt = _outproj_router(
        ya.reshape(m, A_WIDTH), yb.reshape(m, B_WIDTH), x2, w_out[0].astype(BF16),
        r2(g_norm_a), r2(g_norm_b), eg, eb, r2(ln1_g), r2(ln1_b), wr_hi, wr_lo,
        router_bias[0].astype(F32).reshape(N_EXPERTS, 1), tm4)

    counts = cnt[:, 0].astype(jnp.int32)
    nb_e = (counts + MOE_BLOCK - 1) // MOE_BLOCK
    bend = jnp.cumsum(nb_e)
    pstart = (bend - nb_e) * MOE_BLOCK
    n_blocks = (m * TOP_K) // MOE_BLOCK + N_EXPERTS
    p_rows = n_blocks * MOE_BLOCK
    blk = jnp.arange(n_blocks, dtype=jnp.int32)
    block_e = jnp.minimum(jnp.sum((bend[None, :] <= blk[:, None]).astype(jnp.int32), axis=1),
                          N_EXPERTS - 1)
    n_used = bend[-1:].astype(jnp.int32)
    expert_ids = jnp.arange(N_EXPERTS, dtype=jnp.int32)[:, None, None]
    dest = top_r + jnp.sum(jnp.where(top_e[None] == expert_ids, pstart[:, None, None], 0),
                           axis=0)
    tok = jnp.broadcast_to(jnp.arange(m, dtype=jnp.int32)[None, :], (TOP_K, m))
    slot_tok = jnp.zeros((p_rows,), jnp.int32).at[dest.reshape(-1)].set(
        tok.reshape(-1), unique_indices=True, mode='promise_in_bounds')

    y_sorted = _moe_experts(block_e, n_used, slot_tok.reshape(n_blocks, 1, MOE_BLOCK), h1,
                            w_gate[0], w_up[0], w_down[0])
    ysel = jnp.take(y_sorted, dest.reshape(-1), axis=0).reshape(TOP_K, m, D_MODEL)

    out = _final(h1, ysel, top_w.T, ws_gate[0].astype(BF16), ws_up[0].astype(BF16),
                 ws_down[0].astype(BF16), r2(ln2_g), r2(ln2_b), tm4)
    return out.reshape(bsz, seq, D_MODEL)
```

```python
import functools
import math

import jax
import jax.numpy as jnp
from jax import lax
from jax.experimental import pallas as pl
from jax.experimental.pallas import tpu as pltpu

F32 = jnp.float32
BF16 = jnp.bfloat16

D_MODEL = 2048
HEAD_DIM = 64
N_META = 16
GRID_W = 64
A_HEADS = 16
A_KV_HEADS = 4
A_WINDOW = 128
A_BLOCK = 128
N_BUCKETS = 32
MAX_DISTANCE = 128
B_HEADS = 16
NA_ROWS = 8
NA_COLS = 16
A_WIDTH = A_HEADS * HEAD_DIM
KV_WIDTH = A_KV_HEADS * HEAD_DIM
B_WIDTH = B_HEADS * HEAD_DIM
MIX_WIDTH = A_WIDTH + B_WIDTH
IN_WIDTH = A_WIDTH + 2 * KV_WIDTH + 3 * B_WIDTH
N_EXPERTS = 64
TOP_K = 8
N_GROUPS = 8
GROUP_SIZE = N_EXPERTS // N_GROUPS
TOPK_GROUPS = 4
D_EXPERT = 512
D_SHARED = 512
ROUTED_SCALE = 2.5
DEPTH = 1
DEEPNORM_ALPHA = (2 * DEPTH) ** 0.25
LN_EPS = 1e-5
RMS_EPS = 1e-6
NEG = -1e30
SCALE = HEAD_DIM ** -0.5

LANES = 128
SUBLANES = 8
VMEM_LIMIT = 56 * 1024 * 1024

COL_QA = 0
COL_KA = A_WIDTH // LANES
COL_VA = (A_WIDTH + KV_WIDTH) // LANES
COL_QB = (A_WIDTH + 2 * KV_WIDTH) // LANES
COL_KB = COL_QB + B_WIDTH // LANES
COL_VB = COL_KB + B_WIDTH // LANES

WIN_BATCH = 4
NA_UNROLL = 4
MOE_BLOCK = 256
DISPATCH_TILE = 128
D_HALF = D_MODEL // 2
U32 = jnp.uint32


def _pack_rows(x):
    hi = lax.bitcast_convert_type(x[:, :D_HALF].astype(jnp.bfloat16).astype(F32), U32)
    lo = lax.bitcast_convert_type(x[:, D_HALF:].astype(jnp.bfloat16).astype(F32), U32)
    return hi | (lo >> 16)


def _unpack_rows(p):
    hi = lax.bitcast_convert_type(p & jnp.uint32(0xFFFF0000), F32).astype(BF16)
    lo = lax.bitcast_convert_type(p << 16, F32).astype(BF16)
    return hi, lo


def _cparams(sem):
    return pltpu.CompilerParams(dimension_semantics=sem, vmem_limit_bytes=VMEM_LIMIT)


def _layer_norm(x, g, b):
    mu = jnp.mean(x, axis=-1, keepdims=True)
    xc = x - mu
    var = jnp.mean(xc * xc, axis=-1, keepdims=True)
    return xc * lax.rsqrt(var + LN_EPS) * g + b


def _dot(a, b):
    return jnp.dot(a, b, preferred_element_type=F32)


def _dot_nt(a, b):
    return lax.dot_general(a, b, (((1,), (1,)), ((), ())), preferred_element_type=F32)


def _silu(g):
    return g / (1.0 + jnp.exp(-g))


def _ln_inproj_kernel(x_ref, g_ref, b_ref, w_ref, o_ref, h_scr):
    @pl.when(pl.program_id(1) == 0)
    def _():
        h_scr[...] = _layer_norm(x_ref[...], g_ref[...], b_ref[...]).astype(BF16)

    o_ref[...] = _dot(h_scr[...], w_ref[...]).astype(o_ref.dtype)


def _ln_inproj(x2, g, b, w_bf16, tm, tn):
    m = x2.shape[0]
    n = w_bf16.shape[1]
    return pl.pallas_call(
        _ln_inproj_kernel,
        grid=(m // tm, n // tn),
        in_specs=[
            pl.BlockSpec((tm, D_MODEL), lambda i, j: (i, 0)),
            pl.BlockSpec((1, D_MODEL), lambda i, j: (0, 0)),
            pl.BlockSpec((1, D_MODEL), lambda i, j: (0, 0)),
            pl.BlockSpec((D_MODEL, tn), lambda i, j: (0, j)),
        ],
        out_specs=pl.BlockSpec((tm, tn), lambda i, j: (i, j)),
        out_shape=jax.ShapeDtypeStruct((m, n), BF16),
        scratch_shapes=[pltpu.VMEM((tm, D_MODEL), BF16)],
        compiler_params=_cparams(("parallel", "arbitrary")),
        name="ln_inproj",
    )(x2, g, b, w_bf16)


def _t5_bucket(rel):
    nb = N_BUCKETS // 2
    max_exact = nb // 2
    ret = jnp.where(rel > 0, nb, 0)
    n = jnp.abs(rel)
    nf = jnp.maximum(n, 1).astype(F32)
    large = max_exact + (jnp.log(nf / max_exact) / math.log(MAX_DISTANCE / max_exact)
                         * (nb - max_exact)).astype(jnp.int32)
    large = jnp.minimum(large, nb - 1)
    return ret + jnp.where(n < max_exact, n, large)


def _lookup(table_t, idx, n):
    onehot = (idx[None] == jnp.arange(n, dtype=jnp.int32).reshape((n,) + (1,) * idx.ndim))
    return jnp.einsum('hb,b...->h...', table_t, onehot.astype(F32),
                      precision=lax.Precision.HIGHEST)


def _window_bias(t5_table, a_sink, seq):
    assert N_META + A_BLOCK - (N_META - 1) > MAX_DISTANCE
    nblk = seq // A_BLOCK
    t5_t = t5_table.astype(F32).T
    q_loc = jnp.arange(A_BLOCK, dtype=jnp.int32)
    k_loc = jnp.arange(3 * A_BLOCK, dtype=jnp.int32) - A_BLOCK
    rel = k_loc[None, :] - q_loc[:, None]
    band = _lookup(t5_t, _t5_bucket(rel), N_BUCKETS)
    win = jnp.abs(rel) <= A_WINDOW
    sink = jnp.broadcast_to(a_sink.astype(F32)[:, None, None], (A_HEADS, A_BLOCK, 1))
    pad = jnp.full((A_HEADS, A_BLOCK, LANES - N_META - 1), NEG, F32)
    out = []
    for blk in (0, min(1, nblk - 1), nblk - 1):
        gk = blk * A_BLOCK + k_loc
        valid = win & (gk >= 0)[None, :] & (gk < seq)[None, :]
        band_v = jnp.where(valid[None], band, NEG)
        q_pos = N_META + blk * A_BLOCK + q_loc
        rel_m = jnp.arange(N_META, dtype=jnp.int32)[None, :] - q_pos[:, None]
        bias_m = _lookup(t5_t, _t5_bucket(rel_m), N_BUCKETS)
        out.append(jnp.concatenate([bias_m, sink, pad, band_v], axis=-1))
    return jnp.stack(out, axis=0)


def _window_kernel(q_ref, kp_ref, kc_ref, kn_ref, vp_ref, vc_ref, vn_ref, km_ref, vm_ref,
                   bias_ref, o_ref):
    kall = jnp.concatenate([km_ref[...], kp_ref[0], kc_ref[0], kn_ref[0]], axis=0)
    vall = jnp.concatenate([vm_ref[...], vp_ref[0], vc_ref[0], vn_ref[0]], axis=0)
    lane = lax.broadcasted_iota(jnp.int32, (A_BLOCK, LANES), 1)
    in_lo = lane < HEAD_DIM
    group = A_HEADS // A_KV_HEADS
    n_heads = 2 * group
    pair_out = []
    for h0 in range(0, n_heads, WIN_BATCH):
        scores = []
        for hl in range(h0, h0 + WIN_BATCH):
            p, half = hl // 2, hl % 2
            kv_half = hl // group
            qp = q_ref[0, :, p * LANES:(p + 1) * LANES].astype(F32) * SCALE
            src = qp if half == kv_half else pltpu.roll(qp, HEAD_DIM, axis=1)
            keep = in_lo if kv_half == 0 else jnp.logical_not(in_lo)
            qm = jnp.where(keep, src, 0.0).astype(BF16)
            scores.append(_dot_nt(qm, kall))
        probs = []
        for j, s in enumerate(scores):
            s = s + bias_ref[0, h0 + j]
            m = jnp.max(s, axis=-1, keepdims=True)
            e = jnp.exp(s - m)
            probs.append((e.astype(BF16), jnp.sum(e, axis=-1, keepdims=True)))
        res = []
        for j, (e, l) in enumerate(probs):
            hl = h0 + j
            o = _dot(e, vall) / l
            if hl % 2 != hl // group:
                o = pltpu.roll(o, HEAD_DIM, axis=1)
            res.append(o)
        for j in range(0, WIN_BATCH, 2):
            pair_out.append(jnp.where(in_lo, res[j], res[j + 1]))
    o_ref[0] = jnp.concatenate(pair_out, axis=1).astype(o_ref.dtype)


def _window_attention(u, kmeta, vmeta, bias, bsz, seq):
    nblk = seq // A_BLOCK
    n_pairs = A_KV_HEADS // 2
    qw = A_WIDTH // n_pairs

    def variant(n):
        return jnp.where(n == 0, 0, jnp.where(n == nblk - 1, 2, 1))

    def kv_spec(col0, shift):
        return pl.BlockSpec(
            (1, A_BLOCK, LANES),
            lambda b, j, n: (b, jnp.clip(n + shift, 0, nblk - 1), col0 + j))

    return pl.pallas_call(
        _window_kernel,
        grid=(bsz, n_pairs, nblk),
        in_specs=[
            pl.BlockSpec((1, A_BLOCK, qw), lambda b, j, n: (b, n, j)),
            kv_spec(COL_KA, -1), kv_spec(COL_KA, 0), kv_spec(COL_KA, 1),
            kv_spec(COL_VA, -1), kv_spec(COL_VA, 0), kv_spec(COL_VA, 1),
            pl.BlockSpec((LANES, LANES), lambda b, j, n: (0, j)),
            pl.BlockSpec((LANES, LANES), lambda b, j, n: (0, j)),
            pl.BlockSpec((1, A_HEADS // n_pairs, A_BLOCK, 4 * LANES),
                         lambda b, j, n: (variant(n), j, 0, 0)),
        ],
        out_specs=pl.BlockSpec((1, A_BLOCK, qw), lambda b, j, n: (b, n, j)),
        out_shape=jax.ShapeDtypeStruct((bsz, seq, A_WIDTH), BF16),
        compiler_params=_cparams(("parallel", "parallel", "arbitrary")),
        name="window_attention",
    )(u, u, u, u, u, u, u, kmeta, vmeta, bias)


def _na_bias(rpb):
    qc = jnp.arange(GRID_W, dtype=jnp.int32)
    kc = jnp.arange(GRID_W, dtype=jnp.int32)
    cs = jnp.clip(qc - NA_COLS // 2, 0, GRID_W - NA_COLS)
    cmask = (kc[None, :] >= cs[:, None]) & (kc[None, :] < cs[:, None] + NA_COLS)
    dc = jnp.clip(kc[None, :] - qc[:, None] + NA_COLS - 1, 0, 2 * NA_COLS - 2)
    n_dc = 2 * NA_COLS - 1
    onehot = (dc[None] == jnp.arange(n_dc, dtype=jnp.int32)[:, None, None]).astype(F32)
    t = jnp.einsum('hrd,dqk->hrqk', rpb.astype(F32), onehot, precision=lax.Precision.HIGHEST)
    t = jnp.where(cmask[None, None], t, NEG)
    variants = []
    for oi in range(NA_ROWS):
        variants.append(jnp.concatenate(
            [t[:, i - oi + NA_ROWS - 1] for i in range(NA_ROWS)], axis=-1))
    return jnp.stack(variants, axis=1)


def _na_kernel(q_ref, k_ref, v_ref, km_ref, vm_ref, bias_ref, o_ref, *, rows):
    span = NA_ROWS * GRID_W
    lane = lax.broadcasted_iota(jnp.int32, (GRID_W, LANES), 1)
    in_lo = lane < HEAD_DIM
    meta_bias = jnp.where(lane < N_META, 0.0, NEG).astype(F32)
    km = km_ref[...]
    vm = vm_ref[...]

    def body(it, carry):
        work = []
        for j in range(NA_UNROLL):
            r = it * NA_UNROLL + j
            rs = jnp.clip(r - NA_ROWS // 2, 0, rows - NA_ROWS)
            oi = r - rs
            q0 = pl.multiple_of(r * GRID_W, GRID_W)
            k0 = pl.multiple_of(rs * GRID_W, GRID_W)
            q = q_ref[0, pl.ds(q0, GRID_W), :]
            ks = k_ref[0, pl.ds(k0, span), :]
            for hl in range(2):
                keep = in_lo if hl == 0 else jnp.logical_not(in_lo)
                qm = jnp.where(keep, q, jnp.zeros_like(q))
                work.append((_dot_nt(qm, ks), _dot_nt(qm, km), hl, oi, k0, q0))
        probs = []
        for sw, sm, hl, oi, k0, q0 in work:
            sw = sw * SCALE + bias_ref[hl, oi]
            sm = sm * SCALE + meta_bias
            m = jnp.maximum(jnp.max(sw, axis=-1, keepdims=True),
                            jnp.max(sm, axis=-1, keepdims=True))
            ew = jnp.exp(sw - m)
            em = jnp.exp(sm - m)
            l = jnp.sum(ew, axis=-1, keepdims=True) + jnp.sum(em, axis=-1, keepdims=True)
            probs.append((ew.astype(BF16), em.astype(BF16), l, k0, q0))
        res = []
        for ew, em, l, k0, q0 in probs:
            vs = v_ref[0, pl.ds(k0, span), :]
            res.append((_dot(ew, vs) + _dot(em, vm)) / l)
        for j in range(NA_UNROLL):
            q0 = probs[2 * j][4]
            out = jnp.where(in_lo, res[2 * j], res[2 * j + 1])
            o_ref[0, pl.ds(q0, GRID_W), :] = out.astype(o_ref.dtype)
        return carry

    lax.fori_loop(0, rows // NA_UNROLL, body, 0)


def _neighbourhood_attention(u, kmeta, vmeta, bias, bsz, seq):
    rows = seq // GRID_W
    assert rows >= NA_ROWS and rows % NA_UNROLL == 0
    n_pairs = B_HEADS // 2

    def col_spec(col0):
        return pl.BlockSpec((1, seq, LANES), lambda b, j: (b, 0, col0 + j))

    return pl.pallas_call(
        functools.partial(_na_kernel, rows=rows),
        grid=(bsz, n_pairs),
        in_specs=[
            col_spec(COL_QB), col_spec(COL_KB), col_spec(COL_VB),
            pl.BlockSpec((LANES, LANES), lambda b, j: (0, j)),
            pl.BlockSpec((LANES, LANES), lambda b, j: (0, j)),
            pl.BlockSpec((2, NA_ROWS, GRID_W, NA_ROWS * GRID_W), lambda b, j: (j, 0, 0, 0)),
        ],
        out_specs=pl.BlockSpec((1, seq, LANES), lambda b, j: (b, 0, j)),
        out_shape=jax.ShapeDtypeStruct((bsz, seq, B_WIDTH), BF16),
        compiler_params=_cparams(("parallel", "parallel")),
        name="neighbourhood_attention",
    )(u, u, u, kmeta, vmeta, bias)


def _route(scores, rbias):
    t = scores.shape[-1]
    ninf = -jnp.inf
    biased = scores + rbias
    b3 = biased.reshape(N_GROUPS, GROUP_SIZE, t)
    s3 = scores.reshape(N_GROUPS, GROUP_SIZE, t)
    io_in = lax.broadcasted_iota(jnp.int32, b3.shape, 1)
    io_g3 = lax.broadcasted_iota(jnp.int32, b3.shape, 0)
    io_e = io_g3 * GROUP_SIZE + io_in
    m1 = jnp.max(b3, axis=1, keepdims=True)
    i1 = jnp.min(jnp.where(b3 == m1, io_in, GROUP_SIZE), axis=1, keepdims=True)
    m2 = jnp.max(jnp.where(io_in == i1, ninf, b3), axis=1, keepdims=True)
    gs = m1 + m2
    io_g = lax.broadcasted_iota(jnp.int32, gs.shape, 0)
    gmask = jnp.zeros(gs.shape, jnp.bool_)
    cur = gs
    for _ in range(TOPK_GROUPS):
        m = jnp.max(cur, axis=0, keepdims=True)
        i = jnp.min(jnp.where(cur == m, io_g, N_GROUPS), axis=0, keepdims=True)
        pick = io_g == i
        gmask = jnp.logical_or(gmask, pick)
        cur = jnp.where(pick, ninf, cur)
    cur = jnp.where(gmask, b3, ninf)
    picks, top_e, top_s = [], [], []
    for _ in range(TOP_K):
        m = jnp.max(jnp.max(cur, axis=1, keepdims=True), axis=0, keepdims=True)
        i = jnp.min(jnp.min(jnp.where(cur == m, io_e, N_EXPERTS), axis=1, keepdims=True),
                    axis=0, keepdims=True)
        pick = io_e == i
        picks.append(pick)
        top_e.append(i.reshape(1, t))
        w = jnp.sum(jnp.sum(jnp.where(pick, s3, 0.0), axis=1, keepdims=True), axis=0, keepdims=True)
        top_s.append(w.reshape(1, t))
        cur = jnp.where(pick, ninf, cur)
    top_e = jnp.concatenate(top_e, axis=0)
    top_s = jnp.concatenate(top_s, axis=0)
    denom = top_s[0:1]
    for k in range(1, TOP_K):
        denom = denom + top_s[k:k + 1]
    top_w = top_s / (denom + 1e-20) * ROUTED_SCALE
    sel = picks[0]
    for k in range(1, TOP_K):
        sel = jnp.logical_or(sel, picks[k])
    sel = jnp.where(sel, 1.0, 0.0).astype(F32).reshape(N_EXPERTS, t)
    return top_e, top_w, sel, picks


def _outproj_router_kernel(ya_ref, yb_ref, x_ref, wo_ref, ga_ref, gb_ref, eg_ref, eb_ref,
                           g1_ref, b1_ref, wrh_ref, wrl_ref, rb_ref,
                           h1_ref, te_ref, tw_ref, tr_ref, cnt_ref, carry):
    tm = x_ref.shape[0]

    @pl.when(pl.program_id(0) == 0)
    def _():
        carry[...] = jnp.zeros_like(carry)

    def rms(y_ref, g_ref):
        y = y_ref[...].astype(F32)
        inv = lax.rsqrt(jnp.mean(y * y, axis=-1, keepdims=True) + RMS_EPS)
        return (y * inv * g_ref[...]).astype(BF16)

    mix = (_dot(rms(ya_ref, ga_ref), wo_ref[0:A_WIDTH, :])
           + _dot(rms(yb_ref, gb_ref), wo_ref[A_WIDTH:MIX_WIDTH, :]))
    h = _layer_norm(x_ref[...], eg_ref[...], eb_ref[...])
    h1 = _layer_norm(DEEPNORM_ALPHA * h + mix, g1_ref[...], b1_ref[...])
    h1_ref[...] = h1
    hb = h1.astype(BF16)

    hlo = (h1 - hb.astype(F32)).astype(BF16)
    logits = (_dot_nt(wrh_ref[...], hb) + _dot_nt(wrh_ref[...], hlo)
              + _dot_nt(wrl_ref[...], hb))
    scores = 1.0 / (1.0 + jnp.exp(-logits))
    top_e, top_w, sel, picks = _route(scores, rb_ref[...])

    row = lax.broadcasted_iota(jnp.int32, (tm, tm), 0)
    col = lax.broadcasted_iota(jnp.int32, (tm, tm), 1)
    before = jnp.where(row < col, 1.0, 0.0).astype(BF16)
    rank = _dot(sel.astype(BF16), before) + carry[...]
    rank3 = rank.reshape(N_GROUPS, GROUP_SIZE, tm)
    ranks = []
    for k in range(TOP_K):
        rk = jnp.sum(jnp.sum(jnp.where(picks[k], rank3, 0.0), axis=1, keepdims=True),
                     axis=0, keepdims=True)
        ranks.append(rk.reshape(1, tm))
    te_ref[...] = top_e
    tw_ref[...] = top_w
    tr_ref[...] = jnp.concatenate(ranks, axis=0).astype(jnp.int32)
    carry[...] = carry[...] + jnp.sum(sel, axis=-1, keepdims=True)
    cnt_ref[...] = jnp.broadcast_to(carry[...], cnt_ref.shape)


def _outproj_router(ya, yb, x2, wo_bf16, ga, gb, eg, eb, g1, b1, wrh, wrl, rb, tm):
    m = x2.shape[0]
    row = lambda w: pl.BlockSpec((tm, w), lambda i: (i, 0))
    full = lambda a: pl.BlockSpec(a.shape, lambda i: (0,) * a.ndim)
    tok = pl.BlockSpec((TOP_K, tm), lambda i: (0, i))
    return pl.pallas_call(
        _outproj_router_kernel,
        grid=(m // tm,),
        in_specs=[row(A_WIDTH), row(B_WIDTH), row(D_MODEL), full(wo_bf16), full(ga), full(gb),
                  full(eg), full(eb), full(g1), full(b1), full(wrh), full(wrl), full(rb)],
        out_specs=[row(D_MODEL), tok, tok, tok,
                   pl.BlockSpec((N_EXPERTS, LANES), lambda i: (0, 0))],
        out_shape=[jax.ShapeDtypeStruct((m, D_MODEL), F32),
                   jax.ShapeDtypeStruct((TOP_K, m), jnp.int32),
                   jax.ShapeDtypeStruct((TOP_K, m), F32),
                   jax.ShapeDtypeStruct((TOP_K, m), jnp.int32),
                   jax.ShapeDtypeStruct((N_EXPERTS, LANES), F32)],
        scratch_shapes=[pltpu.VMEM((N_EXPERTS, 1), F32)],
        compiler_params=_cparams(("arbitrary",)),
        name="outproj_router",
    )(ya, yb, x2, wo_bf16, ga, gb, eg, eb, g1, b1, wrh, wrl, rb)


def _zero_fill_padding(pad_start_ref, pad_len_ref, n_used, xs_hbm, zeros, sem):
    n_blocks = xs_hbm.shape[0] // MOE_BLOCK
    zeros[...] = jnp.zeros_like(zeros)

    def copies(act):
        def per_expert(e, carry):
            start = pad_start_ref[e]
            n = pad_len_ref[e]
            head = jnp.minimum((-start) & (SUBLANES - 1), n)
            for j in range(SUBLANES - 1):
                @pl.when(j < head)
                def _(j=j):
                    act(pltpu.make_async_copy(zeros.at[pl.ds(0, 1), :],
                                              xs_hbm.at[pl.ds(start + j, 1), :], sem.at[1]))
            aligned = start + head
            rest = n - head
            bit = MOE_BLOCK // 2
            while bit >= SUBLANES:
                @pl.when((rest & bit) != 0)
                def _(bit=bit):
                    off = pl.multiple_of(aligned + (rest & (-2 * bit)), SUBLANES)
                    act(pltpu.make_async_copy(zeros.at[pl.ds(0, bit), :],
                                              xs_hbm.at[pl.ds(off, bit), :], sem.at[1]))
                bit //= 2
            return carry

        def per_block(j, carry):
            off = pl.multiple_of(j * MOE_BLOCK, MOE_BLOCK)
            act(pltpu.make_async_copy(zeros, xs_hbm.at[pl.ds(off, MOE_BLOCK), :], sem.at[1]))
            return carry

        lax.fori_loop(0, N_EXPERTS, per_expert, 0)
        lax.fori_loop(n_used, n_blocks, per_block, 0)

    copies(lambda cp: cp.start())
    copies(lambda cp: cp.wait())


def _dispatch_kernel(pad_start_ref, pad_len_ref, nu_ref, dest_ref, h1_ref, wsg_ref, wsu_ref,
                     wsd_ref, xs_hbm, sh_ref, packed, zeros, sem):
    @pl.when(pl.program_id(0) == 0)
    def _():
        _zero_fill_padding(pad_start_ref, pad_len_ref, nu_ref[0], xs_hbm, zeros, sem)

    tm = h1_ref.shape[0]
    h1 = h1_ref[...]
    packed[...] = _pack_rows(h1)
    for t in range(tm):
        for k in range(TOP_K):
            pltpu.make_async_copy(packed.at[pl.ds(t, 1), :],
                                  xs_hbm.at[pl.ds(dest_ref[k, t], 1), :], sem.at[0]).start()
    hb = h1.astype(BF16)
    g = _dot(hb, wsg_ref[...])
    u = _dot(hb, wsu_ref[...])
    sh_ref[...] = _dot((_silu(g) * u).astype(BF16), wsd_ref[...]).astype(sh_ref.dtype)
    for k in range(TOP_K):
        pltpu.make_async_copy(packed, xs_hbm.at[pl.ds(0, tm), :], sem.at[0]).wait()


def _dispatch_shared(pad_start, pad_len, n_used, dest, h1, wsg, wsu, wsd, p_rows):
    m = h1.shape[0]
    tm = DISPATCH_TILE
    full = lambda a: pl.BlockSpec(a.shape, lambda i, *_: (0,) * a.ndim)
    return pl.pallas_call(
        _dispatch_kernel,
        grid_spec=pltpu.PrefetchScalarGridSpec(
            num_scalar_prefetch=3,
            grid=(m // tm,),
            in_specs=[pl.BlockSpec((TOP_K, tm), lambda i, *_: (0, i), memory_space=pltpu.SMEM),
                      pl.BlockSpec((tm, D_MODEL), lambda i, *_: (i, 0)),
                      full(wsg), full(wsu), full(wsd)],
            out_specs=[pl.BlockSpec(memory_space=pl.ANY),
                       pl.BlockSpec((tm, D_MODEL), lambda i, *_: (i, 0))],
            scratch_shapes=[pltpu.VMEM((tm, D_HALF), U32),
                            pltpu.VMEM((MOE_BLOCK, D_HALF), U32),
                            pltpu.SemaphoreType.DMA((2,))],
        ),
        out_shape=[jax.ShapeDtypeStruct((p_rows, D_HALF), U32),
                   jax.ShapeDtypeStruct((m, D_MODEL), BF16)],
        compiler_params=pltpu.CompilerParams(
            dimension_semantics=("arbitrary",), vmem_limit_bytes=VMEM_LIMIT,
            disable_bounds_checks=True),
        name="dispatch_shared",
    )(pad_start, pad_len, n_used, dest, h1, wsg, wsu, wsd)


def _moe_kernel(be_ref, nu_ref, x_ref, wg_ref, wu_ref, wd_ref, y_ref, wg_b, wu_b, wd_b):
    i = pl.program_id(0)
    e = be_ref[i]
    prev = be_ref[jnp.maximum(i - 1, 0)]

    @pl.when(jnp.logical_or(i == 0, e != prev))
    def _():
        wg_b[...] = wg_ref[0].astype(BF16)
        wu_b[...] = wu_ref[0].astype(BF16)
        wd_b[...] = wd_ref[0].astype(BF16)

    @pl.when(i < nu_ref[0])
    def _():
        hi, lo = _unpack_rows(x_ref[...])
        g = _dot(hi, wg_b[0:D_HALF, :]) + _dot(lo, wg_b[D_HALF:D_MODEL, :])
        u = _dot(hi, wu_b[0:D_HALF, :]) + _dot(lo, wu_b[D_HALF:D_MODEL, :])
        y_ref[...] = _dot((_silu(g) * u).astype(BF16), wd_b[...]).astype(y_ref.dtype)

    @pl.when(i >= nu_ref[0])
    def _():
        y_ref[...] = jnp.zeros_like(y_ref)


def _moe_experts(block_e, n_used, x_sorted, w_gate, w_up, w_down):
    p = x_sorted.shape[0]
    n_blocks = p // MOE_BLOCK
    last = lambda i, be, nu: (jnp.minimum(i, nu[0] - 1), 0)
    return pl.pallas_call(
        _moe_kernel,
        grid_spec=pltpu.PrefetchScalarGridSpec(
            num_scalar_prefetch=2,
            grid=(n_blocks,),
            in_specs=[
                pl.BlockSpec((MOE_BLOCK, D_HALF), last),
                pl.BlockSpec((1, D_MODEL, D_EXPERT), lambda i, be, nu: (be[i], 0, 0)),
                pl.BlockSpec((1, D_MODEL, D_EXPERT), lambda i, be, nu: (be[i], 0, 0)),
                pl.BlockSpec((1, D_EXPERT, D_MODEL), lambda i, be, nu: (be[i], 0, 0)),
            ],
            out_specs=pl.BlockSpec((MOE_BLOCK, D_MODEL), lambda i, be, nu: (i, 0)),
            scratch_shapes=[pltpu.VMEM((D_MODEL, D_EXPERT), BF16),
                            pltpu.VMEM((D_MODEL, D_EXPERT), BF16),
                            pltpu.VMEM((D_EXPERT, D_MODEL), BF16)],
        ),
        out_shape=jax.ShapeDtypeStruct((p, D_MODEL), BF16),
        compiler_params=_cparams(("arbitrary",)),
        name="moe_experts",
    )(block_e, n_used, x_sorted, w_gate, w_up, w_down)


def _final_kernel(h1_ref, sh_ref, ys_ref, tw_ref, g2_ref, b2_ref, o_ref):
    ffn = sh_ref[...].astype(F32)
    for k in range(TOP_K):
        ffn = ffn + ys_ref[k].astype(F32) * tw_ref[:, k:k + 1]
    o_ref[...] = _layer_norm(DEEPNORM_ALPHA * h1_ref[...] + ffn, g2_ref[...], b2_ref[...])


def _final(h1, sh, ysel, tw_t, g2, b2, tm):
    m = h1.shape[0]
    row = lambda w: pl.BlockSpec((tm, w), lambda i: (i, 0))
    full = lambda a: pl.BlockSpec(a.shape, lambda i: (0,) * a.ndim)
    return pl.pallas_call(
        _final_kernel,
        grid=(m // tm,),
        in_specs=[row(D_MODEL), row(D_MODEL),
                  pl.BlockSpec((TOP_K, tm, D_MODEL), lambda i: (0, i, 0)),
                  row(TOP_K), full(g2), full(b2)],
        out_specs=row(D_MODEL),
        out_shape=jax.ShapeDtypeStruct((m, D_MODEL), F32),
        compiler_params=_cparams(("parallel",)),
        name="combine_ln2",
    )(h1, sh, ysel, tw_t, g2, b2)


def _pad_rows(a, rows):
    return jnp.concatenate([a, jnp.zeros((rows - a.shape[0],) + a.shape[1:], a.dtype)], axis=0)


def kernel(x, meta_tokens, ln_emb_g, ln_emb_b, t5_table, w_in, a_sink, na_rpb, g_norm_a, g_norm_b, w_out, ln1_g, ln1_b, w_router, router_bias, w_gate, w_up, w_down, ws_gate, ws_up, ws_down, ln2_g, ln2_b):
    bsz, seq, _ = x.shape
    m = bsz * seq
    r2 = lambda a: a.reshape(1, -1).astype(F32)
    x2 = x.reshape(m, D_MODEL)
    eg, eb = r2(ln_emb_g), r2(ln_emb_b)

    w_in_b = w_in[0].astype(BF16)
    tm = 512 if m % 512 == 0 else 128
    u = _ln_inproj(x2, eg, eb, w_in_b, tm, 1152).reshape(bsz, seq, IN_WIDTH)
    um = _ln_inproj(meta_tokens.astype(F32), eg, eb, w_in_b, N_META, 1152)
    cut = lambda c0, width: _pad_rows(um[:, c0 * LANES:c0 * LANES + width], LANES)

    ya = _window_attention(u, cut(COL_KA, KV_WIDTH), cut(COL_VA, KV_WIDTH),
                           _window_bias(t5_table, a_sink[0], seq), bsz, seq)
    yb = _neighbourhood_attention(u, cut(COL_KB, B_WIDTH), cut(COL_VB, B_WIDTH),
                                  _na_bias(na_rpb[0]), bsz, seq)

    wr_t = w_router[0].astype(F32).T
    wr_hi = wr_t.astype(BF16)
    wr_lo = (wr_t - wr_hi.astype(F32)).astype(BF16)
    tm4 = 256 if m % 256 == 0 else 128
    h1, top_e, top_w, top_r, cnt = _outproj_router(
        ya.reshape(m, A_WIDTH), yb.reshape(m, B_WIDTH), x2, w_out[0].astype(BF16),
        r2(g_norm_a), r2(g_norm_b), eg, eb, r2(ln1_g), r2(ln1_b), wr_hi, wr_lo,
        router_bias[0].astype(F32).reshape(N_EXPERTS, 1), tm4)

    counts = cnt[:, 0].astype(jnp.int32)
    nb_e = (counts + MOE_BLOCK - 1) // MOE_BLOCK
    bend = jnp.cumsum(nb_e)
    pstart = (bend - nb_e) * MOE_BLOCK
    n_blocks = (m * TOP_K) // MOE_BLOCK + N_EXPERTS
    p_rows = n_blocks * MOE_BLOCK
    blk = jnp.arange(n_blocks, dtype=jnp.int32)
    block_e = jnp.minimum(jnp.sum((bend[None, :] <= blk[:, None]).astype(jnp.int32), axis=1),
                          N_EXPERTS - 1)
    n_used = bend[-1:].astype(jnp.int32)
    expert_ids = jnp.arange(N_EXPERTS, dtype=jnp.int32)[:, None, None]
    dest = top_r + jnp.sum(jnp.where(top_e[None] == expert_ids, pstart[:, None, None], 0),
                           axis=0)
    x_sorted, shared = _dispatch_shared(
        pstart + counts, nb_e * MOE_BLOCK - counts, n_used, dest, h1,
        ws_gate[0].astype(BF16), ws_up[0].astype(BF16), ws_down[0].astype(BF16), p_rows)
    y_sorted = _moe_experts(block_e, n_used, x_sorted, w_gate[0], w_up[0], w_down[0])
    ysel = y_sorted.at[dest.reshape(-1)].get(
        mode='promise_in_bounds', unique_indices=True).reshape(TOP_K, m, D_MODEL)

    out = _final(h1, shared, ysel, top_w.T, r2(ln2_g), r2(ln2_b), tm4)
    return out.reshape(bsz, seq, D_MODEL)
```

```python
import functools
import math

import jax
import jax.numpy as jnp
from jax import lax
from jax.experimental import pallas as pl
from jax.experimental.pallas import tpu as pltpu

F32 = jnp.float32
BF16 = jnp.bfloat16

D_MODEL = 2048
HEAD_DIM = 64
N_META = 16
GRID_W = 64
A_HEADS = 16
A_KV_HEADS = 4
A_WINDOW = 128
A_BLOCK = 128
N_BUCKETS = 32
MAX_DISTANCE = 128
B_HEADS = 16
NA_ROWS = 8
NA_COLS = 16
A_WIDTH = A_HEADS * HEAD_DIM
KV_WIDTH = A_KV_HEADS * HEAD_DIM
B_WIDTH = B_HEADS * HEAD_DIM
MIX_WIDTH = A_WIDTH + B_WIDTH
IN_WIDTH = A_WIDTH + 2 * KV_WIDTH + 3 * B_WIDTH
N_EXPERTS = 64
TOP_K = 8
N_GROUPS = 8
GROUP_SIZE = N_EXPERTS // N_GROUPS
TOPK_GROUPS = 4
D_EXPERT = 512
D_SHARED = 512
ROUTED_SCALE = 2.5
DEPTH = 1
DEEPNORM_ALPHA = (2 * DEPTH) ** 0.25
LN_EPS = 1e-5
RMS_EPS = 1e-6
NEG = -1e30
SCALE = HEAD_DIM ** -0.5

LANES = 128
SUBLANES = 8
VMEM_LIMIT = 56 * 1024 * 1024

COL_QA = 0
COL_KA = A_WIDTH // LANES
COL_VA = (A_WIDTH + KV_WIDTH) // LANES
COL_QB = (A_WIDTH + 2 * KV_WIDTH) // LANES
COL_KB = COL_QB + B_WIDTH // LANES
COL_VB = COL_KB + B_WIDTH // LANES

WIN_BATCH = 8
NA_UNROLL = 8
MOE_BLOCK = 512
DISPATCH_TILE = 128
D_HALF = D_MODEL // 2
U32 = jnp.uint32


def _pack_rows(x):
    hi = lax.bitcast_convert_type(x[:, :D_HALF].astype(jnp.bfloat16).astype(F32), U32)
    lo = lax.bitcast_convert_type(x[:, D_HALF:].astype(jnp.bfloat16).astype(F32), U32)
    return hi | (lo >> 16)


def _unpack_rows(p):
    hi = lax.bitcast_convert_type(p & jnp.uint32(0xFFFF0000), F32).astype(BF16)
    lo = lax.bitcast_convert_type(p << 16, F32).astype(BF16)
    return hi, lo


def _cparams(sem):
    return pltpu.CompilerParams(dimension_semantics=sem, vmem_limit_bytes=VMEM_LIMIT)


def _layer_norm(x, g, b):
    mu = jnp.mean(x, axis=-1, keepdims=True)
    xc = x - mu
    var = jnp.mean(xc * xc, axis=-1, keepdims=True)
    return xc * lax.rsqrt(var + LN_EPS) * g + b


def _dot(a, b):
    return jnp.dot(a, b, preferred_element_type=F32)


def _dot_nt(a, b):
    return lax.dot_general(a, b, (((1,), (1,)), ((), ())), preferred_element_type=F32)


def _silu(g):
    return g / (1.0 + jnp.exp(-g))


def _ln_inproj_kernel(x_ref, g_ref, b_ref, w_ref, o_ref, h_scr):
    @pl.when(pl.program_id(1) == 0)
    def _():
        h_scr[...] = _layer_norm(x_ref[...], g_ref[...], b_ref[...]).astype(BF16)

    o_ref[...] = _dot(h_scr[...], w_ref[...]).astype(o_ref.dtype)


def _ln_inproj(x2, g, b, w_bf16, tm, tn):
    m = x2.shape[0]
    n = w_bf16.shape[1]
    return pl.pallas_call(
        _ln_inproj_kernel,
        grid=(m // tm, n // tn),
        in_specs=[
            pl.BlockSpec((tm, D_MODEL), lambda i, j: (i, 0)),
            pl.BlockSpec((1, D_MODEL), lambda i, j: (0, 0)),
            pl.BlockSpec((1, D_MODEL), lambda i, j: (0, 0)),
            pl.BlockSpec((D_MODEL, tn), lambda i, j: (0, j)),
        ],
        out_specs=pl.BlockSpec((tm, tn), lambda i, j: (i, j)),
        out_shape=jax.ShapeDtypeStruct((m, n), BF16),
        scratch_shapes=[pltpu.VMEM((tm, D_MODEL), BF16)],
        compiler_params=_cparams(("parallel", "arbitrary")),
        name="ln_inproj",
    )(x2, g, b, w_bf16)


def _t5_bucket(rel):
    nb = N_BUCKETS // 2
    max_exact = nb // 2
    ret = jnp.where(rel > 0, nb, 0)
    n = jnp.abs(rel)
    nf = jnp.maximum(n, 1).astype(F32)
    large = max_exact + (jnp.log(nf / max_exact) / math.log(MAX_DISTANCE / max_exact)
                         * (nb - max_exact)).astype(jnp.int32)
    large = jnp.minimum(large, nb - 1)
    return ret + jnp.where(n < max_exact, n, large)


def _lookup(table_t, idx, n):
    onehot = (idx[None] == jnp.arange(n, dtype=jnp.int32).reshape((n,) + (1,) * idx.ndim))
    return jnp.einsum('hb,b...->h...', table_t, onehot.astype(F32),
                      precision=lax.Precision.HIGHEST)


def _window_bias(t5_table, a_sink, seq):
    assert N_META + A_BLOCK - (N_META - 1) > MAX_DISTANCE
    nblk = seq // A_BLOCK
    t5_t = t5_table.astype(F32).T
    q_loc = jnp.arange(A_BLOCK, dtype=jnp.int32)
    k_loc = jnp.arange(3 * A_BLOCK, dtype=jnp.int32) - A_BLOCK
    rel = k_loc[None, :] - q_loc[:, None]
    band = _lookup(t5_t, _t5_bucket(rel), N_BUCKETS)
    win = jnp.abs(rel) <= A_WINDOW
    sink = jnp.broadcast_to(a_sink.astype(F32)[:, None, None], (A_HEADS, A_BLOCK, 1))
    pad = jnp.full((A_HEADS, A_BLOCK, LANES - N_META - 1), NEG, F32)
    out = []
    for blk in (0, min(1, nblk - 1), nblk - 1):
        gk = blk * A_BLOCK + k_loc
        valid = win & (gk >= 0)[None, :] & (gk < seq)[None, :]
        band_v = jnp.where(valid[None], band, NEG)
        q_pos = N_META + blk * A_BLOCK + q_loc
        rel_m = jnp.arange(N_META, dtype=jnp.int32)[None, :] - q_pos[:, None]
        bias_m = _lookup(t5_t, _t5_bucket(rel_m), N_BUCKETS)
        out.append(jnp.concatenate([bias_m, sink, pad, band_v], axis=-1))
    return jnp.stack(out, axis=0)


def _window_kernel(q_ref, kp_ref, kc_ref, kn_ref, vp_ref, vc_ref, vn_ref, km_ref, vm_ref,
                   bias_ref, o_ref):
    kall = jnp.concatenate([km_ref[...], kp_ref[0], kc_ref[0], kn_ref[0]], axis=0)
    vall = jnp.concatenate([vm_ref[...], vp_ref[0], vc_ref[0], vn_ref[0]], axis=0)
    lane = lax.broadcasted_iota(jnp.int32, (A_BLOCK, LANES), 1)
    in_lo = lane < HEAD_DIM
    group = A_HEADS // A_KV_HEADS
    n_heads = 2 * group
    pair_out = []
    for h0 in range(0, n_heads, WIN_BATCH):
        scores = []
        for hl in range(h0, h0 + WIN_BATCH):
            p, half = hl // 2, hl % 2
            kv_half = hl // group
            qp = q_ref[0, :, p * LANES:(p + 1) * LANES].astype(F32) * SCALE
            src = qp if half == kv_half else pltpu.roll(qp, HEAD_DIM, axis=1)
            keep = in_lo if kv_half == 0 else jnp.logical_not(in_lo)
            qm = jnp.where(keep, src, 0.0).astype(BF16)
            scores.append(_dot_nt(qm, kall))
        probs = []
        for j, s in enumerate(scores):
            s = s + bias_ref[0, h0 + j]
            m = jnp.max(s, axis=-1, keepdims=True)
            e = jnp.exp(s - m)
            probs.append((e.astype(BF16), jnp.sum(e, axis=-1, keepdims=True)))
        res = []
        for j, (e, l) in enumerate(probs):
            hl = h0 + j
            o = _dot(e, vall) / l
            if hl % 2 != hl // group:
                o = pltpu.roll(o, HEAD_DIM, axis=1)
            res.append(o)
        for j in range(0, WIN_BATCH, 2):
            pair_out.append(jnp.where(in_lo, res[j], res[j + 1]))
    o_ref[0] = jnp.concatenate(pair_out, axis=1).astype(o_ref.dtype)


def _window_attention(u, kmeta, vmeta, bias, bsz, seq):
    nblk = seq // A_BLOCK
    n_pairs = A_KV_HEADS // 2
    qw = A_WIDTH // n_pairs

    def variant(n):
        return jnp.where(n == 0, 0, jnp.where(n == nblk - 1, 2, 1))

    def kv_spec(col0, shift):
        return pl.BlockSpec(
            (1, A_BLOCK, LANES),
            lambda b, j, n: (b, jnp.clip(n + shift, 0, nblk - 1), col0 + j))

    return pl.pallas_call(
        _window_kernel,
        grid=(bsz, n_pairs, nblk),
        in_specs=[
            pl.BlockSpec((1, A_BLOCK, qw), lambda b, j, n: (b, n, j)),
            kv_spec(COL_KA, -1), kv_spec(COL_KA, 0), kv_spec(COL_KA, 1),
            kv_spec(COL_VA, -1), kv_spec(COL_VA, 0), kv_spec(COL_VA, 1),
            pl.BlockSpec((LANES, LANES), lambda b, j, n: (0, j)),
            pl.BlockSpec((LANES, LANES), lambda b, j, n: (0, j)),
            pl.BlockSpec((1, A_HEADS // n_pairs, A_BLOCK, 4 * LANES),
                         lambda b, j, n: (variant(n), j, 0, 0)),
        ],
        out_specs=pl.BlockSpec((1, A_BLOCK, qw), lambda b, j, n: (b, n, j)),
        out_shape=jax.ShapeDtypeStruct((bsz, seq, A_WIDTH), BF16),
        compiler_params=_cparams(("parallel", "parallel", "arbitrary")),
        name="window_attention",
    )(u, u, u, u, u, u, u, kmeta, vmeta, bias)


def _na_bias(rpb):
    qc = jnp.arange(GRID_W, dtype=jnp.int32)
    kc = jnp.arange(GRID_W, dtype=jnp.int32)
    cs = jnp.clip(qc - NA_COLS // 2, 0, GRID_W - NA_COLS)
    cmask = (kc[None, :] >= cs[:, None]) & (kc[None, :] < cs[:, None] + NA_COLS)
    dc = jnp.clip(kc[None, :] - qc[:, None] + NA_COLS - 1, 0, 2 * NA_COLS - 2)
    n_dc = 2 * NA_COLS - 1
    onehot = (dc[None] == jnp.arange(n_dc, dtype=jnp.int32)[:, None, None]).astype(F32)
    t = jnp.einsum('hrd,dqk->hrqk', rpb.astype(F32), onehot, precision=lax.Precision.HIGHEST)
    t = jnp.where(cmask[None, None], t, NEG)
    variants = []
    for oi in range(NA_ROWS):
        variants.append(jnp.concatenate(
            [t[:, i - oi + NA_ROWS - 1] for i in range(NA_ROWS)], axis=-1))
    return jnp.stack(variants, axis=1)


def _na_kernel(q_ref, k_ref, v_ref, km_ref, vm_ref, bias_ref, o_ref, *, rows):
    span = NA_ROWS * GRID_W
    lane = lax.broadcasted_iota(jnp.int32, (GRID_W, LANES), 1)
    in_lo = lane < HEAD_DIM
    meta_bias = jnp.where(lane < N_META, 0.0, NEG).astype(F32)
    km = km_ref[...]
    vm = vm_ref[...]

    def body(it, carry):
        work = []
        for j in range(NA_UNROLL):
            r = it * NA_UNROLL + j
            rs = jnp.clip(r - NA_ROWS // 2, 0, rows - NA_ROWS)
            oi = r - rs
            q0 = pl.multiple_of(r * GRID_W, GRID_W)
            k0 = pl.multiple_of(rs * GRID_W, GRID_W)
            q = q_ref[0, pl.ds(q0, GRID_W), :]
            ks = k_ref[0, pl.ds(k0, span), :]
            for hl in range(2):
                keep = in_lo if hl == 0 else jnp.logical_not(in_lo)
                qm = jnp.where(keep, q, jnp.zeros_like(q))
                work.append((_dot_nt(qm, ks), _dot_nt(qm, km), hl, oi, k0, q0))
        probs = []
        for sw, sm, hl, oi, k0, q0 in work:
            sw = sw * SCALE + bias_ref[hl, oi]
            sm = sm * SCALE + meta_bias
            m = jnp.maximum(jnp.max(sw, axis=-1, keepdims=True),
                            jnp.max(sm, axis=-1, keepdims=True))
            ew = jnp.exp(sw - m)
            em = jnp.exp(sm - m)
            l = jnp.sum(ew, axis=-1, keepdims=True) + jnp.sum(em, axis=-1, keepdims=True)
            probs.append((ew.astype(BF16), em.astype(BF16), l, k0, q0))
        res = []
        for ew, em, l, k0, q0 in probs:
            vs = v_ref[0, pl.ds(k0, span), :]
            res.append((_dot(ew, vs) + _dot(em, vm)) / l)
        for j in range(NA_UNROLL):
            q0 = probs[2 * j][4]
            out = jnp.where(in_lo, res[2 * j], res[2 * j + 1])
            o_ref[0, pl.ds(q0, GRID_W), :] = out.astype(o_ref.dtype)
        return carry

    lax.fori_loop(0, rows // NA_UNROLL, body, 0)


def _neighbourhood_attention(u, kmeta, vmeta, bias, bsz, seq):
    rows = seq // GRID_W
    assert rows >= NA_ROWS and rows % NA_UNROLL == 0
    n_pairs = B_HEADS // 2

    def col_spec(col0):
        return pl.BlockSpec((1, seq, LANES), lambda b, j: (b, 0, col0 + j))

    return pl.pallas_call(
        functools.partial(_na_kernel, rows=rows),
        grid=(bsz, n_pairs),
        in_specs=[
            col_spec(COL_QB), col_spec(COL_KB), col_spec(COL_VB),
            pl.BlockSpec((LANES, LANES), lambda b, j: (0, j)),
            pl.BlockSpec((LANES, LANES), lambda b, j: (0, j)),
            pl.BlockSpec((2, NA_ROWS, GRID_W, NA_ROWS * GRID_W), lambda b, j: (j, 0, 0, 0)),
        ],
        out_specs=pl.BlockSpec((1, seq, LANES), lambda b, j: (b, 0, j)),
        out_shape=jax.ShapeDtypeStruct((bsz, seq, B_WIDTH), BF16),
        compiler_params=_cparams(("parallel", "parallel")),
        name="neighbourhood_attention",
    )(u, u, u, kmeta, vmeta, bias)


def _route(scores, rbias):
    t = scores.shape[-1]
    ninf = -jnp.inf
    biased = scores + rbias
    b3 = biased.reshape(N_GROUPS, GROUP_SIZE, t)
    s3 = scores.reshape(N_GROUPS, GROUP_SIZE, t)
    io_in = lax.broadcasted_iota(jnp.int32, b3.shape, 1)
    io_g3 = lax.broadcasted_iota(jnp.int32, b3.shape, 0)
    io_e = io_g3 * GROUP_SIZE + io_in
    m1 = jnp.max(b3, axis=1, keepdims=True)
    i1 = jnp.min(jnp.where(b3 == m1, io_in, GROUP_SIZE), axis=1, keepdims=True)
    m2 = jnp.max(jnp.where(io_in == i1, ninf, b3), axis=1, keepdims=True)
    gs = m1 + m2
    io_g = lax.broadcasted_iota(jnp.int32, gs.shape, 0)
    gmask = jnp.zeros(gs.shape, jnp.bool_)
    cur = gs
    for _ in range(TOPK_GROUPS):
        m = jnp.max(cur, axis=0, keepdims=True)
        i = jnp.min(jnp.where(cur == m, io_g, N_GROUPS), axis=0, keepdims=True)
        pick = io_g == i
        gmask = jnp.logical_or(gmask, pick)
        cur = jnp.where(pick, ninf, cur)
    cur = jnp.where(gmask, b3, ninf)
    picks, top_e, top_s = [], [], []
    for _ in range(TOP_K):
        m = jnp.max(jnp.max(cur, axis=1, keepdims=True), axis=0, keepdims=True)
        i = jnp.min(jnp.min(jnp.where(cur == m, io_e, N_EXPERTS), axis=1, keepdims=True),
                    axis=0, keepdims=True)
        pick = io_e == i
        picks.append(pick)
        top_e.append(i.reshape(1, t))
        w = jnp.sum(jnp.sum(jnp.where(pick, s3, 0.0), axis=1, keepdims=True), axis=0, keepdims=True)
        top_s.append(w.reshape(1, t))
        cur = jnp.where(pick, ninf, cur)
    top_e = jnp.concatenate(top_e, axis=0)
    top_s = jnp.concatenate(top_s, axis=0)
    denom = top_s[0:1]
    for k in range(1, TOP_K):
        denom = denom + top_s[k:k + 1]
    top_w = top_s / (denom + 1e-20) * ROUTED_SCALE
    sel = picks[0]
    for k in range(1, TOP_K):
        sel = jnp.logical_or(sel, picks[k])
    sel = jnp.where(sel, 1.0, 0.0).astype(F32).reshape(N_EXPERTS, t)
    return top_e, top_w, sel, picks


def _outproj_router_kernel(ya_ref, yb_ref, x_ref, wo_ref, ga_ref, gb_ref, eg_ref, eb_ref,
                           g1_ref, b1_ref, wrh_ref, wrl_ref, rb_ref,
                           h1_ref, te_ref, tw_ref, tr_ref, cnt_ref, carry):
    tm = x_ref.shape[0]

    @pl.when(pl.program_id(0) == 0)
    def _():
        carry[...] = jnp.zeros_like(carry)

    def rms(y_ref, g_ref):
        y = y_ref[...].astype(F32)
        inv = lax.rsqrt(jnp.mean(y * y, axis=-1, keepdims=True) + RMS_EPS)
        return (y * inv * g_ref[...]).astype(BF16)

    mix = (_dot(rms(ya_ref, ga_ref), wo_ref[0:A_WIDTH, :])
           + _dot(rms(yb_ref, gb_ref), wo_ref[A_WIDTH:MIX_WIDTH, :]))
    h = _layer_norm(x_ref[...], eg_ref[...], eb_ref[...])
    h1 = _layer_norm(DEEPNORM_ALPHA * h + mix, g1_ref[...], b1_ref[...])
    h1_ref[...] = h1
    hb = h1.astype(BF16)

    hlo = (h1 - hb.astype(F32)).astype(BF16)
    logits = (_dot_nt(wrh_ref[...], hb) + _dot_nt(wrh_ref[...], hlo)
              + _dot_nt(wrl_ref[...], hb))
    scores = 1.0 / (1.0 + jnp.exp(-logits))
    top_e, top_w, sel, picks = _route(scores, rb_ref[...])

    row = lax.broadcasted_iota(jnp.int32, (tm, tm), 0)
    col = lax.broadcasted_iota(jnp.int32, (tm, tm), 1)
    before = jnp.where(row < col, 1.0, 0.0).astype(BF16)
    rank = _dot(sel.astype(BF16), before) + carry[...]
    rank3 = rank.reshape(N_GROUPS, GROUP_SIZE, tm)
    ranks = []
    for k in range(TOP_K):
        rk = jnp.sum(jnp.sum(jnp.where(picks[k], rank3, 0.0), axis=1, keepdims=True),
                     axis=0, keepdims=True)
        ranks.append(rk.reshape(1, tm))
    te_ref[...] = top_e
    tw_ref[...] = top_w
    tr_ref[...] = jnp.concatenate(ranks, axis=0).astype(jnp.int32)
    carry[...] = carry[...] + jnp.sum(sel, axis=-1, keepdims=True)
    cnt_ref[...] = jnp.broadcast_to(carry[...], cnt_ref.shape)


def _outproj_router(ya, yb, x2, wo_bf16, ga, gb, eg, eb, g1, b1, wrh, wrl, rb, tm):
    m = x2.shape[0]
    row = lambda w: pl.BlockSpec((tm, w), lambda i: (i, 0))
    full = lambda a: pl.BlockSpec(a.shape, lambda i: (0,) * a.ndim)
    tok = pl.BlockSpec((TOP_K, tm), lambda i: (0, i))
    return pl.pallas_call(
        _outproj_router_kernel,
        grid=(m // tm,),
        in_specs=[row(A_WIDTH), row(B_WIDTH), row(D_MODEL), full(wo_bf16), full(ga), full(gb),
                  full(eg), full(eb), full(g1), full(b1), full(wrh), full(wrl), full(rb)],
        out_specs=[row(D_MODEL), tok, tok, tok,
                   pl.BlockSpec((N_EXPERTS, LANES), lambda i: (0, 0))],
        out_shape=[jax.ShapeDtypeStruct((m, D_MODEL), F32),
                   jax.ShapeDtypeStruct((TOP_K, m), jnp.int32),
                   jax.ShapeDtypeStruct((TOP_K, m), F32),
                   jax.ShapeDtypeStruct((TOP_K, m), jnp.int32),
                   jax.ShapeDtypeStruct((N_EXPERTS, LANES), F32)],
        scratch_shapes=[pltpu.VMEM((N_EXPERTS, 1), F32)],
        compiler_params=_cparams(("arbitrary",)),
        name="outproj_router",
    )(ya, yb, x2, wo_bf16, ga, gb, eg, eb, g1, b1, wrh, wrl, rb)


def _zero_fill_padding(pad_start_ref, pad_len_ref, n_used, xs_hbm, zeros, sem):
    n_blocks = xs_hbm.shape[0] // MOE_BLOCK
    zeros[...] = jnp.zeros_like(zeros)

    def copies(act):
        def per_expert(e, carry):
            start = pad_start_ref[e]
            n = pad_len_ref[e]
            head = jnp.minimum((-start) & (SUBLANES - 1), n)
            for j in range(SUBLANES - 1):
                @pl.when(j < head)
                def _(j=j):
                    act(pltpu.make_async_copy(zeros.at[pl.ds(0, 1), :],
                                              xs_hbm.at[pl.ds(start + j, 1), :], sem.at[1]))
            aligned = start + head
            rest = n - head
            bit = MOE_BLOCK // 2
            while bit >= SUBLANES:
                @pl.when((rest & bit) != 0)
                def _(bit=bit):
                    off = pl.multiple_of(aligned + (rest & (-2 * bit)), SUBLANES)
                    act(pltpu.make_async_copy(zeros.at[pl.ds(0, bit), :],
                                              xs_hbm.at[pl.ds(off, bit), :], sem.at[1]))
                bit //= 2
            return carry

        def per_block(j, carry):
            off = pl.multiple_of(j * MOE_BLOCK, MOE_BLOCK)
            act(pltpu.make_async_copy(zeros, xs_hbm.at[pl.ds(off, MOE_BLOCK), :], sem.at[1]))
            return carry

        lax.fori_loop(0, N_EXPERTS, per_expert, 0)
        lax.fori_loop(n_used, n_blocks, per_block, 0)

    copies(lambda cp: cp.start())
    copies(lambda cp: cp.wait())


def _dispatch_kernel(pad_start_ref, pad_len_ref, nu_ref, dest_ref, h1_ref, wsg_ref, wsu_ref,
                     wsd_ref, xs_hbm, sh_ref, packed, zeros, sem):
    @pl.when(pl.program_id(0) == 0)
    def _():
        _zero_fill_padding(pad_start_ref, pad_len_ref, nu_ref[0], xs_hbm, zeros, sem)

    tm = h1_ref.shape[0]
    h1 = h1_ref[...]
    packed[...] = _pack_rows(h1)
    for t in range(tm):
        for k in range(TOP_K):
            pltpu.make_async_copy(packed.at[pl.ds(t, 1), :],
                                  xs_hbm.at[pl.ds(dest_ref[k, t], 1), :], sem.at[0]).start()
    hb = h1.astype(BF16)
    g = _dot(hb, wsg_ref[...])
    u = _dot(hb, wsu_ref[...])
    sh_ref[...] = _dot((_silu(g) * u).astype(BF16), wsd_ref[...]).astype(sh_ref.dtype)
    for k in range(TOP_K):
        pltpu.make_async_copy(packed, xs_hbm.at[pl.ds(0, tm), :], sem.at[0]).wait()


def _dispatch_shared(pad_start, pad_len, n_used, dest, h1, wsg, wsu, wsd, p_rows):
    m = h1.shape[0]
    tm = DISPATCH_TILE
    full = lambda a: pl.BlockSpec(a.shape, lambda i, *_: (0,) * a.ndim)
    return pl.pallas_call(
        _dispatch_kernel,
        grid_spec=pltpu.PrefetchScalarGridSpec(
            num_scalar_prefetch=3,
            grid=(m // tm,),
            in_specs=[pl.BlockSpec((TOP_K, tm), lambda i, *_: (0, i), memory_space=pltpu.SMEM),
                      pl.BlockSpec((tm, D_MODEL), lambda i, *_: (i, 0)),
                      full(wsg), full(wsu), full(wsd)],
            out_specs=[pl.BlockSpec(memory_space=pl.ANY),
                       pl.BlockSpec((tm, D_MODEL), lambda i, *_: (i, 0))],
            scratch_shapes=[pltpu.VMEM((tm, D_HALF), U32),
                            pltpu.VMEM((MOE_BLOCK, D_HALF), U32),
                            pltpu.SemaphoreType.DMA((2,))],
        ),
        out_shape=[jax.ShapeDtypeStruct((p_rows, D_HALF), U32),
                   jax.ShapeDtypeStruct((m, D_MODEL), BF16)],
        compiler_params=pltpu.CompilerParams(
            dimension_semantics=("arbitrary",), vmem_limit_bytes=VMEM_LIMIT,
            disable_bounds_checks=True),
        name="dispatch_shared",
    )(pad_start, pad_len, n_used, dest, h1, wsg, wsu, wsd)


def _moe_kernel(be_ref, nu_ref, x_ref, wg_ref, wu_ref, wd_ref, y_ref, wg_b, wu_b, wd_b):
    i = pl.program_id(0)
    e = be_ref[i]
    prev = be_ref[jnp.maximum(i - 1, 0)]

    @pl.when(jnp.logical_or(i == 0, e != prev))
    def _():
        wg_b[...] = wg_ref[0].astype(BF16)
        wu_b[...] = wu_ref[0].astype(BF16)
        wd_b[...] = wd_ref[0].astype(BF16)

    @pl.when(i < nu_ref[0])
    def _():
        hi, lo = _unpack_rows(x_ref[...])
        g = _dot(hi, wg_b[0:D_HALF, :]) + _dot(lo, wg_b[D_HALF:D_MODEL, :])
        u = _dot(hi, wu_b[0:D_HALF, :]) + _dot(lo, wu_b[D_HALF:D_MODEL, :])
        y_ref[...] = _dot((_silu(g) * u).astype(BF16), wd_b[...]).astype(y_ref.dtype)

    @pl.when(i >= nu_ref[0])
    def _():
        y_ref[...] = jnp.zeros_like(y_ref)


def _moe_experts(block_e, n_used, x_sorted, w_gate, w_up, w_down):
    p = x_sorted.shape[0]
    n_blocks = p // MOE_BLOCK
    last = lambda i, be, nu: (jnp.minimum(i, nu[0] - 1), 0)
    return pl.pallas_call(
        _moe_kernel,
        grid_spec=pltpu.PrefetchScalarGridSpec(
            num_scalar_prefetch=2,
            grid=(n_blocks,),
            in_specs=[
                pl.BlockSpec((MOE_BLOCK, D_HALF), last),
                pl.BlockSpec((1, D_MODEL, D_EXPERT), lambda i, be, nu: (be[i], 0, 0)),
                pl.BlockSpec((1, D_MODEL, D_EXPERT), lambda i, be, nu: (be[i], 0, 0)),
                pl.BlockSpec((1, D_EXPERT, D_MODEL), lambda i, be, nu: (be[i], 0, 0)),
            ],
            out_specs=pl.BlockSpec((MOE_BLOCK, D_MODEL), lambda i, be, nu: (i, 0)),
            scratch_shapes=[pltpu.VMEM((D_MODEL, D_EXPERT), BF16),
                            pltpu.VMEM((D_MODEL, D_EXPERT), BF16),
                            pltpu.VMEM((D_EXPERT, D_MODEL), BF16)],
        ),
        out_shape=jax.ShapeDtypeStruct((p, D_MODEL), BF16),
        compiler_params=_cparams(("arbitrary",)),
        name="moe_experts",
    )(block_e, n_used, x_sorted, w_gate, w_up, w_down)


def _final_kernel(h1_ref, sh_ref, ys_ref, tw_ref, g2_ref, b2_ref, o_ref):
    ffn = sh_ref[...].astype(F32)
    for k in range(TOP_K):
        ffn = ffn + ys_ref[k].astype(F32) * tw_ref[:, k:k + 1]
    o_ref[...] = _layer_norm(DEEPNORM_ALPHA * h1_ref[...] + ffn, g2_ref[...], b2_ref[...])


def _final(h1, sh, ysel, tw_t, g2, b2, tm):
    m = h1.shape[0]
    row = lambda w: pl.BlockSpec((tm, w), lambda i: (i, 0))
    full = lambda a: pl.BlockSpec(a.shape, lambda i: (0,) * a.ndim)
    return pl.pallas_call(
        _final_kernel,
        grid=(m // tm,),
        in_specs=[row(D_MODEL), row(D_MODEL),
                  pl.BlockSpec((TOP_K, tm, D_MODEL), lambda i: (0, i, 0)),
                  row(TOP_K), full(g2), full(b2)],
        out_specs=row(D_MODEL),
        out_shape=jax.ShapeDtypeStruct((m, D_MODEL), F32),
        compiler_params=_cparams(("parallel",)),
        name="combine_ln2",
    )(h1, sh, ysel, tw_t, g2, b2)


def _pad_rows(a, rows):
    return jnp.concatenate([a, jnp.zeros((rows - a.shape[0],) + a.shape[1:], a.dtype)], axis=0)


def kernel(x, meta_tokens, ln_emb_g, ln_emb_b, t5_table, w_in, a_sink, na_rpb, g_norm_a, g_norm_b, w_out, ln1_g, ln1_b, w_router, router_bias, w_gate, w_up, w_down, ws_gate, ws_up, ws_down, ln2_g, ln2_b):
    bsz, seq, _ = x.shape
    m = bsz * seq
    r2 = lambda a: a.reshape(1, -1).astype(F32)
    x2 = x.reshape(m, D_MODEL)
    eg, eb = r2(ln_emb_g), r2(ln_emb_b)

    w_in_b = w_in[0].astype(BF16)
    tm = 1024 if m % 1024 == 0 else 128
    u = _ln_inproj(x2, eg, eb, w_in_b, tm, 1152).reshape(bsz, seq, IN_WIDTH)
    um = _ln_inproj(meta_tokens.astype(F32), eg, eb, w_in_b, N_META, 1152)
    cut = lambda c0, width: _pad_rows(um[:, c0 * LANES:c0 * LANES + width], LANES)

    ya = _window_attention(u, cut(COL_KA, KV_WIDTH), cut(COL_VA, KV_WIDTH),
                           _window_bias(t5_table, a_sink[0], seq), bsz, seq)
    yb = _neighbourhood_attention(u, cut(COL_KB, B_WIDTH), cut(COL_VB, B_WIDTH),
                                  _na_bias(na_rpb[0]), bsz, seq)

    wr_t = w_router[0].astype(F32).T
    wr_hi = wr_t.astype(BF16)
    wr_lo = (wr_t - wr_hi.astype(F32)).astype(BF16)
    tm4 = 256 if m % 256 == 0 else 128
    h1, top_e, top_w, top_r, cnt = _outproj_router(
        ya.reshape(m, A_WIDTH), yb.reshape(m, B_WIDTH), x2, w_out[0].astype(BF16),
        r2(g_norm_a), r2(g_norm_b), eg, eb, r2(ln1_g), r2(ln1_b), wr_hi, wr_lo,
        router_bias[0].astype(F32).reshape(N_EXPERTS, 1), tm4)

    counts = cnt[:, 0].astype(jnp.int32)
    nb_e = (counts + MOE_BLOCK - 1) // MOE_BLOCK
    bend = jnp.cumsum(nb_e)
    pstart = (bend - nb_e) * MOE_BLOCK
    n_blocks = (m * TOP_K) // MOE_BLOCK + N_EXPERTS
    p_rows = n_blocks * MOE_BLOCK
    blk = jnp.arange(n_blocks, dtype=jnp.int32)
    block_e = jnp.minimum(jnp.sum((bend[None, :] <= blk[:, None]).astype(jnp.int32), axis=1),
                          N_EXPERTS - 1)
    n_used = bend[-1:].astype(jnp.int32)
    expert_ids = jnp.arange(N_EXPERTS, dtype=jnp.int32)[:, None, None]
    dest = top_r + jnp.sum(jnp.where(top_e[None] == expert_ids, pstart[:, None, None], 0),
                           axis=0)
    x_sorted, shared = _dispatch_shared(
        pstart + counts, nb_e * MOE_BLOCK - counts, n_used, dest, h1,
        ws_gate[0].astype(BF16), ws_up[0].astype(BF16), ws_down[0].astype(BF16), p_rows)
    y_sorted = _moe_experts(block_e, n_used, x_sorted, w_gate[0], w_up[0], w_down[0])
    ysel = y_sorted.at[dest.reshape(-1)].get(
        mode='promise_in_bounds', unique_indices=True).reshape(TOP_K, m, D_MODEL)

    out = _final(h1, shared, ysel, top_w.T, r2(ln2_g), r2(ln2_b), tm4)
    return out.reshape(bsz, seq, D_MODEL)
```

```python
import functools
import math

import jax
import jax.numpy as jnp
from jax import lax
from jax.experimental import pallas as pl
from jax.experimental.pallas import tpu as pltpu

F32 = jnp.float32
BF16 = jnp.bfloat16

D_MODEL = 2048
HEAD_DIM = 64
N_META = 16
GRID_W = 64
A_HEADS = 16
A_KV_HEADS = 4
A_WINDOW = 128
A_BLOCK = 128
N_BUCKETS = 32
MAX_DISTANCE = 128
B_HEADS = 16
NA_ROWS = 8
NA_COLS = 16
A_WIDTH = A_HEADS * HEAD_DIM
KV_WIDTH = A_KV_HEADS * HEAD_DIM
B_WIDTH = B_HEADS * HEAD_DIM
MIX_WIDTH = A_WIDTH + B_WIDTH
IN_WIDTH = A_WIDTH + 2 * KV_WIDTH + 3 * B_WIDTH
N_EXPERTS = 64
TOP_K = 8
N_GROUPS = 8
GROUP_SIZE = N_EXPERTS // N_GROUPS
TOPK_GROUPS = 4
D_EXPERT = 512
D_SHARED = 512
ROUTED_SCALE = 2.5
DEPTH = 1
DEEPNORM_ALPHA = (2 * DEPTH) ** 0.25
LN_EPS = 1e-5
RMS_EPS = 1e-6
NEG = -1e30
SCALE = HEAD_DIM ** -0.5

LANES = 128
SUBLANES = 8
VMEM_LIMIT = 56 * 1024 * 1024

COL_QA = 0
COL_KA = A_WIDTH // LANES
COL_VA = (A_WIDTH + KV_WIDTH) // LANES
COL_QB = (A_WIDTH + 2 * KV_WIDTH) // LANES
COL_KB = COL_QB + B_WIDTH // LANES
COL_VB = COL_KB + B_WIDTH // LANES

WIN_BATCH = 8
NA_UNROLL = 8
MOE_BLOCK = 256
DISPATCH_TILE = 128
D_HALF = D_MODEL // 2
U32 = jnp.uint32


def _pack_rows(x):
    hi = lax.bitcast_convert_type(x[:, :D_HALF].astype(jnp.bfloat16).astype(F32), U32)
    lo = lax.bitcast_convert_type(x[:, D_HALF:].astype(jnp.bfloat16).astype(F32), U32)
    return hi | (lo >> 16)


def _unpack_rows(p):
    hi = lax.bitcast_convert_type(p & jnp.uint32(0xFFFF0000), F32).astype(BF16)
    lo = lax.bitcast_convert_type(p << 16, F32).astype(BF16)
    return hi, lo


def _cparams(sem):
    return pltpu.CompilerParams(dimension_semantics=sem, vmem_limit_bytes=VMEM_LIMIT)


def _layer_norm(x, g, b):
    mu = jnp.mean(x, axis=-1, keepdims=True)
    xc = x - mu
    var = jnp.mean(xc * xc, axis=-1, keepdims=True)
    return xc * lax.rsqrt(var + LN_EPS) * g + b


def _dot(a, b):
    return jnp.dot(a, b, preferred_element_type=F32)


def _dot_nt(a, b):
    return lax.dot_general(a, b, (((1,), (1,)), ((), ())), preferred_element_type=F32)


def _silu(g):
    return g / (1.0 + jnp.exp(-g))


def _ln_inproj_kernel(x_ref, g_ref, b_ref, w_ref, o_ref, h_scr):
    @pl.when(pl.program_id(1) == 0)
    def _():
        h_scr[...] = _layer_norm(x_ref[...], g_ref[...], b_ref[...]).astype(BF16)

    o_ref[...] = _dot(h_scr[...], w_ref[...]).astype(o_ref.dtype)


def _ln_inproj(x2, g, b, w_bf16, tm, tn):
    m = x2.shape[0]
    n = w_bf16.shape[1]
    return pl.pallas_call(
        _ln_inproj_kernel,
        grid=(m // tm, n // tn),
        in_specs=[
            pl.BlockSpec((tm, D_MODEL), lambda i, j: (i, 0)),
            pl.BlockSpec((1, D_MODEL), lambda i, j: (0, 0)),
            pl.BlockSpec((1, D_MODEL), lambda i, j: (0, 0)),
            pl.BlockSpec((D_MODEL, tn), lambda i, j: (0, j)),
        ],
        out_specs=pl.BlockSpec((tm, tn), lambda i, j: (i, j)),
        out_shape=jax.ShapeDtypeStruct((m, n), BF16),
        scratch_shapes=[pltpu.VMEM((tm, D_MODEL), BF16)],
        compiler_params=_cparams(("parallel", "arbitrary")),
        name="ln_inproj",
    )(x2, g, b, w_bf16)


def _t5_bucket(rel):
    nb = N_BUCKETS // 2
    max_exact = nb // 2
    ret = jnp.where(rel > 0, nb, 0)
    n = jnp.abs(rel)
    nf = jnp.maximum(n, 1).astype(F32)
    large = max_exact + (jnp.log(nf / max_exact) / math.log(MAX_DISTANCE / max_exact)
                         * (nb - max_exact)).astype(jnp.int32)
    large = jnp.minimum(large, nb - 1)
    return ret + jnp.where(n < max_exact, n, large)


def _lookup(table_t, idx, n):
    onehot = (idx[None] == jnp.arange(n, dtype=jnp.int32).reshape((n,) + (1,) * idx.ndim))
    return jnp.einsum('hb,b...->h...', table_t, onehot.astype(F32),
                      precision=lax.Precision.HIGHEST)


def _window_bias(t5_table, a_sink, seq):
    assert N_META + A_BLOCK - (N_META - 1) > MAX_DISTANCE
    nblk = seq // A_BLOCK
    t5_t = t5_table.astype(F32).T
    q_loc = jnp.arange(A_BLOCK, dtype=jnp.int32)
    k_loc = jnp.arange(3 * A_BLOCK, dtype=jnp.int32) - A_BLOCK
    rel = k_loc[None, :] - q_loc[:, None]
    band = _lookup(t5_t, _t5_bucket(rel), N_BUCKETS)
    win = jnp.abs(rel) <= A_WINDOW
    sink = jnp.broadcast_to(a_sink.astype(F32)[:, None, None], (A_HEADS, A_BLOCK, 1))
    pad = jnp.full((A_HEADS, A_BLOCK, LANES - N_META - 1), NEG, F32)
    out = []
    for blk in (0, min(1, nblk - 1), nblk - 1):
        gk = blk * A_BLOCK + k_loc
        valid = win & (gk >= 0)[None, :] & (gk < seq)[None, :]
        band_v = jnp.where(valid[None], band, NEG)
        q_pos = N_META + blk * A_BLOCK + q_loc
        rel_m = jnp.arange(N_META, dtype=jnp.int32)[None, :] - q_pos[:, None]
        bias_m = _lookup(t5_t, _t5_bucket(rel_m), N_BUCKETS)
        out.append(jnp.concatenate([bias_m, sink, pad, band_v], axis=-1))
    return jnp.stack(out, axis=0)


def _window_kernel(q_ref, kp_ref, kc_ref, kn_ref, vp_ref, vc_ref, vn_ref, km_ref, vm_ref,
                   bias_ref, o_ref):
    kall = jnp.concatenate([km_ref[...], kp_ref[0], kc_ref[0], kn_ref[0]], axis=0)
    vall = jnp.concatenate([vm_ref[...], vp_ref[0], vc_ref[0], vn_ref[0]], axis=0)
    lane = lax.broadcasted_iota(jnp.int32, (A_BLOCK, LANES), 1)
    in_lo = lane < HEAD_DIM
    group = A_HEADS // A_KV_HEADS
    n_heads = 2 * group
    pair_out = []
    for h0 in range(0, n_heads, WIN_BATCH):
        scores = []
        for hl in range(h0, h0 + WIN_BATCH):
            p, half = hl // 2, hl % 2
            kv_half = hl // group
            qp = q_ref[0, :, p * LANES:(p + 1) * LANES].astype(F32) * SCALE
            src = qp if half == kv_half else pltpu.roll(qp, HEAD_DIM, axis=1)
            keep = in_lo if kv_half == 0 else jnp.logical_not(in_lo)
            qm = jnp.where(keep, src, 0.0).astype(BF16)
            scores.append(_dot_nt(qm, kall))
        probs = []
        for j, s in enumerate(scores):
            s = s + bias_ref[0, h0 + j]
            m = jnp.max(s, axis=-1, keepdims=True)
            e = jnp.exp(s - m)
            probs.append((e.astype(BF16), jnp.sum(e, axis=-1, keepdims=True)))
        res = []
        for j, (e, l) in enumerate(probs):
            hl = h0 + j
            o = _dot(e, vall) / l
            if hl % 2 != hl // group:
                o = pltpu.roll(o, HEAD_DIM, axis=1)
            res.append(o)
        for j in range(0, WIN_BATCH, 2):
            pair_out.append(jnp.where(in_lo, res[j], res[j + 1]))
    o_ref[0] = jnp.concatenate(pair_out, axis=1).astype(o_ref.dtype)


def _window_attention(u, kmeta, vmeta, bias, bsz, seq):
    nblk = seq // A_BLOCK
    n_pairs = A_KV_HEADS // 2
    qw = A_WIDTH // n_pairs

    def variant(n):
        return jnp.where(n == 0, 0, jnp.where(n == nblk - 1, 2, 1))

    def kv_spec(col0, shift):
        return pl.BlockSpec(
            (1, A_BLOCK, LANES),
            lambda b, j, n: (b, jnp.clip(n + shift, 0, nblk - 1), col0 + j))

    return pl.pallas_call(
        _window_kernel,
        grid=(bsz, n_pairs, nblk),
        in_specs=[
            pl.BlockSpec((1, A_BLOCK, qw), lambda b, j, n: (b, n, j)),
            kv_spec(COL_KA, -1), kv_spec(COL_KA, 0), kv_spec(COL_KA, 1),
            kv_spec(COL_VA, -1), kv_spec(COL_VA, 0), kv_spec(COL_VA, 1),
            pl.BlockSpec((LANES, LANES), lambda b, j, n: (0, j)),
            pl.BlockSpec((LANES, LANES), lambda b, j, n: (0, j)),
            pl.BlockSpec((1, A_HEADS // n_pairs, A_BLOCK, 4 * LANES),
                         lambda b, j, n: (variant(n), j, 0, 0)),
        ],
        out_specs=pl.BlockSpec((1, A_BLOCK, qw), lambda b, j, n: (b, n, j)),
        out_shape=jax.ShapeDtypeStruct((bsz, seq, A_WIDTH), BF16),
        compiler_params=_cparams(("parallel", "parallel", "arbitrary")),
        name="window_attention",
    )(u, u, u, u, u, u, u, kmeta, vmeta, bias)


def _na_bias(rpb):
    qc = jnp.arange(GRID_W, dtype=jnp.int32)
    kc = jnp.arange(GRID_W, dtype=jnp.int32)
    cs = jnp.clip(qc - NA_COLS // 2, 0, GRID_W - NA_COLS)
    cmask = (kc[None, :] >= cs[:, None]) & (kc[None, :] < cs[:, None] + NA_COLS)
    dc = jnp.clip(kc[None, :] - qc[:, None] + NA_COLS - 1, 0, 2 * NA_COLS - 2)
    n_dc = 2 * NA_COLS - 1
    onehot = (dc[None] == jnp.arange(n_dc, dtype=jnp.int32)[:, None, None]).astype(F32)
    t = jnp.einsum('hrd,dqk->hrqk', rpb.astype(F32), onehot, precision=lax.Precision.HIGHEST)
    t = jnp.where(cmask[None, None], t, NEG)
    variants = []
    for oi in range(NA_ROWS):
        variants.append(jnp.concatenate(
            [t[:, i - oi + NA_ROWS - 1] for i in range(NA_ROWS)], axis=-1))
    return jnp.stack(variants, axis=1)


def _na_kernel(q_ref, k_ref, v_ref, km_ref, vm_ref, bias_ref, o_ref, *, rows):
    span = NA_ROWS * GRID_W
    lane = lax.broadcasted_iota(jnp.int32, (GRID_W, LANES), 1)
    in_lo = lane < HEAD_DIM
    meta_bias = jnp.where(lane < N_META, 0.0, NEG).astype(F32)
    km = km_ref[...]
    vm = vm_ref[...]

    def body(it, carry):
        work = []
        for j in range(NA_UNROLL):
            r = it * NA_UNROLL + j
            rs = jnp.clip(r - NA_ROWS // 2, 0, rows - NA_ROWS)
            oi = r - rs
            q0 = pl.multiple_of(r * GRID_W, GRID_W)
            k0 = pl.multiple_of(rs * GRID_W, GRID_W)
            q = q_ref[0, pl.ds(q0, GRID_W), :]
            ks = k_ref[0, pl.ds(k0, span), :]
            for hl in range(2):
                keep = in_lo if hl == 0 else jnp.logical_not(in_lo)
                qm = jnp.where(keep, q, jnp.zeros_like(q))
                work.append((_dot_nt(qm, ks), _dot_nt(qm, km), hl, oi, k0, q0))
        probs = []
        for sw, sm, hl, oi, k0, q0 in work:
            sw = sw * SCALE + bias_ref[hl, oi]
            sm = sm * SCALE + meta_bias
            m = jnp.maximum(jnp.max(sw, axis=-1, keepdims=True),
                            jnp.max(sm, axis=-1, keepdims=True))
            ew = jnp.exp(sw - m)
            em = jnp.exp(sm - m)
            l = jnp.sum(ew, axis=-1, keepdims=True) + jnp.sum(em, axis=-1, keepdims=True)
            probs.append((ew.astype(BF16), em.astype(BF16), l, k0, q0))
        res = []
        for ew, em, l, k0, q0 in probs:
            vs = v_ref[0, pl.ds(k0, span), :]
            res.append((_dot(ew, vs) + _dot(em, vm)) / l)
        for j in range(NA_UNROLL):
            q0 = probs[2 * j][4]
            out = jnp.where(in_lo, res[2 * j], res[2 * j + 1])
            o_ref[0, pl.ds(q0, GRID_W), :] = out.astype(o_ref.dtype)
        return carry

    lax.fori_loop(0, rows // NA_UNROLL, body, 0)


def _neighbourhood_attention(u, kmeta, vmeta, bias, bsz, seq):
    rows = seq // GRID_W
    assert rows >= NA_ROWS and rows % NA_UNROLL == 0
    n_pairs = B_HEADS // 2

    def col_spec(col0):
        return pl.BlockSpec((1, seq, LANES), lambda b, j: (b, 0, col0 + j))

    return pl.pallas_call(
        functools.partial(_na_kernel, rows=rows),
        grid=(bsz, n_pairs),
        in_specs=[
            col_spec(COL_QB), col_spec(COL_KB), col_spec(COL_VB),
            pl.BlockSpec((LANES, LANES), lambda b, j: (0, j)),
            pl.BlockSpec((LANES, LANES), lambda b, j: (0, j)),
            pl.BlockSpec((2, NA_ROWS, GRID_W, NA_ROWS * GRID_W), lambda b, j: (j, 0, 0, 0)),
        ],
        out_specs=pl.BlockSpec((1, seq, LANES), lambda b, j: (b, 0, j)),
        out_shape=jax.ShapeDtypeStruct((bsz, seq, B_WIDTH), BF16),
        compiler_params=_cparams(("parallel", "parallel")),
        name="neighbourhood_attention",
    )(u, u, u, kmeta, vmeta, bias)


def _route(scores, rbias):
    t = scores.shape[-1]
    ninf = -jnp.inf
    biased = scores + rbias
    b3 = biased.reshape(N_GROUPS, GROUP_SIZE, t)
    s3 = scores.reshape(N_GROUPS, GROUP_SIZE, t)
    io_in = lax.broadcasted_iota(jnp.int32, b3.shape, 1)
    io_g3 = lax.broadcasted_iota(jnp.int32, b3.shape, 0)
    io_e = io_g3 * GROUP_SIZE + io_in
    m1 = jnp.max(b3, axis=1, keepdims=True)
    i1 = jnp.min(jnp.where(b3 == m1, io_in, GROUP_SIZE), axis=1, keepdims=True)
    m2 = jnp.max(jnp.where(io_in == i1, ninf, b3), axis=1, keepdims=True)
    gs = m1 + m2
    io_g = lax.broadcasted_iota(jnp.int32, gs.shape, 0)
    gmask = jnp.zeros(gs.shape, jnp.bool_)
    cur = gs
    for _ in range(TOPK_GROUPS):
        m = jnp.max(cur, axis=0, keepdims=True)
        i = jnp.min(jnp.where(cur == m, io_g, N_GROUPS), axis=0, keepdims=True)
        pick = io_g == i
        gmask = jnp.logical_or(gmask, pick)
        cur = jnp.where(pick, ninf, cur)
    cur = jnp.where(gmask, b3, ninf)
    picks, top_e, top_s = [], [], []
    for _ in range(TOP_K):
        m = jnp.max(jnp.max(cur, axis=1, keepdims=True), axis=0, keepdims=True)
        i = jnp.min(jnp.min(jnp.where(cur == m, io_e, N_EXPERTS), axis=1, keepdims=True),
                    axis=0, keepdims=True)
        pick = io_e == i
        picks.append(pick)
        top_e.append(i.reshape(1, t))
        w = jnp.sum(jnp.sum(jnp.where(pick, s3, 0.0), axis=1, keepdims=True), axis=0, keepdims=True)
        top_s.append(w.reshape(1, t))
        cur = jnp.where(pick, ninf, cur)
    top_e = jnp.concatenate(top_e, axis=0)
    top_s = jnp.concatenate(top_s, axis=0)
    denom = top_s[0:1]
    for k in range(1, TOP_K):
        denom = denom + top_s[k:k + 1]
    top_w = top_s / (denom + 1e-20) * ROUTED_SCALE
    sel = picks[0]
    for k in range(1, TOP_K):
        sel = jnp.logical_or(sel, picks[k])
    sel = jnp.where(sel, 1.0, 0.0).astype(F32).reshape(N_EXPERTS, t)
    return top_e, top_w, sel, picks


def _outproj_router_kernel(ya_ref, yb_ref, x_ref, wo_ref, ga_ref, gb_ref, eg_ref, eb_ref,
                           g1_ref, b1_ref, wrh_ref, wrl_ref, rb_ref,
                           h1_ref, te_ref, tw_ref, tr_ref, cnt_ref, carry):
    tm = x_ref.shape[0]

    @pl.when(pl.program_id(0) == 0)
    def _():
        carry[...] = jnp.zeros_like(carry)

    def rms(y_ref, g_ref):
        y = y_ref[...].astype(F32)
        inv = lax.rsqrt(jnp.mean(y * y, axis=-1, keepdims=True) + RMS_EPS)
        return (y * inv * g_ref[...]).astype(BF16)

    mix = _dot(jnp.concatenate([rms(ya_ref, ga_ref), rms(yb_ref, gb_ref)], axis=1), wo_ref[...])
    h = _layer_norm(x_ref[...], eg_ref[...], eb_ref[...])
    h1 = _layer_norm(DEEPNORM_ALPHA * h + mix, g1_ref[...], b1_ref[...])
    h1_ref[...] = h1
    hb = h1.astype(BF16)

    hlo = (h1 - hb.astype(F32)).astype(BF16)
    logits = (_dot_nt(wrh_ref[...], hb) + _dot_nt(wrh_ref[...], hlo)
              + _dot_nt(wrl_ref[...], hb))
    scores = 1.0 / (1.0 + jnp.exp(-logits))
    top_e, top_w, sel, picks = _route(scores, rb_ref[...])

    row = lax.broadcasted_iota(jnp.int32, (tm, tm), 0)
    col = lax.broadcasted_iota(jnp.int32, (tm, tm), 1)
    before = jnp.where(row < col, 1.0, 0.0).astype(BF16)
    rank = _dot(sel.astype(BF16), before) + carry[...]
    rank3 = rank.reshape(N_GROUPS, GROUP_SIZE, tm)
    ranks = []
    for k in range(TOP_K):
        rk = jnp.sum(jnp.sum(jnp.where(picks[k], rank3, 0.0), axis=1, keepdims=True),
                     axis=0, keepdims=True)
        ranks.append(rk.reshape(1, tm))
    te_ref[...] = top_e
    tw_ref[...] = top_w
    tr_ref[...] = jnp.concatenate(ranks, axis=0).astype(jnp.int32)
    carry[...] = carry[...] + jnp.sum(sel, axis=-1, keepdims=True)
    cnt_ref[...] = jnp.broadcast_to(carry[...], cnt_ref.shape)


def _outproj_router(ya, yb, x2, wo_bf16, ga, gb, eg, eb, g1, b1, wrh, wrl, rb, tm):
    m = x2.shape[0]
    row = lambda w: pl.BlockSpec((tm, w), lambda i: (i, 0))
    full = lambda a: pl.BlockSpec(a.shape, lambda i: (0,) * a.ndim)
    tok = pl.BlockSpec((TOP_K, tm), lambda i: (0, i))
    return pl.pallas_call(
        _outproj_router_kernel,
        grid=(m // tm,),
        in_specs=[row(A_WIDTH), row(B_WIDTH), row(D_MODEL), full(wo_bf16), full(ga), full(gb),
                  full(eg), full(eb), full(g1), full(b1), full(wrh), full(wrl), full(rb)],
        out_specs=[row(D_MODEL), tok, tok, tok,
                   pl.BlockSpec((N_EXPERTS, LANES), lambda i: (0, 0))],
        out_shape=[jax.ShapeDtypeStruct((m, D_MODEL), F32),
                   jax.ShapeDtypeStruct((TOP_K, m), jnp.int32),
                   jax.ShapeDtypeStruct((TOP_K, m), F32),
                   jax.ShapeDtypeStruct((TOP_K, m), jnp.int32),
                   jax.ShapeDtypeStruct((N_EXPERTS, LANES), F32)],
        scratch_shapes=[pltpu.VMEM((N_EXPERTS, 1), F32)],
        compiler_params=_cparams(("arbitrary",)),
        name="outproj_router",
    )(ya, yb, x2, wo_bf16, ga, gb, eg, eb, g1, b1, wrh, wrl, rb)


def _zero_fill_padding(pad_start_ref, pad_len_ref, n_used, xs_hbm, zeros, sem):
    n_blocks = xs_hbm.shape[0] // MOE_BLOCK
    zeros[...] = jnp.zeros_like(zeros)

    def copies(act):
        def per_expert(e, carry):
            start = pad_start_ref[e]
            n = pad_len_ref[e]
            head = jnp.minimum((-start) & (SUBLANES - 1), n)
            for j in range(SUBLANES - 1):
                @pl.when(j < head)
                def _(j=j):
                    act(pltpu.make_async_copy(zeros.at[pl.ds(0, 1), :],
                                              xs_hbm.at[pl.ds(start + j, 1), :], sem.at[1]))
            aligned = start + head
            rest = n - head
            bit = MOE_BLOCK // 2
            while bit >= SUBLANES:
                @pl.when((rest & bit) != 0)
                def _(bit=bit):
                    off = pl.multiple_of(aligned + (rest & (-2 * bit)), SUBLANES)
                    act(pltpu.make_async_copy(zeros.at[pl.ds(0, bit), :],
                                              xs_hbm.at[pl.ds(off, bit), :], sem.at[1]))
                bit //= 2
            return carry

        def per_block(j, carry):
            off = pl.multiple_of(j * MOE_BLOCK, MOE_BLOCK)
            act(pltpu.make_async_copy(zeros, xs_hbm.at[pl.ds(off, MOE_BLOCK), :], sem.at[1]))
            return carry

        lax.fori_loop(0, N_EXPERTS, per_expert, 0)
        lax.fori_loop(n_used, n_blocks, per_block, 0)

    copies(lambda cp: cp.start())
    copies(lambda cp: cp.wait())


def _dispatch_kernel(pad_start_ref, pad_len_ref, nu_ref, dest_ref, h1_ref, wsg_ref, wsu_ref,
                     wsd_ref, xs_hbm, sh_ref, packed, zeros, sem):
    @pl.when(pl.program_id(0) == 0)
    def _():
        _zero_fill_padding(pad_start_ref, pad_len_ref, nu_ref[0], xs_hbm, zeros, sem)

    tm = h1_ref.shape[0]
    h1 = h1_ref[...]
    packed[...] = _pack_rows(h1)
    for t in range(tm):
        for k in range(TOP_K):
            pltpu.make_async_copy(packed.at[pl.ds(t, 1), :],
                                  xs_hbm.at[pl.ds(dest_ref[k, t], 1), :], sem.at[0]).start()
    hb = h1.astype(BF16)
    g = _dot(hb, wsg_ref[...])
    u = _dot(hb, wsu_ref[...])
    sh_ref[...] = _dot((_silu(g) * u).astype(BF16), wsd_ref[...]).astype(sh_ref.dtype)
    for k in range(TOP_K):
        pltpu.make_async_copy(packed, xs_hbm.at[pl.ds(0, tm), :], sem.at[0]).wait()


def _dispatch_shared(pad_start, pad_len, n_used, dest, h1, wsg, wsu, wsd, p_rows):
    m = h1.shape[0]
    tm = DISPATCH_TILE
    full = lambda a: pl.BlockSpec(a.shape, lambda i, *_: (0,) * a.ndim)
    return pl.pallas_call(
        _dispatch_kernel,
        grid_spec=pltpu.PrefetchScalarGridSpec(
            num_scalar_prefetch=3,
            grid=(m // tm,),
            in_specs=[pl.BlockSpec((TOP_K, tm), lambda i, *_: (0, i), memory_space=pltpu.SMEM),
                      pl.BlockSpec((tm, D_MODEL), lambda i, *_: (i, 0)),
                      full(wsg), full(wsu), full(wsd)],
            out_specs=[pl.BlockSpec(memory_space=pl.ANY),
                       pl.BlockSpec((tm, D_MODEL), lambda i, *_: (i, 0))],
            scratch_shapes=[pltpu.VMEM((tm, D_HALF), U32),
                            pltpu.VMEM((MOE_BLOCK, D_HALF), U32),
                            pltpu.SemaphoreType.DMA((2,))],
        ),
        out_shape=[jax.ShapeDtypeStruct((p_rows, D_HALF), U32),
                   jax.ShapeDtypeStruct((m, D_MODEL), BF16)],
        compiler_params=pltpu.CompilerParams(
            dimension_semantics=("arbitrary",), vmem_limit_bytes=VMEM_LIMIT,
            disable_bounds_checks=True),
        name="dispatch_shared",
    )(pad_start, pad_len, n_used, dest, h1, wsg, wsu, wsd)


def _moe_kernel(first_ref, nb_ref, nu_ref, x_hbm, wg_ref, wu_ref, wd_ref, y_hbm,
                xbuf, ybuf, xsem, ysem, wg_b, wu_b, wd_b):
    e = pl.program_id(0)
    first = first_ref[e]
    nb = nb_ref[e]
    n_blocks = y_hbm.shape[0] // MOE_BLOCK

    def rows(j):
        return pl.ds(pl.multiple_of((first + j) * MOE_BLOCK, MOE_BLOCK), MOE_BLOCK)

    def x_copy(j, slot):
        return pltpu.make_async_copy(x_hbm.at[rows(j), :], xbuf.at[slot], xsem.at[slot])

    def y_copy(j, slot):
        return pltpu.make_async_copy(ybuf.at[slot], y_hbm.at[rows(j), :], ysem.at[slot])

    @pl.when(nb > 0)
    def _():
        x_copy(0, 0).start()
        wg_b[...] = wg_ref[0].astype(BF16)
        wu_b[...] = wu_ref[0].astype(BF16)
        wd_b[...] = wd_ref[0].astype(BF16)

    def body(j, carry):
        slot = j % 2
        x_copy(j, slot).wait()

        @pl.when(j + 1 < nb)
        def _():
            x_copy(j + 1, 1 - slot).start()

        @pl.when(j >= 2)
        def _():
            y_copy(j - 2, slot).wait()

        hi, lo = _unpack_rows(xbuf[slot])
        g = _dot(hi, wg_b[0:D_HALF, :]) + _dot(lo, wg_b[D_HALF:D_MODEL, :])
        u = _dot(hi, wu_b[0:D_HALF, :]) + _dot(lo, wu_b[D_HALF:D_MODEL, :])
        ybuf[slot] = _dot((_silu(g) * u).astype(BF16), wd_b[...]).astype(ybuf.dtype)
        y_copy(j, slot).start()
        return carry

    lax.fori_loop(0, nb, body, 0)

    for back in (2, 1):
        @pl.when(nb >= back)
        def _(back=back):
            y_copy(nb - back, (nb - back) % 2).wait()

    @pl.when(e == pl.num_programs(0) - 1)
    def _():
        ybuf[0] = jnp.zeros(ybuf.shape[1:], ybuf.dtype)

        def fill(j, carry):
            cp = pltpu.make_async_copy(
                ybuf.at[0], y_hbm.at[pl.ds(pl.multiple_of(j * MOE_BLOCK, MOE_BLOCK), MOE_BLOCK), :],
                ysem.at[0])
            cp.start()
            cp.wait()
            return carry

        lax.fori_loop(nu_ref[0], n_blocks, fill, 0)


def _moe_experts(first_block, blocks_per_expert, n_used, x_sorted, w_gate, w_up, w_down):
    p = x_sorted.shape[0]
    weights = lambda shape: pl.BlockSpec((1,) + shape, lambda e, *_: (e, 0, 0))
    return pl.pallas_call(
        _moe_kernel,
        grid_spec=pltpu.PrefetchScalarGridSpec(
            num_scalar_prefetch=3,
            grid=(N_EXPERTS,),
            in_specs=[
                pl.BlockSpec(memory_space=pl.ANY),
                weights((D_MODEL, D_EXPERT)), weights((D_MODEL, D_EXPERT)),
                weights((D_EXPERT, D_MODEL)),
            ],
            out_specs=pl.BlockSpec(memory_space=pl.ANY),
            scratch_shapes=[pltpu.VMEM((2, MOE_BLOCK, D_HALF), U32),
                            pltpu.VMEM((2, MOE_BLOCK, D_MODEL), BF16),
                            pltpu.SemaphoreType.DMA((2,)),
                            pltpu.SemaphoreType.DMA((2,)),
                            pltpu.VMEM((D_MODEL, D_EXPERT), BF16),
                            pltpu.VMEM((D_MODEL, D_EXPERT), BF16),
                            pltpu.VMEM((D_EXPERT, D_MODEL), BF16)],
        ),
        out_shape=jax.ShapeDtypeStruct((p, D_MODEL), BF16),
        compiler_params=pltpu.CompilerParams(
            dimension_semantics=("arbitrary",), vmem_limit_bytes=VMEM_LIMIT,
            disable_bounds_checks=True),
        name="moe_experts",
    )(first_block, blocks_per_expert, n_used, x_sorted, w_gate, w_up, w_down)


def _final_kernel(h1_ref, sh_ref, ys_ref, tw_ref, g2_ref, b2_ref, o_ref):
    ffn = sh_ref[...].astype(F32)
    for k in range(TOP_K):
        ffn = ffn + ys_ref[k].astype(F32) * tw_ref[:, k:k + 1]
    o_ref[...] = _layer_norm(DEEPNORM_ALPHA * h1_ref[...] + ffn, g2_ref[...], b2_ref[...])


def _final(h1, sh, ysel, tw_t, g2, b2, tm):
    m = h1.shape[0]
    row = lambda w: pl.BlockSpec((tm, w), lambda i: (i, 0))
    full = lambda a: pl.BlockSpec(a.shape, lambda i: (0,) * a.ndim)
    return pl.pallas_call(
        _final_kernel,
        grid=(m // tm,),
        in_specs=[row(D_MODEL), row(D_MODEL),
                  pl.BlockSpec((TOP_K, tm, D_MODEL), lambda i: (0, i, 0)),
                  row(TOP_K), full(g2), full(b2)],
        out_specs=row(D_MODEL),
        out_shape=jax.ShapeDtypeStruct((m, D_MODEL), F32),
        compiler_params=_cparams(("parallel",)),
        name="combine_ln2",
    )(h1, sh, ysel, tw_t, g2, b2)


def _pad_rows(a, rows):
    return jnp.concatenate([a, jnp.zeros((rows - a.shape[0],) + a.shape[1:], a.dtype)], axis=0)


def kernel(x, meta_tokens, ln_emb_g, ln_emb_b, t5_table, w_in, a_sink, na_rpb, g_norm_a, g_norm_b, w_out, ln1_g, ln1_b, w_router, router_bias, w_gate, w_up, w_down, ws_gate, ws_up, ws_down, ln2_g, ln2_b):
    bsz, seq, _ = x.shape
    m = bsz * seq
    r2 = lambda a: a.reshape(1, -1).astype(F32)
    x2 = x.reshape(m, D_MODEL)
    eg, eb = r2(ln_emb_g), r2(ln_emb_b)

    w_in_b = w_in[0].astype(BF16)
    tm = 1024 if m % 1024 == 0 else 128
    u = _ln_inproj(x2, eg, eb, w_in_b, tm, 1152).reshape(bsz, seq, IN_WIDTH)
    um = _ln_inproj(meta_tokens.astype(F32), eg, eb, w_in_b, N_META, 1152)
    cut = lambda c0, width: _pad_rows(um[:, c0 * LANES:c0 * LANES + width], LANES)

    ya = _window_attention(u, cut(COL_KA, KV_WIDTH), cut(COL_VA, KV_WIDTH),
                           _window_bias(t5_table, a_sink[0], seq), bsz, seq)
    yb = _neighbourhood_attention(u, cut(COL_KB, B_WIDTH), cut(COL_VB, B_WIDTH),
                                  _na_bias(na_rpb[0]), bsz, seq)

    wr_t = w_router[0].astype(F32).T
    wr_hi = wr_t.astype(BF16)
    wr_lo = (wr_t - wr_hi.astype(F32)).astype(BF16)
    tm4 = 256 if m % 256 == 0 else 128
    h1, top_e, top_w, top_r, cnt = _outproj_router(
        ya.reshape(m, A_WIDTH), yb.reshape(m, B_WIDTH), x2, w_out[0].astype(BF16),
        r2(g_norm_a), r2(g_norm_b), eg, eb, r2(ln1_g), r2(ln1_b), wr_hi, wr_lo,
        router_bias[0].astype(F32).reshape(N_EXPERTS, 1), tm4)

    counts = cnt[:, 0].astype(jnp.int32)
    nb_e = (counts + MOE_BLOCK - 1) // MOE_BLOCK
    bend = jnp.cumsum(nb_e)
    pstart = (bend - nb_e) * MOE_BLOCK
    n_blocks = (m * TOP_K) // MOE_BLOCK + N_EXPERTS
    p_rows = n_blocks * MOE_BLOCK
    n_used = bend[-1:].astype(jnp.int32)
    expert_ids = jnp.arange(N_EXPERTS, dtype=jnp.int32)[:, None, None]
    dest = top_r + jnp.sum(jnp.where(top_e[None] == expert_ids, pstart[:, None, None], 0),
                           axis=0)
    x_sorted, shared = _dispatch_shared(
        pstart + counts, nb_e * MOE_BLOCK - counts, n_used, dest, h1,
        ws_gate[0].astype(BF16), ws_up[0].astype(BF16), ws_down[0].astype(BF16), p_rows)
    y_sorted = _moe_experts(bend - nb_e, nb_e, n_used, x_sorted, w_gate[0], w_up[0], w_down[0])
    ysel = y_sorted.at[dest.reshape(-1)].get(
        mode='promise_in_bounds', unique_indices=True).reshape(TOP_K, m, D_MODEL)

    out = _final(h1, shared, ysel, top_w.T, r2(ln2_g), r2(ln2_b), tm4)
    return out.reshape(bsz, seq, D_MODEL)
```

```python
import functools
import math

import jax
import jax.numpy as jnp
from jax import lax
from jax.experimental import pallas as pl
from jax.experimental.pallas import tpu as pltpu

F32 = jnp.float32
BF16 = jnp.bfloat16

D_MODEL = 2048
HEAD_DIM = 64
N_META = 16
GRID_W = 64
A_HEADS = 16
A_KV_HEADS = 4
A_WINDOW = 128
A_BLOCK = 128
N_BUCKETS = 32
MAX_DISTANCE = 128
B_HEADS = 16
NA_ROWS = 8
NA_COLS = 16
A_WIDTH = A_HEADS * HEAD_DIM
KV_WIDTH = A_KV_HEADS * HEAD_DIM
B_WIDTH = B_HEADS * HEAD_DIM
MIX_WIDTH = A_WIDTH + B_WIDTH
IN_WIDTH = A_WIDTH + 2 * KV_WIDTH + 3 * B_WIDTH
N_EXPERTS = 64
TOP_K = 8
N_GROUPS = 8
GROUP_SIZE = N_EXPERTS // N_GROUPS
TOPK_GROUPS = 4
D_EXPERT = 512
D_SHARED = 512
ROUTED_SCALE = 2.5
DEPTH = 1
DEEPNORM_ALPHA = (2 * DEPTH) ** 0.25
LN_EPS = 1e-5
RMS_EPS = 1e-6
NEG = -1e30
SCALE = HEAD_DIM ** -0.5

LANES = 128
SUBLANES = 8
VMEM_LIMIT = 56 * 1024 * 1024

COL_QA = 0
COL_KA = A_WIDTH // LANES
COL_VA = (A_WIDTH + KV_WIDTH) // LANES
COL_QB = (A_WIDTH + 2 * KV_WIDTH) // LANES
COL_KB = COL_QB + B_WIDTH // LANES
COL_VB = COL_KB + B_WIDTH // LANES

WIN_BATCH = 8
NA_UNROLL = 8
MOE_BLOCK = 512
DISPATCH_TILE = 128
D_HALF = D_MODEL // 2
U32 = jnp.uint32


def _pack_rows(x):
    hi = lax.bitcast_convert_type(x[:, :D_HALF].astype(jnp.bfloat16).astype(F32), U32)
    lo = lax.bitcast_convert_type(x[:, D_HALF:].astype(jnp.bfloat16).astype(F32), U32)
    return hi | (lo >> 16)


def _unpack_rows(p):
    hi = lax.bitcast_convert_type(p & jnp.uint32(0xFFFF0000), F32).astype(BF16)
    lo = lax.bitcast_convert_type(p << 16, F32).astype(BF16)
    return hi, lo


def _cparams(sem):
    return pltpu.CompilerParams(dimension_semantics=sem, vmem_limit_bytes=VMEM_LIMIT)


def _layer_norm(x, g, b):
    mu = jnp.mean(x, axis=-1, keepdims=True)
    xc = x - mu
    var = jnp.mean(xc * xc, axis=-1, keepdims=True)
    return xc * lax.rsqrt(var + LN_EPS) * g + b


def _dot(a, b):
    return jnp.dot(a, b, preferred_element_type=F32)


def _dot_nt(a, b):
    return lax.dot_general(a, b, (((1,), (1,)), ((), ())), preferred_element_type=F32)


def _silu(g):
    return g / (1.0 + jnp.exp(-g))


def _ln_inproj_kernel(x_ref, g_ref, b_ref, w_ref, o_ref, h_scr):
    @pl.when(pl.program_id(1) == 0)
    def _():
        h_scr[...] = _layer_norm(x_ref[...], g_ref[...], b_ref[...]).astype(BF16)

    o_ref[...] = _dot(h_scr[...], w_ref[...]).astype(o_ref.dtype)


def _ln_inproj(x2, g, b, w_bf16, tm, tn):
    m = x2.shape[0]
    n = w_bf16.shape[1]
    return pl.pallas_call(
        _ln_inproj_kernel,
        grid=(m // tm, n // tn),
        in_specs=[
            pl.BlockSpec((tm, D_MODEL), lambda i, j: (i, 0)),
            pl.BlockSpec((1, D_MODEL), lambda i, j: (0, 0)),
            pl.BlockSpec((1, D_MODEL), lambda i, j: (0, 0)),
            pl.BlockSpec((D_MODEL, tn), lambda i, j: (0, j)),
        ],
        out_specs=pl.BlockSpec((tm, tn), lambda i, j: (i, j)),
        out_shape=jax.ShapeDtypeStruct((m, n), BF16),
        scratch_shapes=[pltpu.VMEM((tm, D_MODEL), BF16)],
        compiler_params=_cparams(("parallel", "arbitrary")),
        name="ln_inproj",
    )(x2, g, b, w_bf16)


def _t5_bucket(rel):
    nb = N_BUCKETS // 2
    max_exact = nb // 2
    ret = jnp.where(rel > 0, nb, 0)
    n = jnp.abs(rel)
    nf = jnp.maximum(n, 1).astype(F32)
    large = max_exact + (jnp.log(nf / max_exact) / math.log(MAX_DISTANCE / max_exact)
                         * (nb - max_exact)).astype(jnp.int32)
    large = jnp.minimum(large, nb - 1)
    return ret + jnp.where(n < max_exact, n, large)


def _lookup(table_t, idx, n):
    onehot = (idx[None] == jnp.arange(n, dtype=jnp.int32).reshape((n,) + (1,) * idx.ndim))
    return jnp.einsum('hb,b...->h...', table_t, onehot.astype(F32),
                      precision=lax.Precision.HIGHEST)


def _window_bias(t5_table, a_sink, seq):
    assert N_META + A_BLOCK - (N_META - 1) > MAX_DISTANCE
    nblk = seq // A_BLOCK
    t5_t = t5_table.astype(F32).T
    q_loc = jnp.arange(A_BLOCK, dtype=jnp.int32)
    k_loc = jnp.arange(3 * A_BLOCK, dtype=jnp.int32) - A_BLOCK
    rel = k_loc[None, :] - q_loc[:, None]
    band = _lookup(t5_t, _t5_bucket(rel), N_BUCKETS)
    win = jnp.abs(rel) <= A_WINDOW
    sink = jnp.broadcast_to(a_sink.astype(F32)[:, None, None], (A_HEADS, A_BLOCK, 1))
    pad = jnp.full((A_HEADS, A_BLOCK, LANES - N_META - 1), NEG, F32)
    out = []
    for blk in (0, min(1, nblk - 1), nblk - 1):
        gk = blk * A_BLOCK + k_loc
        valid = win & (gk >= 0)[None, :] & (gk < seq)[None, :]
        band_v = jnp.where(valid[None], band, NEG)
        q_pos = N_META + blk * A_BLOCK + q_loc
        rel_m = jnp.arange(N_META, dtype=jnp.int32)[None, :] - q_pos[:, None]
        bias_m = _lookup(t5_t, _t5_bucket(rel_m), N_BUCKETS)
        out.append(jnp.concatenate([bias_m, sink, pad, band_v], axis=-1))
    return jnp.stack(out, axis=0)


def _window_kernel(q_ref, kp_ref, kc_ref, kn_ref, vp_ref, vc_ref, vn_ref, km_ref, vm_ref,
                   bias_ref, o_ref):
    kall = jnp.concatenate([km_ref[...], kp_ref[0], kc_ref[0], kn_ref[0]], axis=0)
    vall = jnp.concatenate([vm_ref[...], vp_ref[0], vc_ref[0], vn_ref[0]], axis=0)
    lane = lax.broadcasted_iota(jnp.int32, (A_BLOCK, LANES), 1)
    in_lo = lane < HEAD_DIM
    group = A_HEADS // A_KV_HEADS
    n_heads = 2 * group
    pair_out = []
    for h0 in range(0, n_heads, WIN_BATCH):
        scores = []
        for hl in range(h0, h0 + WIN_BATCH):
            p, half = hl // 2, hl % 2
            kv_half = hl // group
            qp = q_ref[0, :, p * LANES:(p + 1) * LANES].astype(F32) * SCALE
            src = qp if half == kv_half else pltpu.roll(qp, HEAD_DIM, axis=1)
            keep = in_lo if kv_half == 0 else jnp.logical_not(in_lo)
            qm = jnp.where(keep, src, 0.0).astype(BF16)
            scores.append(_dot_nt(qm, kall))
        probs = []
        for j, s in enumerate(scores):
            s = s + bias_ref[0, h0 + j]
            m = jnp.max(s, axis=-1, keepdims=True)
            e = jnp.exp(s - m)
            probs.append((e.astype(BF16), jnp.sum(e, axis=-1, keepdims=True)))
        res = []
        for j, (e, l) in enumerate(probs):
            hl = h0 + j
            o = _dot(e, vall) / l
            if hl % 2 != hl // group:
                o = pltpu.roll(o, HEAD_DIM, axis=1)
            res.append(o)
        for j in range(0, WIN_BATCH, 2):
            pair_out.append(jnp.where(in_lo, res[j], res[j + 1]))
    o_ref[0] = jnp.concatenate(pair_out, axis=1).astype(o_ref.dtype)


def _window_attention(u, kmeta, vmeta, bias, bsz, seq):
    nblk = seq // A_BLOCK
    n_pairs = A_KV_HEADS // 2
    qw = A_WIDTH // n_pairs

    def variant(n):
        return jnp.where(n == 0, 0, jnp.where(n == nblk - 1, 2, 1))

    def kv_spec(col0, shift):
        return pl.BlockSpec(
            (1, A_BLOCK, LANES),
            lambda b, j, n: (b, jnp.clip(n + shift, 0, nblk - 1), col0 + j))

    return pl.pallas_call(
        _window_kernel,
        grid=(bsz, n_pairs, nblk),
        in_specs=[
            pl.BlockSpec((1, A_BLOCK, qw), lambda b, j, n: (b, n, j)),
            kv_spec(COL_KA, -1), kv_spec(COL_KA, 0), kv_spec(COL_KA, 1),
            kv_spec(COL_VA, -1), kv_spec(COL_VA, 0), kv_spec(COL_VA, 1),
            pl.BlockSpec((LANES, LANES), lambda b, j, n: (0, j)),
            pl.BlockSpec((LANES, LANES), lambda b, j, n: (0, j)),
            pl.BlockSpec((1, A_HEADS // n_pairs, A_BLOCK, 4 * LANES),
                         lambda b, j, n: (variant(n), j, 0, 0)),
        ],
        out_specs=pl.BlockSpec((1, A_BLOCK, qw), lambda b, j, n: (b, n, j)),
        out_shape=jax.ShapeDtypeStruct((bsz, seq, A_WIDTH), BF16),
        compiler_params=_cparams(("parallel", "parallel", "arbitrary")),
        name="window_attention",
    )(u, u, u, u, u, u, u, kmeta, vmeta, bias)


def _na_bias(rpb):
    qc = jnp.arange(GRID_W, dtype=jnp.int32)
    kc = jnp.arange(GRID_W, dtype=jnp.int32)
    cs = jnp.clip(qc - NA_COLS // 2, 0, GRID_W - NA_COLS)
    cmask = (kc[None, :] >= cs[:, None]) & (kc[None, :] < cs[:, None] + NA_COLS)
    dc = jnp.clip(kc[None, :] - qc[:, None] + NA_COLS - 1, 0, 2 * NA_COLS - 2)
    n_dc = 2 * NA_COLS - 1
    onehot = (dc[None] == jnp.arange(n_dc, dtype=jnp.int32)[:, None, None]).astype(F32)
    t = jnp.einsum('hrd,dqk->hrqk', rpb.astype(F32), onehot, precision=lax.Precision.HIGHEST)
    t = jnp.where(cmask[None, None], t, NEG)
    variants = []
    for oi in range(NA_ROWS):
        variants.append(jnp.concatenate(
            [t[:, i - oi + NA_ROWS - 1] for i in range(NA_ROWS)], axis=-1))
    return jnp.stack(variants, axis=1)


def _na_kernel(q_ref, k_ref, v_ref, km_ref, vm_ref, bias_ref, o_ref, *, rows):
    span = NA_ROWS * GRID_W
    lane = lax.broadcasted_iota(jnp.int32, (GRID_W, LANES), 1)
    in_lo = lane < HEAD_DIM
    meta_bias = jnp.where(lane < N_META, 0.0, NEG).astype(F32)
    km = km_ref[...]
    vm = vm_ref[...]

    def body(it, carry):
        work = []
        for j in range(NA_UNROLL):
            r = it * NA_UNROLL + j
            rs = jnp.clip(r - NA_ROWS // 2, 0, rows - NA_ROWS)
            oi = r - rs
            q0 = pl.multiple_of(r * GRID_W, GRID_W)
            k0 = pl.multiple_of(rs * GRID_W, GRID_W)
            q = q_ref[0, pl.ds(q0, GRID_W), :]
            ks = k_ref[0, pl.ds(k0, span), :]
            for hl in range(2):
                keep = in_lo if hl == 0 else jnp.logical_not(in_lo)
                qm = jnp.where(keep, q, jnp.zeros_like(q))
                work.append((_dot_nt(qm, ks), _dot_nt(qm, km), hl, oi, k0, q0))
        probs = []
        for sw, sm, hl, oi, k0, q0 in work:
            sw = sw * SCALE + bias_ref[hl, oi]
            sm = sm * SCALE + meta_bias
            m = jnp.maximum(jnp.max(sw, axis=-1, keepdims=True),
                            jnp.max(sm, axis=-1, keepdims=True))
            ew = jnp.exp(sw - m)
            em = jnp.exp(sm - m)
            l = jnp.sum(ew, axis=-1, keepdims=True) + jnp.sum(em, axis=-1, keepdims=True)
            probs.append((ew.astype(BF16), em.astype(BF16), l, k0, q0))
        res = []
        for ew, em, l, k0, q0 in probs:
            vs = v_ref[0, pl.ds(k0, span), :]
            res.append((_dot(ew, vs) + _dot(em, vm)) / l)
        for j in range(NA_UNROLL):
            q0 = probs[2 * j][4]
            out = jnp.where(in_lo, res[2 * j], res[2 * j + 1])
            o_ref[0, pl.ds(q0, GRID_W), :] = out.astype(o_ref.dtype)
        return carry

    lax.fori_loop(0, rows // NA_UNROLL, body, 0)


def _neighbourhood_attention(u, kmeta, vmeta, bias, bsz, seq):
    rows = seq // GRID_W
    assert rows >= NA_ROWS and rows % NA_UNROLL == 0
    n_pairs = B_HEADS // 2

    def col_spec(col0):
        return pl.BlockSpec((1, seq, LANES), lambda b, j: (b, 0, col0 + j))

    return pl.pallas_call(
        functools.partial(_na_kernel, rows=rows),
        grid=(bsz, n_pairs),
        in_specs=[
            col_spec(COL_QB), col_spec(COL_KB), col_spec(COL_VB),
            pl.BlockSpec((LANES, LANES), lambda b, j: (0, j)),
            pl.BlockSpec((LANES, LANES), lambda b, j: (0, j)),
            pl.BlockSpec((2, NA_ROWS, GRID_W, NA_ROWS * GRID_W), lambda b, j: (j, 0, 0, 0)),
        ],
        out_specs=pl.BlockSpec((1, seq, LANES), lambda b, j: (b, 0, j)),
        out_shape=jax.ShapeDtypeStruct((bsz, seq, B_WIDTH), BF16),
        compiler_params=_cparams(("parallel", "parallel")),
        name="neighbourhood_attention",
    )(u, u, u, kmeta, vmeta, bias)


def _route(scores, rbias):
    t = scores.shape[-1]
    ninf = -jnp.inf
    biased = scores + rbias
    b3 = biased.reshape(N_GROUPS, GROUP_SIZE, t)
    s3 = scores.reshape(N_GROUPS, GROUP_SIZE, t)
    io_in = lax.broadcasted_iota(jnp.int32, b3.shape, 1)
    io_g3 = lax.broadcasted_iota(jnp.int32, b3.shape, 0)
    io_e = io_g3 * GROUP_SIZE + io_in
    m1 = jnp.max(b3, axis=1, keepdims=True)
    i1 = jnp.min(jnp.where(b3 == m1, io_in, GROUP_SIZE), axis=1, keepdims=True)
    m2 = jnp.max(jnp.where(io_in == i1, ninf, b3), axis=1, keepdims=True)
    gs = m1 + m2
    io_g = lax.broadcasted_iota(jnp.int32, gs.shape, 0)
    gmask = jnp.zeros(gs.shape, jnp.bool_)
    cur = gs
    for _ in range(TOPK_GROUPS):
        m = jnp.max(cur, axis=0, keepdims=True)
        i = jnp.min(jnp.where(cur == m, io_g, N_GROUPS), axis=0, keepdims=True)
        pick = io_g == i
        gmask = jnp.logical_or(gmask, pick)
        cur = jnp.where(pick, ninf, cur)
    cur = jnp.where(gmask, b3, ninf)
    picks, top_e, top_s = [], [], []
    for _ in range(TOP_K):
        m = jnp.max(jnp.max(cur, axis=1, keepdims=True), axis=0, keepdims=True)
        i = jnp.min(jnp.min(jnp.where(cur == m, io_e, N_EXPERTS), axis=1, keepdims=True),
                    axis=0, keepdims=True)
        pick = io_e == i
        picks.append(pick)
        top_e.append(i.reshape(1, t))
        w = jnp.sum(jnp.sum(jnp.where(pick, s3, 0.0), axis=1, keepdims=True), axis=0, keepdims=True)
        top_s.append(w.reshape(1, t))
        cur = jnp.where(pick, ninf, cur)
    top_e = jnp.concatenate(top_e, axis=0)
    top_s = jnp.concatenate(top_s, axis=0)
    denom = top_s[0:1]
    for k in range(1, TOP_K):
        denom = denom + top_s[k:k + 1]
    top_w = top_s / (denom + 1e-20) * ROUTED_SCALE
    sel = picks[0]
    for k in range(1, TOP_K):
        sel = jnp.logical_or(sel, picks[k])
    sel = jnp.where(sel, 1.0, 0.0).astype(F32).reshape(N_EXPERTS, t)
    return top_e, top_w, sel, picks


def _outproj_router_kernel(ya_ref, yb_ref, x_ref, wo_ref, ga_ref, gb_ref, eg_ref, eb_ref,
                           g1_ref, b1_ref, wrh_ref, wrl_ref, rb_ref,
                           h1_ref, te_ref, tw_ref, tr_ref, cnt_ref, carry):
    tm = x_ref.shape[0]

    @pl.when(pl.program_id(0) == 0)
    def _():
        carry[...] = jnp.zeros_like(carry)

    def rms(y_ref, g_ref):
        y = y_ref[...].astype(F32)
        inv = lax.rsqrt(jnp.mean(y * y, axis=-1, keepdims=True) + RMS_EPS)
        return (y * inv * g_ref[...]).astype(BF16)

    mix = _dot(jnp.concatenate([rms(ya_ref, ga_ref), rms(yb_ref, gb_ref)], axis=1), wo_ref[...])
    h = _layer_norm(x_ref[...], eg_ref[...], eb_ref[...])
    h1 = _layer_norm(DEEPNORM_ALPHA * h + mix, g1_ref[...], b1_ref[...])
    h1_ref[...] = h1
    hb = h1.astype(BF16)

    hlo = (h1 - hb.astype(F32)).astype(BF16)
    logits = (_dot_nt(wrh_ref[...], hb) + _dot_nt(wrh_ref[...], hlo)
              + _dot_nt(wrl_ref[...], hb))
    scores = 1.0 / (1.0 + jnp.exp(-logits))
    top_e, top_w, sel, picks = _route(scores, rb_ref[...])

    row = lax.broadcasted_iota(jnp.int32, (tm, tm), 0)
    col = lax.broadcasted_iota(jnp.int32, (tm, tm), 1)
    before = jnp.where(row < col, 1.0, 0.0).astype(BF16)
    rank = _dot(sel.astype(BF16), before) + carry[...]
    rank3 = rank.reshape(N_GROUPS, GROUP_SIZE, tm)
    ranks = []
    for k in range(TOP_K):
        rk = jnp.sum(jnp.sum(jnp.where(picks[k], rank3, 0.0), axis=1, keepdims=True),
                     axis=0, keepdims=True)
        ranks.append(rk.reshape(1, tm))
    te_ref[...] = top_e
    tw_ref[...] = top_w
    tr_ref[...] = jnp.concatenate(ranks, axis=0).astype(jnp.int32)
    carry[...] = carry[...] + jnp.sum(sel, axis=-1, keepdims=True)
    cnt_ref[...] = jnp.broadcast_to(carry[...], cnt_ref.shape)


def _outproj_router(ya, yb, x2, wo_bf16, ga, gb, eg, eb, g1, b1, wrh, wrl, rb, tm):
    m = x2.shape[0]
    row = lambda w: pl.BlockSpec((tm, w), lambda i: (i, 0))
    full = lambda a: pl.BlockSpec(a.shape, lambda i: (0,) * a.ndim)
    tok = pl.BlockSpec((TOP_K, tm), lambda i: (0, i))
    return pl.pallas_call(
        _outproj_router_kernel,
        grid=(m // tm,),
        in_specs=[row(A_WIDTH), row(B_WIDTH), row(D_MODEL), full(wo_bf16), full(ga), full(gb),
                  full(eg), full(eb), full(g1), full(b1), full(wrh), full(wrl), full(rb)],
        out_specs=[row(D_MODEL), tok, tok, tok,
                   pl.BlockSpec((N_EXPERTS, LANES), lambda i: (0, 0))],
        out_shape=[jax.ShapeDtypeStruct((m, D_MODEL), F32),
                   jax.ShapeDtypeStruct((TOP_K, m), jnp.int32),
                   jax.ShapeDtypeStruct((TOP_K, m), F32),
                   jax.ShapeDtypeStruct((TOP_K, m), jnp.int32),
                   jax.ShapeDtypeStruct((N_EXPERTS, LANES), F32)],
        scratch_shapes=[pltpu.VMEM((N_EXPERTS, 1), F32)],
        compiler_params=_cparams(("arbitrary",)),
        name="outproj_router",
    )(ya, yb, x2, wo_bf16, ga, gb, eg, eb, g1, b1, wrh, wrl, rb)


def _zero_fill_padding(pad_start_ref, pad_len_ref, n_used, xs_hbm, zeros, sem):
    n_blocks = xs_hbm.shape[0] // MOE_BLOCK
    zeros[...] = jnp.zeros_like(zeros)

    def copies(act):
        def per_expert(e, carry):
            start = pad_start_ref[e]
            n = pad_len_ref[e]
            head = jnp.minimum((-start) & (SUBLANES - 1), n)
            for j in range(SUBLANES - 1):
                @pl.when(j < head)
                def _(j=j):
                    act(pltpu.make_async_copy(zeros.at[pl.ds(0, 1), :],
                                              xs_hbm.at[pl.ds(start + j, 1), :], sem.at[1]))
            aligned = start + head
            rest = n - head
            bit = MOE_BLOCK // 2
            while bit >= SUBLANES:
                @pl.when((rest & bit) != 0)
                def _(bit=bit):
                    off = pl.multiple_of(aligned + (rest & (-2 * bit)), SUBLANES)
                    act(pltpu.make_async_copy(zeros.at[pl.ds(0, bit), :],
                                              xs_hbm.at[pl.ds(off, bit), :], sem.at[1]))
                bit //= 2
            return carry

        def per_block(j, carry):
            off = pl.multiple_of(j * MOE_BLOCK, MOE_BLOCK)
            act(pltpu.make_async_copy(zeros, xs_hbm.at[pl.ds(off, MOE_BLOCK), :], sem.at[1]))
            return carry

        lax.fori_loop(0, N_EXPERTS, per_expert, 0)
        lax.fori_loop(n_used, n_blocks, per_block, 0)

    copies(lambda cp: cp.start())
    copies(lambda cp: cp.wait())


def _dispatch_kernel(pad_start_ref, pad_len_ref, nu_ref, dest_ref, h1_ref, wsg_ref, wsu_ref,
                     wsd_ref, home0_hbm, xs_hbm, sh_ref, home_hbm, packed, zeros, home, sem):
    i = pl.program_id(0)
    tm = h1_ref.shape[0]
    m = tm * pl.num_programs(0)

    @pl.when(i == 0)
    def _():
        cp = pltpu.make_async_copy(home0_hbm, home, sem.at[1])
        cp.start()
        cp.wait()
        _zero_fill_padding(pad_start_ref, pad_len_ref, nu_ref[0], xs_hbm, zeros, sem)

    h1 = h1_ref[...]
    packed[...] = _pack_rows(h1)
    for t in range(tm):
        for k in range(TOP_K):
            slot = dest_ref[k, t]
            home[slot] = k * m + i * tm + t
            pltpu.make_async_copy(packed.at[pl.ds(t, 1), :],
                                  xs_hbm.at[pl.ds(slot, 1), :], sem.at[0]).start()
    hb = h1.astype(BF16)
    g = _dot(hb, wsg_ref[...])
    u = _dot(hb, wsu_ref[...])
    sh_ref[...] = _dot((_silu(g) * u).astype(BF16), wsd_ref[...]).astype(sh_ref.dtype)
    for k in range(TOP_K):
        pltpu.make_async_copy(packed, xs_hbm.at[pl.ds(0, tm), :], sem.at[0]).wait()

    @pl.when(i == pl.num_programs(0) - 1)
    def _():
        cp = pltpu.make_async_copy(home, home_hbm, sem.at[1])
        cp.start()
        cp.wait()


def _dispatch_shared(pad_start, pad_len, n_used, dest, h1, wsg, wsu, wsd, home0):
    m = h1.shape[0]
    tm = DISPATCH_TILE
    p_rows = home0.shape[0]
    full = lambda a: pl.BlockSpec(a.shape, lambda i, *_: (0,) * a.ndim)
    return pl.pallas_call(
        _dispatch_kernel,
        grid_spec=pltpu.PrefetchScalarGridSpec(
            num_scalar_prefetch=3,
            grid=(m // tm,),
            in_specs=[pl.BlockSpec((TOP_K, tm), lambda i, *_: (0, i), memory_space=pltpu.SMEM),
                      pl.BlockSpec((tm, D_MODEL), lambda i, *_: (i, 0)),
                      full(wsg), full(wsu), full(wsd),
                      pl.BlockSpec(memory_space=pl.ANY)],
            out_specs=[pl.BlockSpec(memory_space=pl.ANY),
                       pl.BlockSpec((tm, D_MODEL), lambda i, *_: (i, 0)),
                       pl.BlockSpec(memory_space=pl.ANY)],
            scratch_shapes=[pltpu.VMEM((tm, D_HALF), U32),
                            pltpu.VMEM((MOE_BLOCK, D_HALF), U32),
                            pltpu.SMEM((p_rows,), jnp.int32),
                            pltpu.SemaphoreType.DMA((2,))],
        ),
        out_shape=[jax.ShapeDtypeStruct((p_rows, D_HALF), U32),
                   jax.ShapeDtypeStruct((m, D_MODEL), BF16),
                   jax.ShapeDtypeStruct((p_rows,), jnp.int32)],
        compiler_params=pltpu.CompilerParams(
            dimension_semantics=("arbitrary",), vmem_limit_bytes=VMEM_LIMIT,
            disable_bounds_checks=True),
        name="dispatch_shared",
    )(pad_start, pad_len, n_used, dest, h1, wsg, wsu, wsd, home0)


def _moe_kernel(be_ref, nu_ref, home_prev_ref, x_ref, wg_ref, wu_ref, wd_ref, y_hbm,
                ybuf0, ybuf1, sem, wg_b, wu_b, wd_b):
    i = pl.program_id(0)
    n_used = nu_ref[0]
    e = be_ref[i]
    prev = be_ref[jnp.maximum(i - 1, 0)]
    even = i % 2 == 0
    bufs = (ybuf0, ybuf1)
    spare0 = y_hbm.shape[0] - 2 * MOE_BLOCK

    @pl.when(i == 0)
    def _():
        ybuf1[...] = jnp.zeros_like(ybuf1)
        for b in range(2):
            cp = pltpu.make_async_copy(
                ybuf1, y_hbm.at[pl.ds(spare0 + b * MOE_BLOCK, MOE_BLOCK), :], sem.at[1])
            cp.start()
            cp.wait()

    @pl.when(jnp.logical_and(i < n_used, jnp.logical_or(i == 0, e != prev)))
    def _():
        wg_b[...] = wg_ref[0].astype(BF16)
        wu_b[...] = wu_ref[0].astype(BF16)
        wd_b[...] = wd_ref[0].astype(BF16)

    def scatter_rows(b):
        for r in range(MOE_BLOCK):
            pltpu.make_async_copy(bufs[b].at[pl.ds(r, 1), :],
                                  y_hbm.at[pl.ds(home_prev_ref[0, 0, r], 1), :],
                                  sem.at[b]).start()

    def wait_rows(b):
        pltpu.make_async_copy(bufs[b], y_hbm.at[pl.ds(0, MOE_BLOCK), :], sem.at[b]).wait()

    def compute(b):
        hi, lo = _unpack_rows(x_ref[...])
        g = _dot(hi, wg_b[0:D_HALF, :]) + _dot(lo, wg_b[D_HALF:D_MODEL, :])
        u = _dot(hi, wu_b[0:D_HALF, :]) + _dot(lo, wu_b[D_HALF:D_MODEL, :])
        bufs[b][...] = _pack_rows(_dot((_silu(g) * u).astype(BF16), wd_b[...]))

    for b in range(2):
        mine = even if b == 0 else jnp.logical_not(even)

        @pl.when(mine & (i >= 2) & (i <= n_used + 1))
        def _():
            wait_rows(b)

        @pl.when(mine & (i >= 1) & (i < n_used))
        def _():
            scatter_rows(1 - b)
            compute(b)

        @pl.when(mine & (i >= 1) & (i == n_used))
        def _():
            scatter_rows(1 - b)

    @pl.when((i == 0) & (n_used > 0))
    def _():
        compute(0)


def _moe_experts(block_e, n_used, home, x_sorted, w_gate, w_up, w_down, m):
    n_blocks = x_sorted.shape[0] // MOE_BLOCK
    assert block_e.shape[0] == n_blocks + 2
    used = lambda i, be, nu: (jnp.minimum(i, nu[0] - 1), 0)
    weights = lambda shape: pl.BlockSpec(
        (1,) + shape, lambda i, be, nu: (be[jnp.minimum(i, nu[0] - 1)], 0, 0))
    return pl.pallas_call(
        _moe_kernel,
        grid_spec=pltpu.PrefetchScalarGridSpec(
            num_scalar_prefetch=2,
            grid=(n_blocks + 2,),
            in_specs=[
                pl.BlockSpec((1, 1, MOE_BLOCK),
                             lambda i, be, nu: (jnp.clip(i - 1, 0, n_blocks - 1), 0, 0),
                             memory_space=pltpu.SMEM),
                pl.BlockSpec((MOE_BLOCK, D_HALF), used),
                weights((D_MODEL, D_EXPERT)), weights((D_MODEL, D_EXPERT)),
                weights((D_EXPERT, D_MODEL)),
            ],
            out_specs=pl.BlockSpec(memory_space=pl.ANY),
            scratch_shapes=[pltpu.VMEM((MOE_BLOCK, D_HALF), U32),
                            pltpu.VMEM((MOE_BLOCK, D_HALF), U32),
                            pltpu.SemaphoreType.DMA((2,)),
                            pltpu.VMEM((D_MODEL, D_EXPERT), BF16),
                            pltpu.VMEM((D_MODEL, D_EXPERT), BF16),
                            pltpu.VMEM((D_EXPERT, D_MODEL), BF16)],
        ),
        out_shape=jax.ShapeDtypeStruct((TOP_K * m + 2 * MOE_BLOCK, D_HALF), U32),
        compiler_params=pltpu.CompilerParams(
            dimension_semantics=("arbitrary",), vmem_limit_bytes=VMEM_LIMIT,
            disable_bounds_checks=True),
        name="moe_experts",
    )(block_e, n_used, home.reshape(n_blocks, 1, MOE_BLOCK), x_sorted, w_gate, w_up, w_down)


def _final_kernel(h1_ref, sh_ref, *rest):
    ys_refs = rest[:TOP_K]
    tw_ref, g2_ref, b2_ref, o_ref = rest[TOP_K:]
    sh = sh_ref[...].astype(F32)
    left, right = sh[:, :D_HALF], sh[:, D_HALF:]
    for k in range(TOP_K):
        p = ys_refs[k][...]
        w = tw_ref[:, k:k + 1]
        left = left + lax.bitcast_convert_type(p & jnp.uint32(0xFFFF0000), F32) * w
        right = right + lax.bitcast_convert_type(p << 16, F32) * w
    ffn = jnp.concatenate([left, right], axis=1)
    o_ref[...] = _layer_norm(DEEPNORM_ALPHA * h1_ref[...] + ffn, g2_ref[...], b2_ref[...])


def _final(h1, sh, y_home, tw_t, g2, b2, tm):
    m = h1.shape[0]
    row = lambda w: pl.BlockSpec((tm, w), lambda i: (i, 0))
    full = lambda a: pl.BlockSpec(a.shape, lambda i: (0,) * a.ndim)
    choice = lambda k: pl.BlockSpec((tm, D_HALF), lambda i: (k * (m // tm) + i, 0))
    return pl.pallas_call(
        _final_kernel,
        grid=(m // tm,),
        in_specs=[row(D_MODEL), row(D_MODEL)] + [choice(k) for k in range(TOP_K)]
                 + [row(TOP_K), full(g2), full(b2)],
        out_specs=row(D_MODEL),
        out_shape=jax.ShapeDtypeStruct((m, D_MODEL), F32),
        compiler_params=_cparams(("parallel",)),
        name="combine_ln2",
    )(h1, sh, *([y_home] * TOP_K), tw_t, g2, b2)


def _pad_rows(a, rows):
    return jnp.concatenate([a, jnp.zeros((rows - a.shape[0],) + a.shape[1:], a.dtype)], axis=0)


def kernel(x, meta_tokens, ln_emb_g, ln_emb_b, t5_table, w_in, a_sink, na_rpb, g_norm_a, g_norm_b, w_out, ln1_g, ln1_b, w_router, router_bias, w_gate, w_up, w_down, ws_gate, ws_up, ws_down, ln2_g, ln2_b):
    bsz, seq, _ = x.shape
    m = bsz * seq
    r2 = lambda a: a.reshape(1, -1).astype(F32)
    x2 = x.reshape(m, D_MODEL)
    eg, eb = r2(ln_emb_g), r2(ln_emb_b)

    w_in_b = w_in[0].astype(BF16)
    tm = 1024 if m % 1024 == 0 else 128
    u = _ln_inproj(x2, eg, eb, w_in_b, tm, 1152).reshape(bsz, seq, IN_WIDTH)
    um = _ln_inproj(meta_tokens.astype(F32), eg, eb, w_in_b, N_META, 1152)
    cut = lambda c0, width: _pad_rows(um[:, c0 * LANES:c0 * LANES + width], LANES)

    ya = _window_attention(u, cut(COL_KA, KV_WIDTH), cut(COL_VA, KV_WIDTH),
                           _window_bias(t5_table, a_sink[0], seq), bsz, seq)
    yb = _neighbourhood_attention(u, cut(COL_KB, B_WIDTH), cut(COL_VB, B_WIDTH),
                                  _na_bias(na_rpb[0]), bsz, seq)

    wr_t = w_router[0].astype(F32).T
    wr_hi = wr_t.astype(BF16)
    wr_lo = (wr_t - wr_hi.astype(F32)).astype(BF16)
    tm4 = 256 if m % 256 == 0 else 128
    h1, top_e, top_w, top_r, cnt = _outproj_router(
        ya.reshape(m, A_WIDTH), yb.reshape(m, B_WIDTH), x2, w_out[0].astype(BF16),
        r2(g_norm_a), r2(g_norm_b), eg, eb, r2(ln1_g), r2(ln1_b), wr_hi, wr_lo,
        router_bias[0].astype(F32).reshape(N_EXPERTS, 1), tm4)

    counts = cnt[:, 0].astype(jnp.int32)
    nb_e = (counts + MOE_BLOCK - 1) // MOE_BLOCK
    bend = jnp.cumsum(nb_e)
    pstart = (bend - nb_e) * MOE_BLOCK
    n_blocks = (m * TOP_K) // MOE_BLOCK + N_EXPERTS
    p_rows = n_blocks * MOE_BLOCK
    blk = jnp.arange(n_blocks + 2, dtype=jnp.int32)
    block_e = jnp.minimum(jnp.sum((bend[None, :] <= blk[:, None]).astype(jnp.int32), axis=1),
                          N_EXPERTS - 1)
    n_used = bend[-1:].astype(jnp.int32)
    expert_ids = jnp.arange(N_EXPERTS, dtype=jnp.int32)[:, None, None]
    dest = top_r + jnp.sum(jnp.where(top_e[None] == expert_ids, pstart[:, None, None], 0),
                           axis=0)
    slot = jnp.arange(p_rows, dtype=jnp.int32)
    home0 = TOP_K * m + (slot // MOE_BLOCK % 2) * MOE_BLOCK + slot % MOE_BLOCK

    x_sorted, shared, home = _dispatch_shared(
        pstart + counts, nb_e * MOE_BLOCK - counts, n_used, dest, h1,
        ws_gate[0].astype(BF16), ws_up[0].astype(BF16), ws_down[0].astype(BF16), home0)
    y_home = _moe_experts(block_e, n_used, home, x_sorted, w_gate[0], w_up[0], w_down[0], m)

    out = _final(h1, shared, y_home, top_w.T, r2(ln2_g), r2(ln2_b), tm4)
    return out.reshape(bsz, seq, D_MODEL)
```

```python
import functools
import math

import jax
import jax.numpy as jnp
from jax import lax
from jax.experimental import pallas as pl
from jax.experimental.pallas import tpu as pltpu

F32 = jnp.float32
BF16 = jnp.bfloat16

D_MODEL = 2048
HEAD_DIM = 64
N_META = 16
GRID_W = 64
A_HEADS = 16
A_KV_HEADS = 4
A_WINDOW = 128
A_BLOCK = 128
N_BUCKETS = 32
MAX_DISTANCE = 128
B_HEADS = 16
NA_ROWS = 8
NA_COLS = 16
A_WIDTH = A_HEADS * HEAD_DIM
KV_WIDTH = A_KV_HEADS * HEAD_DIM
B_WIDTH = B_HEADS * HEAD_DIM
MIX_WIDTH = A_WIDTH + B_WIDTH
IN_WIDTH = A_WIDTH + 2 * KV_WIDTH + 3 * B_WIDTH
N_EXPERTS = 64
TOP_K = 8
N_GROUPS = 8
GROUP_SIZE = N_EXPERTS // N_GROUPS
TOPK_GROUPS = 4
D_EXPERT = 512
D_SHARED = 512
ROUTED_SCALE = 2.5
DEPTH = 1
DEEPNORM_ALPHA = (2 * DEPTH) ** 0.25
LN_EPS = 1e-5
RMS_EPS = 1e-6
NEG = -1e30
SCALE = HEAD_DIM ** -0.5

LANES = 128
SUBLANES = 8
VMEM_LIMIT = 56 * 1024 * 1024

COL_QA = 0
COL_KA = A_WIDTH // LANES
COL_VA = (A_WIDTH + KV_WIDTH) // LANES
COL_QB = (A_WIDTH + 2 * KV_WIDTH) // LANES
COL_KB = COL_QB + B_WIDTH // LANES
COL_VB = COL_KB + B_WIDTH // LANES

WIN_BATCH = 8
NA_UNROLL = 8
MOE_BLOCK = 512
DISPATCH_TILE = 128
D_HALF = D_MODEL // 2
U32 = jnp.uint32


def _pack_rows(x):
    hi = lax.bitcast_convert_type(x[:, :D_HALF].astype(jnp.bfloat16).astype(F32), U32)
    lo = lax.bitcast_convert_type(x[:, D_HALF:].astype(jnp.bfloat16).astype(F32), U32)
    return hi | (lo >> 16)


def _unpack_rows(p):
    hi = lax.bitcast_convert_type(p & jnp.uint32(0xFFFF0000), F32).astype(BF16)
    lo = lax.bitcast_convert_type(p << 16, F32).astype(BF16)
    return hi, lo


def _cparams(sem):
    return pltpu.CompilerParams(dimension_semantics=sem, vmem_limit_bytes=VMEM_LIMIT)


def _layer_norm(x, g, b):
    mu = jnp.mean(x, axis=-1, keepdims=True)
    xc = x - mu
    var = jnp.mean(xc * xc, axis=-1, keepdims=True)
    return xc * lax.rsqrt(var + LN_EPS) * g + b


def _dot(a, b):
    return jnp.dot(a, b, preferred_element_type=F32)


def _dot_nt(a, b):
    return lax.dot_general(a, b, (((1,), (1,)), ((), ())), preferred_element_type=F32)


def _silu(g):
    return g / (1.0 + jnp.exp(-g))


def _ln_inproj_kernel(x_ref, g_ref, b_ref, w_ref, o_ref, h_scr):
    @pl.when(pl.program_id(1) == 0)
    def _():
        h_scr[...] = _layer_norm(x_ref[...], g_ref[...], b_ref[...]).astype(BF16)

    o_ref[...] = _dot(h_scr[...], w_ref[...]).astype(o_ref.dtype)


def _ln_inproj(x2, g, b, w_bf16, tm, tn):
    m = x2.shape[0]
    n = w_bf16.shape[1]
    return pl.pallas_call(
        _ln_inproj_kernel,
        grid=(m // tm, n // tn),
        in_specs=[
            pl.BlockSpec((tm, D_MODEL), lambda i, j: (i, 0)),
            pl.BlockSpec((1, D_MODEL), lambda i, j: (0, 0)),
            pl.BlockSpec((1, D_MODEL), lambda i, j: (0, 0)),
            pl.BlockSpec((D_MODEL, tn), lambda i, j: (0, j)),
        ],
        out_specs=pl.BlockSpec((tm, tn), lambda i, j: (i, j)),
        out_shape=jax.ShapeDtypeStruct((m, n), BF16),
        scratch_shapes=[pltpu.VMEM((tm, D_MODEL), BF16)],
        compiler_params=_cparams(("parallel", "arbitrary")),
        name="ln_inproj",
    )(x2, g, b, w_bf16)


def _t5_bucket(rel):
    nb = N_BUCKETS // 2
    max_exact = nb // 2
    ret = jnp.where(rel > 0, nb, 0)
    n = jnp.abs(rel)
    nf = jnp.maximum(n, 1).astype(F32)
    large = max_exact + (jnp.log(nf / max_exact) / math.log(MAX_DISTANCE / max_exact)
                         * (nb - max_exact)).astype(jnp.int32)
    large = jnp.minimum(large, nb - 1)
    return ret + jnp.where(n < max_exact, n, large)


def _lookup(table_t, idx, n):
    onehot = (idx[None] == jnp.arange(n, dtype=jnp.int32).reshape((n,) + (1,) * idx.ndim))
    return jnp.einsum('hb,b...->h...', table_t, onehot.astype(F32),
                      precision=lax.Precision.HIGHEST)


def _window_bias(t5_table, a_sink, seq):
    assert N_META + A_BLOCK - (N_META - 1) > MAX_DISTANCE
    nblk = seq // A_BLOCK
    t5_t = t5_table.astype(F32).T
    q_loc = jnp.arange(A_BLOCK, dtype=jnp.int32)
    k_loc = jnp.arange(3 * A_BLOCK, dtype=jnp.int32) - A_BLOCK
    rel = k_loc[None, :] - q_loc[:, None]
    band = _lookup(t5_t, _t5_bucket(rel), N_BUCKETS)
    win = jnp.abs(rel) <= A_WINDOW
    sink = jnp.broadcast_to(a_sink.astype(F32)[:, None, None], (A_HEADS, A_BLOCK, 1))
    pad = jnp.full((A_HEADS, A_BLOCK, LANES - N_META - 1), NEG, F32)
    out = []
    for blk in (0, min(1, nblk - 1), nblk - 1):
        gk = blk * A_BLOCK + k_loc
        valid = win & (gk >= 0)[None, :] & (gk < seq)[None, :]
        band_v = jnp.where(valid[None], band, NEG)
        q_pos = N_META + blk * A_BLOCK + q_loc
        rel_m = jnp.arange(N_META, dtype=jnp.int32)[None, :] - q_pos[:, None]
        bias_m = _lookup(t5_t, _t5_bucket(rel_m), N_BUCKETS)
        out.append(jnp.concatenate([bias_m, sink, pad, band_v], axis=-1))
    return jnp.stack(out, axis=0)


def _window_kernel(q_ref, kp_ref, kc_ref, kn_ref, vp_ref, vc_ref, vn_ref, km_ref, vm_ref,
                   bias_ref, o_ref):
    kall = jnp.concatenate([km_ref[...], kp_ref[0], kc_ref[0], kn_ref[0]], axis=0)
    vall = jnp.concatenate([vm_ref[...], vp_ref[0], vc_ref[0], vn_ref[0]], axis=0)
    lane = lax.broadcasted_iota(jnp.int32, (A_BLOCK, LANES), 1)
    in_lo = lane < HEAD_DIM
    group = A_HEADS // A_KV_HEADS
    n_heads = 2 * group
    pair_out = []
    for h0 in range(0, n_heads, WIN_BATCH):
        scores = []
        for hl in range(h0, h0 + WIN_BATCH):
            p, half = hl // 2, hl % 2
            kv_half = hl // group
            qp = q_ref[0, :, p * LANES:(p + 1) * LANES].astype(F32) * SCALE
            src = qp if half == kv_half else pltpu.roll(qp, HEAD_DIM, axis=1)
            keep = in_lo if kv_half == 0 else jnp.logical_not(in_lo)
            qm = jnp.where(keep, src, 0.0).astype(BF16)
            scores.append(_dot_nt(qm, kall))
        probs = []
        for j, s in enumerate(scores):
            s = s + bias_ref[0, h0 + j]
            m = jnp.max(s, axis=-1, keepdims=True)
            e = jnp.exp(s - m)
            probs.append((e.astype(BF16), jnp.sum(e, axis=-1, keepdims=True)))
        res = []
        for j, (e, l) in enumerate(probs):
            hl = h0 + j
            o = _dot(e, vall) / l
            if hl % 2 != hl // group:
                o = pltpu.roll(o, HEAD_DIM, axis=1)
            res.append(o)
        for j in range(0, WIN_BATCH, 2):
            pair_out.append(jnp.where(in_lo, res[j], res[j + 1]))
    o_ref[0] = jnp.concatenate(pair_out, axis=1).astype(o_ref.dtype)


def _window_attention(u, kmeta, vmeta, bias, bsz, seq):
    nblk = seq // A_BLOCK
    n_pairs = A_KV_HEADS // 2
    qw = A_WIDTH // n_pairs

    def variant(n):
        return jnp.where(n == 0, 0, jnp.where(n == nblk - 1, 2, 1))

    def kv_spec(col0, shift):
        return pl.BlockSpec(
            (1, A_BLOCK, LANES),
            lambda b, j, n: (b, jnp.clip(n + shift, 0, nblk - 1), col0 + j))

    return pl.pallas_call(
        _window_kernel,
        grid=(bsz, n_pairs, nblk),
        in_specs=[
            pl.BlockSpec((1, A_BLOCK, qw), lambda b, j, n: (b, n, j)),
            kv_spec(COL_KA, -1), kv_spec(COL_KA, 0), kv_spec(COL_KA, 1),
            kv_spec(COL_VA, -1), kv_spec(COL_VA, 0), kv_spec(COL_VA, 1),
            pl.BlockSpec((LANES, LANES), lambda b, j, n: (0, j)),
            pl.BlockSpec((LANES, LANES), lambda b, j, n: (0, j)),
            pl.BlockSpec((1, A_HEADS // n_pairs, A_BLOCK, 4 * LANES),
                         lambda b, j, n: (variant(n), j, 0, 0)),
        ],
        out_specs=pl.BlockSpec((1, A_BLOCK, qw), lambda b, j, n: (b, n, j)),
        out_shape=jax.ShapeDtypeStruct((bsz, seq, A_WIDTH), BF16),
        compiler_params=_cparams(("parallel", "parallel", "arbitrary")),
        name="window_attention",
    )(u, u, u, u, u, u, u, kmeta, vmeta, bias)


def _na_bias(rpb):
    qc = jnp.arange(GRID_W, dtype=jnp.int32)
    kc = jnp.arange(GRID_W, dtype=jnp.int32)
    cs = jnp.clip(qc - NA_COLS // 2, 0, GRID_W - NA_COLS)
    cmask = (kc[None, :] >= cs[:, None]) & (kc[None, :] < cs[:, None] + NA_COLS)
    dc = jnp.clip(kc[None, :] - qc[:, None] + NA_COLS - 1, 0, 2 * NA_COLS - 2)
    n_dc = 2 * NA_COLS - 1
    onehot = (dc[None] == jnp.arange(n_dc, dtype=jnp.int32)[:, None, None]).astype(F32)
    t = jnp.einsum('hrd,dqk->hrqk', rpb.astype(F32), onehot, precision=lax.Precision.HIGHEST)
    t = jnp.where(cmask[None, None], t, NEG)
    variants = []
    for oi in range(NA_ROWS):
        variants.append(jnp.concatenate(
            [t[:, i - oi + NA_ROWS - 1] for i in range(NA_ROWS)], axis=-1))
    return jnp.stack(variants, axis=1)


def _na_kernel(q_ref, k_ref, v_ref, km_ref, vm_ref, bias_ref, o_ref, *, rows):
    span = NA_ROWS * GRID_W
    lane = lax.broadcasted_iota(jnp.int32, (GRID_W, LANES), 1)
    in_lo = lane < HEAD_DIM
    meta_bias = jnp.where(lane < N_META, 0.0, NEG).astype(F32)
    km = km_ref[...]
    vm = vm_ref[...]

    def body(it, carry):
        work = []
        for j in range(NA_UNROLL):
            r = it * NA_UNROLL + j
            rs = jnp.clip(r - NA_ROWS // 2, 0, rows - NA_ROWS)
            oi = r - rs
            q0 = pl.multiple_of(r * GRID_W, GRID_W)
            k0 = pl.multiple_of(rs * GRID_W, GRID_W)
            q = q_ref[0, pl.ds(q0, GRID_W), :]
            ks = k_ref[0, pl.ds(k0, span), :]
            for hl in range(2):
                keep = in_lo if hl == 0 else jnp.logical_not(in_lo)
                qm = jnp.where(keep, q, jnp.zeros_like(q))
                work.append((_dot_nt(qm, ks), _dot_nt(qm, km), hl, oi, k0, q0))
        probs = []
        for sw, sm, hl, oi, k0, q0 in work:
            sw = sw * SCALE + bias_ref[hl, oi]
            sm = sm * SCALE + meta_bias
            m = jnp.maximum(jnp.max(sw, axis=-1, keepdims=True),
                            jnp.max(sm, axis=-1, keepdims=True))
            ew = jnp.exp(sw - m)
            em = jnp.exp(sm - m)
            l = jnp.sum(ew, axis=-1, keepdims=True) + jnp.sum(em, axis=-1, keepdims=True)
            probs.append((ew.astype(BF16), em.astype(BF16), l, k0, q0))
        res = []
        for ew, em, l, k0, q0 in probs:
            vs = v_ref[0, pl.ds(k0, span), :]
            res.append((_dot(ew, vs) + _dot(em, vm)) / l)
        for j in range(NA_UNROLL):
            q0 = probs[2 * j][4]
            out = jnp.where(in_lo, res[2 * j], res[2 * j + 1])
            o_ref[0, pl.ds(q0, GRID_W), :] = out.astype(o_ref.dtype)
        return carry

    lax.fori_loop(0, rows // NA_UNROLL, body, 0)


def _neighbourhood_attention(u, kmeta, vmeta, bias, bsz, seq):
    rows = seq // GRID_W
    assert rows >= NA_ROWS and rows % NA_UNROLL == 0
    n_pairs = B_HEADS // 2

    def col_spec(col0):
        return pl.BlockSpec((1, seq, LANES), lambda b, j: (b, 0, col0 + j))

    return pl.pallas_call(
        functools.partial(_na_kernel, rows=rows),
        grid=(bsz, n_pairs),
        in_specs=[
            col_spec(COL_QB), col_spec(COL_KB), col_spec(COL_VB),
            pl.BlockSpec((LANES, LANES), lambda b, j: (0, j)),
            pl.BlockSpec((LANES, LANES), lambda b, j: (0, j)),
            pl.BlockSpec((2, NA_ROWS, GRID_W, NA_ROWS * GRID_W), lambda b, j: (j, 0, 0, 0)),
        ],
        out_specs=pl.BlockSpec((1, seq, LANES), lambda b, j: (b, 0, j)),
        out_shape=jax.ShapeDtypeStruct((bsz, seq, B_WIDTH), BF16),
        compiler_params=_cparams(("parallel", "parallel")),
        name="neighbourhood_attention",
    )(u, u, u, kmeta, vmeta, bias)


def _route(scores, rbias):
    t = scores.shape[-1]
    ninf = -jnp.inf
    biased = scores + rbias
    b3 = biased.reshape(N_GROUPS, GROUP_SIZE, t)
    s3 = scores.reshape(N_GROUPS, GROUP_SIZE, t)
    io_in = lax.broadcasted_iota(jnp.int32, b3.shape, 1)
    io_g3 = lax.broadcasted_iota(jnp.int32, b3.shape, 0)
    io_e = io_g3 * GROUP_SIZE + io_in
    m1 = jnp.max(b3, axis=1, keepdims=True)
    i1 = jnp.min(jnp.where(b3 == m1, io_in, GROUP_SIZE), axis=1, keepdims=True)
    m2 = jnp.max(jnp.where(io_in == i1, ninf, b3), axis=1, keepdims=True)
    gs = m1 + m2
    io_g = lax.broadcasted_iota(jnp.int32, gs.shape, 0)
    gmask = jnp.zeros(gs.shape, jnp.bool_)
    cur = gs
    for _ in range(TOPK_GROUPS):
        m = jnp.max(cur, axis=0, keepdims=True)
        i = jnp.min(jnp.where(cur == m, io_g, N_GROUPS), axis=0, keepdims=True)
        pick = io_g == i
        gmask = jnp.logical_or(gmask, pick)
        cur = jnp.where(pick, ninf, cur)
    cur = jnp.where(gmask, b3, ninf)
    picks, top_e, top_s = [], [], []
    for _ in range(TOP_K):
        m = jnp.max(jnp.max(cur, axis=1, keepdims=True), axis=0, keepdims=True)
        i = jnp.min(jnp.min(jnp.where(cur == m, io_e, N_EXPERTS), axis=1, keepdims=True),
                    axis=0, keepdims=True)
        pick = io_e == i
        picks.append(pick)
        top_e.append(i.reshape(1, t))
        w = jnp.sum(jnp.sum(jnp.where(pick, s3, 0.0), axis=1, keepdims=True), axis=0, keepdims=True)
        top_s.append(w.reshape(1, t))
        cur = jnp.where(pick, ninf, cur)
    top_e = jnp.concatenate(top_e, axis=0)
    top_s = jnp.concatenate(top_s, axis=0)
    denom = top_s[0:1]
    for k in range(1, TOP_K):
        denom = denom + top_s[k:k + 1]
    top_w = top_s / (denom + 1e-20) * ROUTED_SCALE
    sel = picks[0]
    for k in range(1, TOP_K):
        sel = jnp.logical_or(sel, picks[k])
    sel = jnp.where(sel, 1.0, 0.0).astype(F32).reshape(N_EXPERTS, t)
    return top_e, top_w, sel, picks


def _outproj_router_kernel(ya_ref, yb_ref, x_ref, wo_ref, ga_ref, gb_ref, eg_ref, eb_ref,
                           g1_ref, b1_ref, wrh_ref, wrl_ref, rb_ref,
                           h1_ref, te_ref, tw_ref, tr_ref, cnt_ref, carry):
    tm = x_ref.shape[0]

    @pl.when(pl.program_id(0) == 0)
    def _():
        carry[...] = jnp.zeros_like(carry)

    def rms(y_ref, g_ref):
        y = y_ref[...].astype(F32)
        inv = lax.rsqrt(jnp.mean(y * y, axis=-1, keepdims=True) + RMS_EPS)
        return (y * inv * g_ref[...]).astype(BF16)

    mix = _dot(jnp.concatenate([rms(ya_ref, ga_ref), rms(yb_ref, gb_ref)], axis=1), wo_ref[...])
    h = _layer_norm(x_ref[...], eg_ref[...], eb_ref[...])
    h1 = _layer_norm(DEEPNORM_ALPHA * h + mix, g1_ref[...], b1_ref[...])
    h1_ref[...] = h1
    hb = h1.astype(BF16)

    hlo = (h1 - hb.astype(F32)).astype(BF16)
    logits = (_dot_nt(wrh_ref[...], hb) + _dot_nt(wrh_ref[...], hlo)
              + _dot_nt(wrl_ref[...], hb))
    scores = 1.0 / (1.0 + jnp.exp(-logits))
    top_e, top_w, sel, picks = _route(scores, rb_ref[...])

    row = lax.broadcasted_iota(jnp.int32, (tm, tm), 0)
    col = lax.broadcasted_iota(jnp.int32, (tm, tm), 1)
    before = jnp.where(row < col, 1.0, 0.0).astype(BF16)
    rank = _dot(sel.astype(BF16), before) + carry[...]
    rank3 = rank.reshape(N_GROUPS, GROUP_SIZE, tm)
    ranks = []
    for k in range(TOP_K):
        rk = jnp.sum(jnp.sum(jnp.where(picks[k], rank3, 0.0), axis=1, keepdims=True),
                     axis=0, keepdims=True)
        ranks.append(rk.reshape(1, tm))
    te_ref[...] = top_e
    tw_ref[...] = top_w
    tr_ref[...] = jnp.concatenate(ranks, axis=0).astype(jnp.int32)
    carry[...] = carry[...] + jnp.sum(sel, axis=-1, keepdims=True)
    cnt_ref[...] = jnp.broadcast_to(carry[...], cnt_ref.shape)


def _outproj_router(ya, yb, x2, wo_bf16, ga, gb, eg, eb, g1, b1, wrh, wrl, rb, tm):
    m = x2.shape[0]
    row = lambda w: pl.BlockSpec((tm, w), lambda i: (i, 0))
    full = lambda a: pl.BlockSpec(a.shape, lambda i: (0,) * a.ndim)
    tok = pl.BlockSpec((TOP_K, tm), lambda i: (0, i))
    return pl.pallas_call(
        _outproj_router_kernel,
        grid=(m // tm,),
        in_specs=[row(A_WIDTH), row(B_WIDTH), row(D_MODEL), full(wo_bf16), full(ga), full(gb),
                  full(eg), full(eb), full(g1), full(b1), full(wrh), full(wrl), full(rb)],
        out_specs=[row(D_MODEL), tok, tok, tok,
                   pl.BlockSpec((N_EXPERTS, LANES), lambda i: (0, 0))],
        out_shape=[jax.ShapeDtypeStruct((m, D_MODEL), F32),
                   jax.ShapeDtypeStruct((TOP_K, m), jnp.int32),
                   jax.ShapeDtypeStruct((TOP_K, m), F32),
                   jax.ShapeDtypeStruct((TOP_K, m), jnp.int32),
                   jax.ShapeDtypeStruct((N_EXPERTS, LANES), F32)],
        scratch_shapes=[pltpu.VMEM((N_EXPERTS, 1), F32)],
        compiler_params=_cparams(("arbitrary",)),
        name="outproj_router",
    )(ya, yb, x2, wo_bf16, ga, gb, eg, eb, g1, b1, wrh, wrl, rb)


def _zero_fill_padding(pad_start_ref, pad_len_ref, n_used, xs_hbm, zeros, sem):
    n_blocks = xs_hbm.shape[0] // MOE_BLOCK
    zeros[...] = jnp.zeros_like(zeros)

    def copies(act):
        def per_expert(e, carry):
            start = pad_start_ref[e]
            n = pad_len_ref[e]
            head = jnp.minimum((-start) & (SUBLANES - 1), n)
            for j in range(SUBLANES - 1):
                @pl.when(j < head)
                def _(j=j):
                    act(pltpu.make_async_copy(zeros.at[pl.ds(0, 1), :],
                                              xs_hbm.at[pl.ds(start + j, 1), :], sem.at[1]))
            aligned = start + head
            rest = n - head
            bit = MOE_BLOCK // 2
            while bit >= SUBLANES:
                @pl.when((rest & bit) != 0)
                def _(bit=bit):
                    off = pl.multiple_of(aligned + (rest & (-2 * bit)), SUBLANES)
                    act(pltpu.make_async_copy(zeros.at[pl.ds(0, bit), :],
                                              xs_hbm.at[pl.ds(off, bit), :], sem.at[1]))
                bit //= 2
            return carry

        def per_block(j, carry):
            off = pl.multiple_of(j * MOE_BLOCK, MOE_BLOCK)
            act(pltpu.make_async_copy(zeros, xs_hbm.at[pl.ds(off, MOE_BLOCK), :], sem.at[1]))
            return carry

        lax.fori_loop(0, N_EXPERTS, per_expert, 0)
        lax.fori_loop(n_used, n_blocks, per_block, 0)

    copies(lambda cp: cp.start())
    copies(lambda cp: cp.wait())


def _dispatch_kernel(pad_start_ref, pad_len_ref, nu_ref, dest_ref, h1_ref, wsg_ref, wsu_ref,
                     wsd_ref, home0_hbm, xs_hbm, sh_ref, home_hbm, packed, zeros, home, sem):
    i = pl.program_id(0)
    tm = h1_ref.shape[0]
    m = tm * pl.num_programs(0)

    @pl.when(i == 0)
    def _():
        cp = pltpu.make_async_copy(home0_hbm, home, sem.at[1])
        cp.start()
        cp.wait()
        _zero_fill_padding(pad_start_ref, pad_len_ref, nu_ref[0], xs_hbm, zeros, sem)

    h1 = h1_ref[...]
    packed[...] = _pack_rows(h1)
    for k in range(TOP_K):
        def record(t, carry, k=k):
            home[dest_ref[k, t]] = k * m + i * tm + t
            return carry
        lax.fori_loop(0, tm, record, 0, unroll=8)
    for t in range(tm):
        for k in range(TOP_K):
            pltpu.make_async_copy(packed.at[pl.ds(t, 1), :],
                                  xs_hbm.at[pl.ds(dest_ref[k, t], 1), :], sem.at[0]).start()
    hb = h1.astype(BF16)
    g = _dot(hb, wsg_ref[...])
    u = _dot(hb, wsu_ref[...])
    sh_ref[...] = _dot((_silu(g) * u).astype(BF16), wsd_ref[...]).astype(sh_ref.dtype)
    for k in range(TOP_K):
        pltpu.make_async_copy(packed, xs_hbm.at[pl.ds(0, tm), :], sem.at[0]).wait()

    @pl.when(i == pl.num_programs(0) - 1)
    def _():
        cp = pltpu.make_async_copy(home, home_hbm, sem.at[1])
        cp.start()
        cp.wait()


def _dispatch_shared(pad_start, pad_len, n_used, dest, h1, wsg, wsu, wsd, home0):
    m = h1.shape[0]
    tm = DISPATCH_TILE
    p_rows = home0.shape[0]
    full = lambda a: pl.BlockSpec(a.shape, lambda i, *_: (0,) * a.ndim)
    return pl.pallas_call(
        _dispatch_kernel,
        grid_spec=pltpu.PrefetchScalarGridSpec(
            num_scalar_prefetch=3,
            grid=(m // tm,),
            in_specs=[pl.BlockSpec((TOP_K, tm), lambda i, *_: (0, i), memory_space=pltpu.SMEM),
                      pl.BlockSpec((tm, D_MODEL), lambda i, *_: (i, 0)),
                      full(wsg), full(wsu), full(wsd),
                      pl.BlockSpec(memory_space=pl.ANY)],
            out_specs=[pl.BlockSpec(memory_space=pl.ANY),
                       pl.BlockSpec((tm, D_MODEL), lambda i, *_: (i, 0)),
                       pl.BlockSpec(memory_space=pl.ANY)],
            scratch_shapes=[pltpu.VMEM((tm, D_HALF), U32),
                            pltpu.VMEM((MOE_BLOCK, D_HALF), U32),
                            pltpu.SMEM((p_rows,), jnp.int32),
                            pltpu.SemaphoreType.DMA((2,))],
        ),
        out_shape=[jax.ShapeDtypeStruct((p_rows, D_HALF), U32),
                   jax.ShapeDtypeStruct((m, D_MODEL), BF16),
                   jax.ShapeDtypeStruct((p_rows,), jnp.int32)],
        compiler_params=pltpu.CompilerParams(
            dimension_semantics=("arbitrary",), vmem_limit_bytes=VMEM_LIMIT,
            disable_bounds_checks=True),
        name="dispatch_shared",
    )(pad_start, pad_len, n_used, dest, h1, wsg, wsu, wsd, home0)


def _moe_kernel(be_ref, nu_ref, home_prev_ref, x_ref, wg_ref, wu_ref, wd_ref, y_hbm,
                ybuf0, ybuf1, sem, wg_b, wu_b, wd_b):
    i = pl.program_id(0)
    n_used = nu_ref[0]
    e = be_ref[i]
    prev = be_ref[jnp.maximum(i - 1, 0)]
    even = i % 2 == 0
    bufs = (ybuf0, ybuf1)
    spare0 = y_hbm.shape[0] - 2 * MOE_BLOCK

    @pl.when(i == 0)
    def _():
        ybuf1[...] = jnp.zeros_like(ybuf1)
        for b in range(2):
            cp = pltpu.make_async_copy(
                ybuf1, y_hbm.at[pl.ds(spare0 + b * MOE_BLOCK, MOE_BLOCK), :], sem.at[1])
            cp.start()
            cp.wait()

    @pl.when(jnp.logical_and(i < n_used, jnp.logical_or(i == 0, e != prev)))
    def _():
        wg_b[...] = wg_ref[0].astype(BF16)
        wu_b[...] = wu_ref[0].astype(BF16)
        wd_b[...] = wd_ref[0].astype(BF16)

    def scatter_rows(b):
        for r in range(MOE_BLOCK):
            pltpu.make_async_copy(bufs[b].at[pl.ds(r, 1), :],
                                  y_hbm.at[pl.ds(home_prev_ref[0, 0, r], 1), :],
                                  sem.at[b]).start()

    def wait_rows(b):
        pltpu.make_async_copy(bufs[b], y_hbm.at[pl.ds(0, MOE_BLOCK), :], sem.at[b]).wait()

    def compute(b):
        hi, lo = _unpack_rows(x_ref[...])
        g = _dot(hi, wg_b[0:D_HALF, :]) + _dot(lo, wg_b[D_HALF:D_MODEL, :])
        u = _dot(hi, wu_b[0:D_HALF, :]) + _dot(lo, wu_b[D_HALF:D_MODEL, :])
        bufs[b][...] = _pack_rows(_dot((_silu(g) * u).astype(BF16), wd_b[...]))

    for b in range(2):
        mine = even if b == 0 else jnp.logical_not(even)

        @pl.when(mine & (i >= 2) & (i <= n_used + 1))
        def _():
            wait_rows(b)

        @pl.when(mine & (i >= 1) & (i < n_used))
        def _():
            scatter_rows(1 - b)
            compute(b)

        @pl.when(mine & (i >= 1) & (i == n_used))
        def _():
            scatter_rows(1 - b)

    @pl.when((i == 0) & (n_used > 0))
    def _():
        compute(0)


def _moe_experts(block_e, n_used, home, x_sorted, w_gate, w_up, w_down, m):
    n_blocks = x_sorted.shape[0] // MOE_BLOCK
    assert block_e.shape[0] == n_blocks + 2
    used = lambda i, be, nu: (jnp.minimum(i, nu[0] - 1), 0)
    weights = lambda shape: pl.BlockSpec(
        (1,) + shape, lambda i, be, nu: (be[jnp.minimum(i, nu[0] - 1)], 0, 0))
    return pl.pallas_call(
        _moe_kernel,
        grid_spec=pltpu.PrefetchScalarGridSpec(
            num_scalar_prefetch=2,
            grid=(n_blocks + 2,),
            in_specs=[
                pl.BlockSpec((1, 1, MOE_BLOCK),
                             lambda i, be, nu: (jnp.clip(i - 1, 0, n_blocks - 1), 0, 0),
                             memory_space=pltpu.SMEM),
                pl.BlockSpec((MOE_BLOCK, D_HALF), used),
                weights((D_MODEL, D_EXPERT)), weights((D_MODEL, D_EXPERT)),
                weights((D_EXPERT, D_MODEL)),
            ],
            out_specs=pl.BlockSpec(memory_space=pl.ANY),
            scratch_shapes=[pltpu.VMEM((MOE_BLOCK, D_HALF), U32),
                            pltpu.VMEM((MOE_BLOCK, D_HALF), U32),
                            pltpu.SemaphoreType.DMA((2,)),
                            pltpu.VMEM((D_MODEL, D_EXPERT), BF16),
                            pltpu.VMEM((D_MODEL, D_EXPERT), BF16),
                            pltpu.VMEM((D_EXPERT, D_MODEL), BF16)],
        ),
        out_shape=jax.ShapeDtypeStruct((TOP_K * m + 2 * MOE_BLOCK, D_HALF), U32),
        compiler_params=pltpu.CompilerParams(
            dimension_semantics=("arbitrary",), vmem_limit_bytes=VMEM_LIMIT,
            disable_bounds_checks=True),
        name="moe_experts",
    )(block_e, n_used, home.reshape(n_blocks, 1, MOE_BLOCK), x_sorted, w_gate, w_up, w_down)


def _final_kernel(h1_ref, sh_ref, *rest):
    ys_refs = rest[:TOP_K]
    tw_ref, g2_ref, b2_ref, o_ref = rest[TOP_K:]
    sh = sh_ref[...].astype(F32)
    left, right = sh[:, :D_HALF], sh[:, D_HALF:]
    for k in range(TOP_K):
        p = ys_refs[k][...]
        w = tw_ref[:, k:k + 1]
        left = left + lax.bitcast_convert_type(p & jnp.uint32(0xFFFF0000), F32) * w
        right = right + lax.bitcast_convert_type(p << 16, F32) * w
    ffn = jnp.concatenate([left, right], axis=1)
    o_ref[...] = _layer_norm(DEEPNORM_ALPHA * h1_ref[...] + ffn, g2_ref[...], b2_ref[...])


def _final(h1, sh, y_home, tw_t, g2, b2, tm):
    m = h1.shape[0]
    row = lambda w: pl.BlockSpec((tm, w), lambda i: (i, 0))
    full = lambda a: pl.BlockSpec(a.shape, lambda i: (0,) * a.ndim)
    choice = lambda k: pl.BlockSpec((tm, D_HALF), lambda i: (k * (m // tm) + i, 0))
    return pl.pallas_call(
        _final_kernel,
        grid=(m // tm,),
        in_specs=[row(D_MODEL), row(D_MODEL)] + [choice(k) for k in range(TOP_K)]
                 + [row(TOP_K), full(g2), full(b2)],
        out_specs=row(D_MODEL),
        out_shape=jax.ShapeDtypeStruct((m, D_MODEL), F32),
        compiler_params=_cparams(("parallel",)),
        name="combine_ln2",
    )(h1, sh, *([y_home] * TOP_K), tw_t, g2, b2)


def _pad_rows(a, rows):
    return jnp.concatenate([a, jnp.zeros((rows - a.shape[0],) + a.shape[1:], a.dtype)], axis=0)


def kernel(x, meta_tokens, ln_emb_g, ln_emb_b, t5_table, w_in, a_sink, na_rpb, g_norm_a, g_norm_b, w_out, ln1_g, ln1_b, w_router, router_bias, w_gate, w_up, w_down, ws_gate, ws_up, ws_down, ln2_g, ln2_b):
    bsz, seq, _ = x.shape
    m = bsz * seq
    r2 = lambda a: a.reshape(1, -1).astype(F32)
    x2 = x.reshape(m, D_MODEL)
    eg, eb = r2(ln_emb_g), r2(ln_emb_b)

    w_in_b = w_in[0].astype(BF16)
    tm = 1024 if m % 1024 == 0 else 128
    u = _ln_inproj(x2, eg, eb, w_in_b, tm, 1152).reshape(bsz, seq, IN_WIDTH)
    um = _ln_inproj(meta_tokens.astype(F32), eg, eb, w_in_b, N_META, 1152)
    cut = lambda c0, width: _pad_rows(um[:, c0 * LANES:c0 * LANES + width], LANES)

    ya = _window_attention(u, cut(COL_KA, KV_WIDTH), cut(COL_VA, KV_WIDTH),
                           _window_bias(t5_table, a_sink[0], seq), bsz, seq)
    yb = _neighbourhood_attention(u, cut(COL_KB, B_WIDTH), cut(COL_VB, B_WIDTH),
                                  _na_bias(na_rpb[0]), bsz, seq)

    wr_t = w_router[0].astype(F32).T
    wr_hi = wr_t.astype(BF16)
    wr_lo = (wr_t - wr_hi.astype(F32)).astype(BF16)
    tm4 = 256 if m % 256 == 0 else 128
    h1, top_e, top_w, top_r, cnt = _outproj_router(
        ya.reshape(m, A_WIDTH), yb.reshape(m, B_WIDTH), x2, w_out[0].astype(BF16),
        r2(g_norm_a), r2(g_norm_b), eg, eb, r2(ln1_g), r2(ln1_b), wr_hi, wr_lo,
        router_bias[0].astype(F32).reshape(N_EXPERTS, 1), tm4)

    counts = cnt[:, 0].astype(jnp.int32)
    nb_e = (counts + MOE_BLOCK - 1) // MOE_BLOCK
    bend = jnp.cumsum(nb_e)
    pstart = (bend - nb_e) * MOE_BLOCK
    n_blocks = (m * TOP_K) // MOE_BLOCK + N_EXPERTS
    p_rows = n_blocks * MOE_BLOCK
    blk = jnp.arange(n_blocks + 2, dtype=jnp.int32)
    block_e = jnp.minimum(jnp.sum((bend[None, :] <= blk[:, None]).astype(jnp.int32), axis=1),
                          N_EXPERTS - 1)
    n_used = bend[-1:].astype(jnp.int32)
    expert_ids = jnp.arange(N_EXPERTS, dtype=jnp.int32)[:, None, None]
    dest = top_r + jnp.sum(jnp.where(top_e[None] == expert_ids, pstart[:, None, None], 0),
                           axis=0)
    slot = jnp.arange(p_rows, dtype=jnp.int32)
    home0 = TOP_K * m + (slot // MOE_BLOCK % 2) * MOE_BLOCK + slot % MOE_BLOCK

    x_sorted, shared, home = _dispatch_shared(
        pstart + counts, nb_e * MOE_BLOCK - counts, n_used, dest, h1,
        ws_gate[0].astype(BF16), ws_up[0].astype(BF16), ws_down[0].astype(BF16), home0)
    y_home = _moe_experts(block_e, n_used, home, x_sorted, w_gate[0], w_up[0], w_down[0], m)

    out = _final(h1, shared, y_home, top_w.T, r2(ln2_g), r2(ln2_b), tm4)
    return out.reshape(bsz, seq, D_MODEL)
```

```python
import functools
import math

import jax
import jax.numpy as jnp
from jax import lax
from jax.experimental import pallas as pl
from jax.experimental.pallas import tpu as pltpu

F32 = jnp.float32
BF16 = jnp.bfloat16

D_MODEL = 2048
HEAD_DIM = 64
N_META = 16
GRID_W = 64
A_HEADS = 16
A_KV_HEADS = 4
A_WINDOW = 128
A_BLOCK = 128
N_BUCKETS = 32
MAX_DISTANCE = 128
B_HEADS = 16
NA_ROWS = 8
NA_COLS = 16
A_WIDTH = A_HEADS * HEAD_DIM
KV_WIDTH = A_KV_HEADS * HEAD_DIM
B_WIDTH = B_HEADS * HEAD_DIM
MIX_WIDTH = A_WIDTH + B_WIDTH
IN_WIDTH = A_WIDTH + 2 * KV_WIDTH + 3 * B_WIDTH
N_EXPERTS = 64
TOP_K = 8
N_GROUPS = 8
GROUP_SIZE = N_EXPERTS // N_GROUPS
TOPK_GROUPS = 4
D_EXPERT = 512
D_SHARED = 512
ROUTED_SCALE = 2.5
DEPTH = 1
DEEPNORM_ALPHA = (2 * DEPTH) ** 0.25
LN_EPS = 1e-5
RMS_EPS = 1e-6
NEG = -1e30
SCALE = HEAD_DIM ** -0.5

LANES = 128
LANE_BITS = LANES.bit_length() - 1
SUBLANES = 8
VMEM_LIMIT = 56 * 1024 * 1024

COL_QA = 0
COL_KA = A_WIDTH // LANES
COL_VA = (A_WIDTH + KV_WIDTH) // LANES
COL_QB = (A_WIDTH + 2 * KV_WIDTH) // LANES
COL_KB = COL_QB + B_WIDTH // LANES
COL_VB = COL_KB + B_WIDTH // LANES

WIN_BATCH = 8
NA_UNROLL = 8
MOE_BLOCK = 512
DISPATCH_TILE = 128
D_HALF = D_MODEL // 2
U32 = jnp.uint32


def _pack_rows(x):
    hi = lax.bitcast_convert_type(x[:, :D_HALF].astype(jnp.bfloat16).astype(F32), U32)
    lo = lax.bitcast_convert_type(x[:, D_HALF:].astype(jnp.bfloat16).astype(F32), U32)
    return hi | (lo >> 16)


def _unpack_rows(p):
    hi = lax.bitcast_convert_type(p & jnp.uint32(0xFFFF0000), F32).astype(BF16)
    lo = lax.bitcast_convert_type(p << 16, F32).astype(BF16)
    return hi, lo


def _cparams(sem):
    return pltpu.CompilerParams(dimension_semantics=sem, vmem_limit_bytes=VMEM_LIMIT)


def _layer_norm(x, g, b):
    mu = jnp.mean(x, axis=-1, keepdims=True)
    xc = x - mu
    var = jnp.mean(xc * xc, axis=-1, keepdims=True)
    return xc * lax.rsqrt(var + LN_EPS) * g + b


def _dot(a, b):
    return jnp.dot(a, b, preferred_element_type=F32)


def _dot_nt(a, b):
    return lax.dot_general(a, b, (((1,), (1,)), ((), ())), preferred_element_type=F32)


def _silu(g):
    return g / (1.0 + jnp.exp(-g))


def _ln_inproj_kernel(x_ref, g_ref, b_ref, w_ref, o_ref, h_scr):
    @pl.when(pl.program_id(1) == 0)
    def _():
        h_scr[...] = _layer_norm(x_ref[...], g_ref[...], b_ref[...]).astype(BF16)

    o_ref[...] = _dot(h_scr[...], w_ref[...]).astype(o_ref.dtype)


def _ln_inproj(x2, g, b, w_bf16, tm, tn):
    m = x2.shape[0]
    n = w_bf16.shape[1]
    return pl.pallas_call(
        _ln_inproj_kernel,
        grid=(m // tm, n // tn),
        in_specs=[
            pl.BlockSpec((tm, D_MODEL), lambda i, j: (i, 0)),
            pl.BlockSpec((1, D_MODEL), lambda i, j: (0, 0)),
            pl.BlockSpec((1, D_MODEL), lambda i, j: (0, 0)),
            pl.BlockSpec((D_MODEL, tn), lambda i, j: (0, j)),
        ],
        out_specs=pl.BlockSpec((tm, tn), lambda i, j: (i, j)),
        out_shape=jax.ShapeDtypeStruct((m, n), BF16),
        scratch_shapes=[pltpu.VMEM((tm, D_MODEL), BF16)],
        compiler_params=_cparams(("parallel", "arbitrary")),
        name="ln_inproj",
    )(x2, g, b, w_bf16)


def _t5_bucket(rel):
    nb = N_BUCKETS // 2
    max_exact = nb // 2
    ret = jnp.where(rel > 0, nb, 0)
    n = jnp.abs(rel)
    nf = jnp.maximum(n, 1).astype(F32)
    large = max_exact + (jnp.log(nf / max_exact) / math.log(MAX_DISTANCE / max_exact)
                         * (nb - max_exact)).astype(jnp.int32)
    large = jnp.minimum(large, nb - 1)
    return ret + jnp.where(n < max_exact, n, large)


def _lookup(table_t, idx, n):
    onehot = (idx[None] == jnp.arange(n, dtype=jnp.int32).reshape((n,) + (1,) * idx.ndim))
    return jnp.einsum('hb,b...->h...', table_t, onehot.astype(F32),
                      precision=lax.Precision.HIGHEST)


def _window_bias(t5_table, a_sink, seq):
    assert N_META + A_BLOCK - (N_META - 1) > MAX_DISTANCE
    nblk = seq // A_BLOCK
    t5_t = t5_table.astype(F32).T
    q_loc = jnp.arange(A_BLOCK, dtype=jnp.int32)
    k_loc = jnp.arange(3 * A_BLOCK, dtype=jnp.int32) - A_BLOCK
    rel = k_loc[None, :] - q_loc[:, None]
    band = _lookup(t5_t, _t5_bucket(rel), N_BUCKETS)
    win = jnp.abs(rel) <= A_WINDOW
    sink = jnp.broadcast_to(a_sink.astype(F32)[:, None, None], (A_HEADS, A_BLOCK, 1))
    pad = jnp.full((A_HEADS, A_BLOCK, LANES - N_META - 1), NEG, F32)
    out = []
    for blk in (0, min(1, nblk - 1), nblk - 1):
        gk = blk * A_BLOCK + k_loc
        valid = win & (gk >= 0)[None, :] & (gk < seq)[None, :]
        band_v = jnp.where(valid[None], band, NEG)
        q_pos = N_META + blk * A_BLOCK + q_loc
        rel_m = jnp.arange(N_META, dtype=jnp.int32)[None, :] - q_pos[:, None]
        bias_m = _lookup(t5_t, _t5_bucket(rel_m), N_BUCKETS)
        out.append(jnp.concatenate([bias_m, sink, pad, band_v], axis=-1))
    return jnp.stack(out, axis=0)


def _window_kernel(q_ref, kp_ref, kc_ref, kn_ref, vp_ref, vc_ref, vn_ref, km_ref, vm_ref,
                   bias_ref, o_ref):
    kall = jnp.concatenate([km_ref[...], kp_ref[0], kc_ref[0], kn_ref[0]], axis=0)
    vall = jnp.concatenate([vm_ref[...], vp_ref[0], vc_ref[0], vn_ref[0]], axis=0)
    lane = lax.broadcasted_iota(jnp.int32, (A_BLOCK, LANES), 1)
    in_lo = lane < HEAD_DIM
    group = A_HEADS // A_KV_HEADS
    n_heads = 2 * group
    pair_out = []
    for h0 in range(0, n_heads, WIN_BATCH):
        scores = []
        for hl in range(h0, h0 + WIN_BATCH):
            p, half = hl // 2, hl % 2
            kv_half = hl // group
            qp = q_ref[0, :, p * LANES:(p + 1) * LANES].astype(F32) * SCALE
            src = qp if half == kv_half else pltpu.roll(qp, HEAD_DIM, axis=1)
            keep = in_lo if kv_half == 0 else jnp.logical_not(in_lo)
            qm = jnp.where(keep, src, 0.0).astype(BF16)
            scores.append(_dot_nt(qm, kall))
        probs = []
        for j, s in enumerate(scores):
            s = s + bias_ref[0, h0 + j]
            m = jnp.max(s, axis=-1, keepdims=True)
            e = jnp.exp(s - m)
            probs.append((e.astype(BF16), jnp.sum(e, axis=-1, keepdims=True)))
        res = []
        for j, (e, l) in enumerate(probs):
            hl = h0 + j
            o = _dot(e, vall) / l
            if hl % 2 != hl // group:
                o = pltpu.roll(o, HEAD_DIM, axis=1)
            res.append(o)
        for j in range(0, WIN_BATCH, 2):
            pair_out.append(jnp.where(in_lo, res[j], res[j + 1]))
    o_ref[0] = jnp.concatenate(pair_out, axis=1).astype(o_ref.dtype)


def _window_attention(u, kmeta, vmeta, bias, bsz, seq):
    nblk = seq // A_BLOCK
    n_pairs = A_KV_HEADS // 2
    qw = A_WIDTH // n_pairs

    def variant(n):
        return jnp.where(n == 0, 0, jnp.where(n == nblk - 1, 2, 1))

    def kv_spec(col0, shift):
        return pl.BlockSpec(
            (1, A_BLOCK, LANES),
            lambda b, j, n: (b, jnp.clip(n + shift, 0, nblk - 1), col0 + j))

    return pl.pallas_call(
        _window_kernel,
        grid=(bsz, n_pairs, nblk),
        in_specs=[
            pl.BlockSpec((1, A_BLOCK, qw), lambda b, j, n: (b, n, j)),
            kv_spec(COL_KA, -1), kv_spec(COL_KA, 0), kv_spec(COL_KA, 1),
            kv_spec(COL_VA, -1), kv_spec(COL_VA, 0), kv_spec(COL_VA, 1),
            pl.BlockSpec((LANES, LANES), lambda b, j, n: (0, j)),
            pl.BlockSpec((LANES, LANES), lambda b, j, n: (0, j)),
            pl.BlockSpec((1, A_HEADS // n_pairs, A_BLOCK, 4 * LANES),
                         lambda b, j, n: (variant(n), j, 0, 0)),
        ],
        out_specs=pl.BlockSpec((1, A_BLOCK, qw), lambda b, j, n: (b, n, j)),
        out_shape=jax.ShapeDtypeStruct((bsz, seq, A_WIDTH), BF16),
        compiler_params=_cparams(("parallel", "parallel", "arbitrary")),
        name="window_attention",
    )(u, u, u, u, u, u, u, kmeta, vmeta, bias)


def _na_bias(rpb):
    qc = jnp.arange(GRID_W, dtype=jnp.int32)
    kc = jnp.arange(GRID_W, dtype=jnp.int32)
    cs = jnp.clip(qc - NA_COLS // 2, 0, GRID_W - NA_COLS)
    cmask = (kc[None, :] >= cs[:, None]) & (kc[None, :] < cs[:, None] + NA_COLS)
    dc = jnp.clip(kc[None, :] - qc[:, None] + NA_COLS - 1, 0, 2 * NA_COLS - 2)
    n_dc = 2 * NA_COLS - 1
    onehot = (dc[None] == jnp.arange(n_dc, dtype=jnp.int32)[:, None, None]).astype(F32)
    t = jnp.einsum('hrd,dqk->hrqk', rpb.astype(F32), onehot, precision=lax.Precision.HIGHEST)
    t = jnp.where(cmask[None, None], t, NEG)
    variants = []
    for oi in range(NA_ROWS):
        variants.append(jnp.concatenate(
            [t[:, i - oi + NA_ROWS - 1] for i in range(NA_ROWS)], axis=-1))
    return jnp.stack(variants, axis=1)


def _na_kernel(q_ref, k_ref, v_ref, km_ref, vm_ref, bias_ref, o_ref, *, rows):
    span = NA_ROWS * GRID_W
    lane = lax.broadcasted_iota(jnp.int32, (GRID_W, LANES), 1)
    in_lo = lane < HEAD_DIM
    meta_bias = jnp.where(lane < N_META, 0.0, NEG).astype(F32)
    km = km_ref[...]
    vm = vm_ref[...]

    def body(it, carry):
        work = []
        for j in range(NA_UNROLL):
            r = it * NA_UNROLL + j
            rs = jnp.clip(r - NA_ROWS // 2, 0, rows - NA_ROWS)
            oi = r - rs
            q0 = pl.multiple_of(r * GRID_W, GRID_W)
            k0 = pl.multiple_of(rs * GRID_W, GRID_W)
            q = q_ref[0, pl.ds(q0, GRID_W), :]
            ks = k_ref[0, pl.ds(k0, span), :]
            for hl in range(2):
                keep = in_lo if hl == 0 else jnp.logical_not(in_lo)
                qm = jnp.where(keep, q, jnp.zeros_like(q))
                work.append((_dot_nt(qm, ks), _dot_nt(qm, km), hl, oi, k0, q0))
        probs = []
        for sw, sm, hl, oi, k0, q0 in work:
            sw = sw * SCALE + bias_ref[hl, oi]
            sm = sm * SCALE + meta_bias
            m = jnp.maximum(jnp.max(sw, axis=-1, keepdims=True),
                            jnp.max(sm, axis=-1, keepdims=True))
            ew = jnp.exp(sw - m)
            em = jnp.exp(sm - m)
            l = jnp.sum(ew, axis=-1, keepdims=True) + jnp.sum(em, axis=-1, keepdims=True)
            probs.append((ew.astype(BF16), em.astype(BF16), l, k0, q0))
        res = []
        for ew, em, l, k0, q0 in probs:
            vs = v_ref[0, pl.ds(k0, span), :]
            res.append((_dot(ew, vs) + _dot(em, vm)) / l)
        for j in range(NA_UNROLL):
            q0 = probs[2 * j][4]
            out = jnp.where(in_lo, res[2 * j], res[2 * j + 1])
            o_ref[0, pl.ds(q0, GRID_W), :] = out.astype(o_ref.dtype)
        return carry

    lax.fori_loop(0, rows // NA_UNROLL, body, 0)


def _neighbourhood_attention(u, kmeta, vmeta, bias, bsz, seq):
    rows = seq // GRID_W
    assert rows >= NA_ROWS and rows % NA_UNROLL == 0
    n_pairs = B_HEADS // 2

    def col_spec(col0):
        return pl.BlockSpec((1, seq, LANES), lambda b, j: (b, 0, col0 + j))

    return pl.pallas_call(
        functools.partial(_na_kernel, rows=rows),
        grid=(bsz, n_pairs),
        in_specs=[
            col_spec(COL_QB), col_spec(COL_KB), col_spec(COL_VB),
            pl.BlockSpec((LANES, LANES), lambda b, j: (0, j)),
            pl.BlockSpec((LANES, LANES), lambda b, j: (0, j)),
            pl.BlockSpec((2, NA_ROWS, GRID_W, NA_ROWS * GRID_W), lambda b, j: (j, 0, 0, 0)),
        ],
        out_specs=pl.BlockSpec((1, seq, LANES), lambda b, j: (b, 0, j)),
        out_shape=jax.ShapeDtypeStruct((bsz, seq, B_WIDTH), BF16),
        compiler_params=_cparams(("parallel", "parallel")),
        name="neighbourhood_attention",
    )(u, u, u, kmeta, vmeta, bias)


def _route(scores, rbias):
    t = scores.shape[-1]
    ninf = -jnp.inf
    biased = scores + rbias
    b3 = biased.reshape(N_GROUPS, GROUP_SIZE, t)
    s3 = scores.reshape(N_GROUPS, GROUP_SIZE, t)
    io_in = lax.broadcasted_iota(jnp.int32, b3.shape, 1)
    io_g3 = lax.broadcasted_iota(jnp.int32, b3.shape, 0)
    io_e = io_g3 * GROUP_SIZE + io_in
    m1 = jnp.max(b3, axis=1, keepdims=True)
    i1 = jnp.min(jnp.where(b3 == m1, io_in, GROUP_SIZE), axis=1, keepdims=True)
    m2 = jnp.max(jnp.where(io_in == i1, ninf, b3), axis=1, keepdims=True)
    gs = m1 + m2
    io_g = lax.broadcasted_iota(jnp.int32, gs.shape, 0)
    gmask = jnp.zeros(gs.shape, jnp.bool_)
    cur = gs
    for _ in range(TOPK_GROUPS):
        m = jnp.max(cur, axis=0, keepdims=True)
        i = jnp.min(jnp.where(cur == m, io_g, N_GROUPS), axis=0, keepdims=True)
        pick = io_g == i
        gmask = jnp.logical_or(gmask, pick)
        cur = jnp.where(pick, ninf, cur)
    cur = jnp.where(gmask, b3, ninf)
    picks, top_e, top_s = [], [], []
    for _ in range(TOP_K):
        m = jnp.max(jnp.max(cur, axis=1, keepdims=True), axis=0, keepdims=True)
        i = jnp.min(jnp.min(jnp.where(cur == m, io_e, N_EXPERTS), axis=1, keepdims=True),
                    axis=0, keepdims=True)
        pick = io_e == i
        picks.append(pick)
        top_e.append(i.reshape(1, t))
        w = jnp.sum(jnp.sum(jnp.where(pick, s3, 0.0), axis=1, keepdims=True), axis=0, keepdims=True)
        top_s.append(w.reshape(1, t))
        cur = jnp.where(pick, ninf, cur)
    top_e = jnp.concatenate(top_e, axis=0)
    top_s = jnp.concatenate(top_s, axis=0)
    denom = top_s[0:1]
    for k in range(1, TOP_K):
        denom = denom + top_s[k:k + 1]
    top_w = top_s / (denom + 1e-20) * ROUTED_SCALE
    sel = picks[0]
    for k in range(1, TOP_K):
        sel = jnp.logical_or(sel, picks[k])
    sel = jnp.where(sel, 1.0, 0.0).astype(F32).reshape(N_EXPERTS, t)
    return top_e, top_w, sel, picks


def _outproj_router_kernel(ya_ref, yb_ref, x_ref, wo_ref, ga_ref, gb_ref, eg_ref, eb_ref,
                           g1_ref, b1_ref, wrh_ref, wrl_ref, rb_ref,
                           h1_ref, te_ref, tw_ref, tr_ref, cnt_ref, carry):
    tm = x_ref.shape[0]

    @pl.when(pl.program_id(0) == 0)
    def _():
        carry[...] = jnp.zeros_like(carry)

    def rms(y_ref, g_ref):
        y = y_ref[...].astype(F32)
        inv = lax.rsqrt(jnp.mean(y * y, axis=-1, keepdims=True) + RMS_EPS)
        return (y * inv * g_ref[...]).astype(BF16)

    mix = _dot(jnp.concatenate([rms(ya_ref, ga_ref), rms(yb_ref, gb_ref)], axis=1), wo_ref[...])
    h = _layer_norm(x_ref[...], eg_ref[...], eb_ref[...])
    h1 = _layer_norm(DEEPNORM_ALPHA * h + mix, g1_ref[...], b1_ref[...])
    h1_ref[...] = h1
    hb = h1.astype(BF16)

    hlo = (h1 - hb.astype(F32)).astype(BF16)
    logits = (_dot_nt(wrh_ref[...], hb) + _dot_nt(wrh_ref[...], hlo)
              + _dot_nt(wrl_ref[...], hb))
    scores = 1.0 / (1.0 + jnp.exp(-logits))
    top_e, top_w, sel, picks = _route(scores, rb_ref[...])

    row = lax.broadcasted_iota(jnp.int32, (tm, tm), 0)
    col = lax.broadcasted_iota(jnp.int32, (tm, tm), 1)
    before = jnp.where(row < col, 1.0, 0.0).astype(BF16)
    rank = _dot(sel.astype(BF16), before) + carry[...]
    rank3 = rank.reshape(N_GROUPS, GROUP_SIZE, tm)
    ranks = []
    for k in range(TOP_K):
        rk = jnp.sum(jnp.sum(jnp.where(picks[k], rank3, 0.0), axis=1, keepdims=True),
                     axis=0, keepdims=True)
        ranks.append(rk.reshape(1, tm))
    te_ref[...] = top_e
    tw_ref[...] = top_w
    tr_ref[...] = jnp.concatenate(ranks, axis=0).astype(jnp.int32)
    carry[...] = carry[...] + jnp.sum(sel, axis=-1, keepdims=True)
    cnt_ref[...] = jnp.broadcast_to(carry[...], cnt_ref.shape)


def _outproj_router(ya, yb, x2, wo_bf16, ga, gb, eg, eb, g1, b1, wrh, wrl, rb, tm):
    m = x2.shape[0]
    row = lambda w: pl.BlockSpec((tm, w), lambda i: (i, 0))
    full = lambda a: pl.BlockSpec(a.shape, lambda i: (0,) * a.ndim)
    tok = pl.BlockSpec((TOP_K, tm), lambda i: (0, i))
    return pl.pallas_call(
        _outproj_router_kernel,
        grid=(m // tm,),
        in_specs=[row(A_WIDTH), row(B_WIDTH), row(D_MODEL), full(wo_bf16), full(ga), full(gb),
                  full(eg), full(eb), full(g1), full(b1), full(wrh), full(wrl), full(rb)],
        out_specs=[row(D_MODEL), tok, tok, tok,
                   pl.BlockSpec((N_EXPERTS, LANES), lambda i: (0, 0))],
        out_shape=[jax.ShapeDtypeStruct((m, D_MODEL), F32),
                   jax.ShapeDtypeStruct((TOP_K, m), jnp.int32),
                   jax.ShapeDtypeStruct((TOP_K, m), F32),
                   jax.ShapeDtypeStruct((TOP_K, m), jnp.int32),
                   jax.ShapeDtypeStruct((N_EXPERTS, LANES), F32)],
        scratch_shapes=[pltpu.VMEM((N_EXPERTS, 1), F32)],
        compiler_params=_cparams(("arbitrary",)),
        name="outproj_router",
    )(ya, yb, x2, wo_bf16, ga, gb, eg, eb, g1, b1, wrh, wrl, rb)


def _zero_fill_padding(pad_start_ref, pad_len_ref, n_used, xs_hbm, zeros, sem):
    n_blocks = xs_hbm.shape[0] // MOE_BLOCK
    zeros[...] = jnp.zeros_like(zeros)

    def copies(act):
        def per_expert(e, carry):
            start = pad_start_ref[e]
            n = pad_len_ref[e]
            head = jnp.minimum((-start) & (SUBLANES - 1), n)
            for j in range(SUBLANES - 1):
                @pl.when(j < head)
                def _(j=j):
                    act(pltpu.make_async_copy(zeros.at[pl.ds(0, 1), :],
                                              xs_hbm.at[pl.ds(start + j, 1), :], sem.at[1]))
            aligned = start + head
            rest = n - head
            bit = MOE_BLOCK // 2
            while bit >= SUBLANES:
                @pl.when((rest & bit) != 0)
                def _(bit=bit):
                    off = pl.multiple_of(aligned + (rest & (-2 * bit)), SUBLANES)
                    act(pltpu.make_async_copy(zeros.at[pl.ds(0, bit), :],
                                              xs_hbm.at[pl.ds(off, bit), :], sem.at[1]))
                bit //= 2
            return carry

        def per_block(j, carry):
            off = pl.multiple_of(j * MOE_BLOCK, MOE_BLOCK)
            act(pltpu.make_async_copy(zeros, xs_hbm.at[pl.ds(off, MOE_BLOCK), :], sem.at[1]))
            return carry

        lax.fori_loop(0, N_EXPERTS, per_expert, 0)
        lax.fori_loop(n_used, n_blocks, per_block, 0)

    copies(lambda cp: cp.start())
    copies(lambda cp: cp.wait())


def _dispatch_kernel(pad_start_ref, pad_len_ref, nu_ref, dest_ref, h1_ref, wsg_ref, wsu_ref,
                     wsd_ref, home0_ref, xs_hbm, sh_ref, home_ref, packed, zeros, sem, *, m):
    i = pl.program_id(0)
    tm = h1_ref.shape[0]
    lane = lax.broadcasted_iota(jnp.int32, (1, LANES), 1)

    @pl.when(i == 0)
    def _():
        home_ref[...] = home0_ref[...]
        _zero_fill_padding(pad_start_ref, pad_len_ref, nu_ref[0], xs_hbm, zeros, sem)

    h1 = h1_ref[...]
    packed[...] = _pack_rows(h1)
    for t in range(tm):
        for k in range(TOP_K):
            slot = dest_ref[k, t]
            pltpu.store(home_ref.at[pl.ds(lax.shift_right_logical(slot, LANE_BITS), 1), :],
                        jnp.full((1, LANES), k * m + i * tm + t, jnp.int32),
                        mask=lane == (slot & (LANES - 1)))
            pltpu.make_async_copy(packed.at[pl.ds(t, 1), :],
                                  xs_hbm.at[pl.ds(slot, 1), :], sem.at[0]).start()
    hb = h1.astype(BF16)
    g = _dot(hb, wsg_ref[...])
    u = _dot(hb, wsu_ref[...])
    sh_ref[...] = _dot((_silu(g) * u).astype(BF16), wsd_ref[...]).astype(sh_ref.dtype)
    for k in range(TOP_K):
        pltpu.make_async_copy(packed, xs_hbm.at[pl.ds(0, tm), :], sem.at[0]).wait()


def _dispatch_shared(pad_start, pad_len, n_used, dest, h1, wsg, wsu, wsd, home0):
    m = h1.shape[0]
    tm = DISPATCH_TILE
    p_rows = home0.size
    full = lambda a: pl.BlockSpec(a.shape, lambda i, *_: (0,) * a.ndim)
    return pl.pallas_call(
        functools.partial(_dispatch_kernel, m=m),
        grid_spec=pltpu.PrefetchScalarGridSpec(
            num_scalar_prefetch=3,
            grid=(m // tm,),
            in_specs=[pl.BlockSpec((TOP_K, tm), lambda i, *_: (0, i), memory_space=pltpu.SMEM),
                      pl.BlockSpec((tm, D_MODEL), lambda i, *_: (i, 0)),
                      full(wsg), full(wsu), full(wsd), full(home0)],
            out_specs=[pl.BlockSpec(memory_space=pl.ANY),
                       pl.BlockSpec((tm, D_MODEL), lambda i, *_: (i, 0)),
                       full(home0)],
            scratch_shapes=[pltpu.VMEM((tm, D_HALF), U32),
                            pltpu.VMEM((MOE_BLOCK, D_HALF), U32),
                            pltpu.SemaphoreType.DMA((2,))],
        ),
        out_shape=[jax.ShapeDtypeStruct((p_rows, D_HALF), U32),
                   jax.ShapeDtypeStruct((m, D_MODEL), BF16),
                   jax.ShapeDtypeStruct(home0.shape, jnp.int32)],
        compiler_params=pltpu.CompilerParams(
            dimension_semantics=("arbitrary",), vmem_limit_bytes=VMEM_LIMIT,
            disable_bounds_checks=True),
        name="dispatch_shared",
    )(pad_start, pad_len, n_used, dest, h1, wsg, wsu, wsd, home0)


def _moe_kernel(be_ref, nu_ref, home_prev_ref, x_ref, wg_ref, wu_ref, wd_ref, y_hbm,
                ybuf0, ybuf1, sem, wg_b, wu_b, wd_b):
    i = pl.program_id(0)
    n_used = nu_ref[0]
    e = be_ref[i]
    prev = be_ref[jnp.maximum(i - 1, 0)]
    even = i % 2 == 0
    bufs = (ybuf0, ybuf1)
    spare0 = y_hbm.shape[0] - 2 * MOE_BLOCK

    @pl.when(i == 0)
    def _():
        ybuf1[...] = jnp.zeros_like(ybuf1)
        for b in range(2):
            cp = pltpu.make_async_copy(
                ybuf1, y_hbm.at[pl.ds(spare0 + b * MOE_BLOCK, MOE_BLOCK), :], sem.at[1])
            cp.start()
            cp.wait()

    @pl.when(jnp.logical_and(i < n_used, jnp.logical_or(i == 0, e != prev)))
    def _():
        wg_b[...] = wg_ref[0].astype(BF16)
        wu_b[...] = wu_ref[0].astype(BF16)
        wd_b[...] = wd_ref[0].astype(BF16)

    def scatter_rows(b):
        for r in range(MOE_BLOCK):
            pltpu.make_async_copy(bufs[b].at[pl.ds(r, 1), :],
                                  y_hbm.at[pl.ds(home_prev_ref[0, 0, r], 1), :],
                                  sem.at[b]).start()

    def wait_rows(b):
        pltpu.make_async_copy(bufs[b], y_hbm.at[pl.ds(0, MOE_BLOCK), :], sem.at[b]).wait()

    def compute(b):
        hi, lo = _unpack_rows(x_ref[...])
        g = _dot(hi, wg_b[0:D_HALF, :]) + _dot(lo, wg_b[D_HALF:D_MODEL, :])
        u = _dot(hi, wu_b[0:D_HALF, :]) + _dot(lo, wu_b[D_HALF:D_MODEL, :])
        bufs[b][...] = _pack_rows(_dot((_silu(g) * u).astype(BF16), wd_b[...]))

    for b in range(2):
        mine = even if b == 0 else jnp.logical_not(even)

        @pl.when(mine & (i >= 2) & (i <= n_used + 1))
        def _():
            wait_rows(b)

        @pl.when(mine & (i >= 1) & (i < n_used))
        def _():
            scatter_rows(1 - b)
            compute(b)

        @pl.when(mine & (i >= 1) & (i == n_used))
        def _():
            scatter_rows(1 - b)

    @pl.when((i == 0) & (n_used > 0))
    def _():
        compute(0)


def _moe_experts(block_e, n_used, home, x_sorted, w_gate, w_up, w_down, m):
    n_blocks = x_sorted.shape[0] // MOE_BLOCK
    assert block_e.shape[0] == n_blocks + 2
    used = lambda i, be, nu: (jnp.minimum(i, nu[0] - 1), 0)
    weights = lambda shape: pl.BlockSpec(
        (1,) + shape, lambda i, be, nu: (be[jnp.minimum(i, nu[0] - 1)], 0, 0))
    return pl.pallas_call(
        _moe_kernel,
        grid_spec=pltpu.PrefetchScalarGridSpec(
            num_scalar_prefetch=2,
            grid=(n_blocks + 2,),
            in_specs=[
                pl.BlockSpec((1, 1, MOE_BLOCK),
                             lambda i, be, nu: (jnp.clip(i - 1, 0, n_blocks - 1), 0, 0),
                             memory_space=pltpu.SMEM),
                pl.BlockSpec((MOE_BLOCK, D_HALF), used),
                weights((D_MODEL, D_EXPERT)), weights((D_MODEL, D_EXPERT)),
                weights((D_EXPERT, D_MODEL)),
            ],
            out_specs=pl.BlockSpec(memory_space=pl.ANY),
            scratch_shapes=[pltpu.VMEM((MOE_BLOCK, D_HALF), U32),
                            pltpu.VMEM((MOE_BLOCK, D_HALF), U32),
                            pltpu.SemaphoreType.DMA((2,)),
                            pltpu.VMEM((D_MODEL, D_EXPERT), BF16),
                            pltpu.VMEM((D_MODEL, D_EXPERT), BF16),
                            pltpu.VMEM((D_EXPERT, D_MODEL), BF16)],
        ),
        out_shape=jax.ShapeDtypeStruct((TOP_K * m + 2 * MOE_BLOCK, D_HALF), U32),
        compiler_params=pltpu.CompilerParams(
            dimension_semantics=("arbitrary",), vmem_limit_bytes=VMEM_LIMIT,
            disable_bounds_checks=True),
        name="moe_experts",
    )(block_e, n_used, home.reshape(n_blocks, 1, MOE_BLOCK), x_sorted, w_gate, w_up, w_down)


def _final_kernel(h1_ref, sh_ref, *rest):
    ys_refs = rest[:TOP_K]
    tw_ref, g2_ref, b2_ref, o_ref = rest[TOP_K:]
    sh = sh_ref[...].astype(F32)
    left, right = sh[:, :D_HALF], sh[:, D_HALF:]
    for k in range(TOP_K):
        p = ys_refs[k][...]
        w = tw_ref[:, k:k + 1]
        left = left + lax.bitcast_convert_type(p & jnp.uint32(0xFFFF0000), F32) * w
        right = right + lax.bitcast_convert_type(p << 16, F32) * w
    ffn = jnp.concatenate([left, right], axis=1)
    o_ref[...] = _layer_norm(DEEPNORM_ALPHA * h1_ref[...] + ffn, g2_ref[...], b2_ref[...])


def _final(h1, sh, y_home, tw_t, g2, b2, tm):
    m = h1.shape[0]
    row = lambda w: pl.BlockSpec((tm, w), lambda i: (i, 0))
    full = lambda a: pl.BlockSpec(a.shape, lambda i: (0,) * a.ndim)
    choice = lambda k: pl.BlockSpec((tm, D_HALF), lambda i: (k * (m // tm) + i, 0))
    return pl.pallas_call(
        _final_kernel,
        grid=(m // tm,),
        in_specs=[row(D_MODEL), row(D_MODEL)] + [choice(k) for k in range(TOP_K)]
                 + [row(TOP_K), full(g2), full(b2)],
        out_specs=row(D_MODEL),
        out_shape=jax.ShapeDtypeStruct((m, D_MODEL), F32),
        compiler_params=_cparams(("parallel",)),
        name="combine_ln2",
    )(h1, sh, *([y_home] * TOP_K), tw_t, g2, b2)


def _pad_rows(a, rows):
    return jnp.concatenate([a, jnp.zeros((rows - a.shape[0],) + a.shape[1:], a.dtype)], axis=0)


def kernel(x, meta_tokens, ln_emb_g, ln_emb_b, t5_table, w_in, a_sink, na_rpb, g_norm_a, g_norm_b, w_out, ln1_g, ln1_b, w_router, router_bias, w_gate, w_up, w_down, ws_gate, ws_up, ws_down, ln2_g, ln2_b):
    bsz, seq, _ = x.shape
    m = bsz * seq
    r2 = lambda a: a.reshape(1, -1).astype(F32)
    x2 = x.reshape(m, D_MODEL)
    eg, eb = r2(ln_emb_g), r2(ln_emb_b)

    w_in_b = w_in[0].astype(BF16)
    tm = 1024 if m % 1024 == 0 else 128
    u = _ln_inproj(x2, eg, eb, w_in_b, tm, 1152).reshape(bsz, seq, IN_WIDTH)
    um = _ln_inproj(meta_tokens.astype(F32), eg, eb, w_in_b, N_META, 1152)
    cut = lambda c0, width: _pad_rows(um[:, c0 * LANES:c0 * LANES + width], LANES)

    ya = _window_attention(u, cut(COL_KA, KV_WIDTH), cut(COL_VA, KV_WIDTH),
                           _window_bias(t5_table, a_sink[0], seq), bsz, seq)
    yb = _neighbourhood_attention(u, cut(COL_KB, B_WIDTH), cut(COL_VB, B_WIDTH),
                                  _na_bias(na_rpb[0]), bsz, seq)

    wr_t = w_router[0].astype(F32).T
    wr_hi = wr_t.astype(BF16)
    wr_lo = (wr_t - wr_hi.astype(F32)).astype(BF16)
    tm4 = 256 if m % 256 == 0 else 128
    h1, top_e, top_w, top_r, cnt = _outproj_router(
        ya.reshape(m, A_WIDTH), yb.reshape(m, B_WIDTH), x2, w_out[0].astype(BF16),
        r2(g_norm_a), r2(g_norm_b), eg, eb, r2(ln1_g), r2(ln1_b), wr_hi, wr_lo,
        router_bias[0].astype(F32).reshape(N_EXPERTS, 1), tm4)

    counts = cnt[:, 0].astype(jnp.int32)
    nb_e = (counts + MOE_BLOCK - 1) // MOE_BLOCK
    bend = jnp.cumsum(nb_e)
    pstart = (bend - nb_e) * MOE_BLOCK
    n_blocks = (m * TOP_K) // MOE_BLOCK + N_EXPERTS
    p_rows = n_blocks * MOE_BLOCK
    blk = jnp.arange(n_blocks + 2, dtype=jnp.int32)
    block_e = jnp.minimum(jnp.sum((bend[None, :] <= blk[:, None]).astype(jnp.int32), axis=1),
                          N_EXPERTS - 1)
    n_used = bend[-1:].astype(jnp.int32)
    expert_ids = jnp.arange(N_EXPERTS, dtype=jnp.int32)[:, None, None]
    dest = top_r + jnp.sum(jnp.where(top_e[None] == expert_ids, pstart[:, None, None], 0),
                           axis=0)
    slot = jnp.arange(p_rows, dtype=jnp.int32)
    home0 = (TOP_K * m + (slot // MOE_BLOCK % 2) * MOE_BLOCK + slot % MOE_BLOCK
             ).reshape(p_rows // LANES, LANES)

    x_sorted, shared, home = _dispatch_shared(
        pstart + counts, nb_e * MOE_BLOCK - counts, n_used, dest, h1,
        ws_gate[0].astype(BF16), ws_up[0].astype(BF16), ws_down[0].astype(BF16), home0)
    y_home = _moe_experts(block_e, n_used, home, x_sorted, w_gate[0], w_up[0], w_down[0], m)

    out = _final(h1, shared, y_home, top_w.T, r2(ln2_g), r2(ln2_b), tm4)
    return out.reshape(bsz, seq, D_MODEL)
```

```python
import functools
import math

import jax
import jax.numpy as jnp
from jax import lax
from jax.experimental import pallas as pl
from jax.experimental.pallas import tpu as pltpu

F32 = jnp.float32
BF16 = jnp.bfloat16

D_MODEL = 2048
HEAD_DIM = 64
N_META = 16
GRID_W = 64
A_HEADS = 16
A_KV_HEADS = 4
A_WINDOW = 128
A_BLOCK = 128
N_BUCKETS = 32
MAX_DISTANCE = 128
B_HEADS = 16
NA_ROWS = 8
NA_COLS = 16
A_WIDTH = A_HEADS * HEAD_DIM
KV_WIDTH = A_KV_HEADS * HEAD_DIM
B_WIDTH = B_HEADS * HEAD_DIM
MIX_WIDTH = A_WIDTH + B_WIDTH
IN_WIDTH = A_WIDTH + 2 * KV_WIDTH + 3 * B_WIDTH
N_EXPERTS = 64
TOP_K = 8
N_GROUPS = 8
GROUP_SIZE = N_EXPERTS // N_GROUPS
TOPK_GROUPS = 4
D_EXPERT = 512
D_SHARED = 512
ROUTED_SCALE = 2.5
DEPTH = 1
DEEPNORM_ALPHA = (2 * DEPTH) ** 0.25
LN_EPS = 1e-5
RMS_EPS = 1e-6
NEG = -1e30
SCALE = HEAD_DIM ** -0.5

LANES = 128
LANE_BITS = LANES.bit_length() - 1
SUBLANES = 8
VMEM_LIMIT = 56 * 1024 * 1024

COL_QA = 0
COL_KA = A_WIDTH // LANES
COL_VA = (A_WIDTH + KV_WIDTH) // LANES
COL_QB = (A_WIDTH + 2 * KV_WIDTH) // LANES
COL_KB = COL_QB + B_WIDTH // LANES
COL_VB = COL_KB + B_WIDTH // LANES

WIN_BATCH = 8
NA_UNROLL = 8
MOE_BLOCK = 512
DISPATCH_TILE = 128
D_HALF = D_MODEL // 2
U32 = jnp.uint32


def _pack_rows(x):
    hi = lax.bitcast_convert_type(x[:, :D_HALF].astype(jnp.bfloat16).astype(F32), U32)
    lo = lax.bitcast_convert_type(x[:, D_HALF:].astype(jnp.bfloat16).astype(F32), U32)
    return hi | (lo >> 16)


def _unpack_rows(p):
    hi = lax.bitcast_convert_type(p & jnp.uint32(0xFFFF0000), F32).astype(BF16)
    lo = lax.bitcast_convert_type(p << 16, F32).astype(BF16)
    return hi, lo


def _cparams(sem):
    return pltpu.CompilerParams(dimension_semantics=sem, vmem_limit_bytes=VMEM_LIMIT)


def _layer_norm(x, g, b):
    mu = jnp.mean(x, axis=-1, keepdims=True)
    xc = x - mu
    var = jnp.mean(xc * xc, axis=-1, keepdims=True)
    return xc * lax.rsqrt(var + LN_EPS) * g + b


def _dot(a, b):
    return jnp.dot(a, b, preferred_element_type=F32)


def _dot_nt(a, b):
    return lax.dot_general(a, b, (((1,), (1,)), ((), ())), preferred_element_type=F32)


def _silu(g):
    return g / (1.0 + jnp.exp(-g))


def _ln_inproj_kernel(x_ref, g_ref, b_ref, w_ref, o_ref, h_scr):
    @pl.when(pl.program_id(1) == 0)
    def _():
        h_scr[...] = _layer_norm(x_ref[...], g_ref[...], b_ref[...]).astype(BF16)

    o_ref[...] = _dot(h_scr[...], w_ref[...]).astype(o_ref.dtype)


def _ln_inproj(x2, g, b, w_bf16, tm, tn):
    m = x2.shape[0]
    n = w_bf16.shape[1]
    return pl.pallas_call(
        _ln_inproj_kernel,
        grid=(m // tm, n // tn),
        in_specs=[
            pl.BlockSpec((tm, D_MODEL), lambda i, j: (i, 0)),
            pl.BlockSpec((1, D_MODEL), lambda i, j: (0, 0)),
            pl.BlockSpec((1, D_MODEL), lambda i, j: (0, 0)),
            pl.BlockSpec((D_MODEL, tn), lambda i, j: (0, j)),
        ],
        out_specs=pl.BlockSpec((tm, tn), lambda i, j: (i, j)),
        out_shape=jax.ShapeDtypeStruct((m, n), BF16),
        scratch_shapes=[pltpu.VMEM((tm, D_MODEL), BF16)],
        compiler_params=_cparams(("parallel", "arbitrary")),
        name="ln_inproj",
    )(x2, g, b, w_bf16)


def _t5_bucket(rel):
    nb = N_BUCKETS // 2
    max_exact = nb // 2
    ret = jnp.where(rel > 0, nb, 0)
    n = jnp.abs(rel)
    nf = jnp.maximum(n, 1).astype(F32)
    large = max_exact + (jnp.log(nf / max_exact) / math.log(MAX_DISTANCE / max_exact)
                         * (nb - max_exact)).astype(jnp.int32)
    large = jnp.minimum(large, nb - 1)
    return ret + jnp.where(n < max_exact, n, large)


def _lookup(table_t, idx, n):
    onehot = (idx[None] == jnp.arange(n, dtype=jnp.int32).reshape((n,) + (1,) * idx.ndim))
    return jnp.einsum('hb,b...->h...', table_t, onehot.astype(F32),
                      precision=lax.Precision.HIGHEST)


def _window_bias(t5_table, a_sink, seq):
    assert N_META + A_BLOCK - (N_META - 1) > MAX_DISTANCE
    nblk = seq // A_BLOCK
    t5_t = t5_table.astype(F32).T
    q_loc = jnp.arange(A_BLOCK, dtype=jnp.int32)
    k_loc = jnp.arange(3 * A_BLOCK, dtype=jnp.int32) - A_BLOCK
    rel = k_loc[None, :] - q_loc[:, None]
    band = _lookup(t5_t, _t5_bucket(rel), N_BUCKETS)
    win = jnp.abs(rel) <= A_WINDOW
    sink = jnp.broadcast_to(a_sink.astype(F32)[:, None, None], (A_HEADS, A_BLOCK, 1))
    pad = jnp.full((A_HEADS, A_BLOCK, LANES - N_META - 1), NEG, F32)
    out = []
    for blk in (0, min(1, nblk - 1), nblk - 1):
        gk = blk * A_BLOCK + k_loc
        valid = win & (gk >= 0)[None, :] & (gk < seq)[None, :]
        band_v = jnp.where(valid[None], band, NEG)
        q_pos = N_META + blk * A_BLOCK + q_loc
        rel_m = jnp.arange(N_META, dtype=jnp.int32)[None, :] - q_pos[:, None]
        bias_m = _lookup(t5_t, _t5_bucket(rel_m), N_BUCKETS)
        out.append(jnp.concatenate([bias_m, sink, pad, band_v], axis=-1))
    return jnp.stack(out, axis=0)


def _window_kernel(q_ref, kp_ref, kc_ref, kn_ref, vp_ref, vc_ref, vn_ref, km_ref, vm_ref,
                   bias_ref, o_ref):
    kall = jnp.concatenate([km_ref[...], kp_ref[0], kc_ref[0], kn_ref[0]], axis=0)
    vall = jnp.concatenate([vm_ref[...], vp_ref[0], vc_ref[0], vn_ref[0]], axis=0)
    lane = lax.broadcasted_iota(jnp.int32, (A_BLOCK, LANES), 1)
    in_lo = lane < HEAD_DIM
    v_lo = lax.broadcasted_iota(jnp.int32, vall.shape, 1) < HEAD_DIM
    one = jnp.ones_like(vall)
    v_sum = (jnp.where(v_lo, vall, one), jnp.where(v_lo, one, vall))
    group = A_HEADS // A_KV_HEADS
    n_heads = 2 * group
    pair_out = []
    for h0 in range(0, n_heads, WIN_BATCH):
        scores = []
        for hl in range(h0, h0 + WIN_BATCH):
            p, half = hl // 2, hl % 2
            kv_half = hl // group
            qp = q_ref[0, :, p * LANES:(p + 1) * LANES].astype(F32) * SCALE
            src = qp if half == kv_half else pltpu.roll(qp, HEAD_DIM, axis=1)
            keep = in_lo if kv_half == 0 else jnp.logical_not(in_lo)
            qm = jnp.where(keep, src, 0.0).astype(BF16)
            scores.append(_dot_nt(qm, kall))
        probs = []
        for j, s in enumerate(scores):
            s = s + bias_ref[0, h0 + j]
            m = jnp.max(s, axis=-1, keepdims=True)
            probs.append(jnp.exp(s - m).astype(BF16))
        res = []
        for j, e in enumerate(probs):
            hl = h0 + j
            o = _dot(e, v_sum[hl // group])
            o = o / pltpu.roll(o, HEAD_DIM, axis=1)
            if hl % 2 != hl // group:
                o = pltpu.roll(o, HEAD_DIM, axis=1)
            res.append(o)
        for j in range(0, WIN_BATCH, 2):
            pair_out.append(jnp.where(in_lo, res[j], res[j + 1]))
    o_ref[0] = jnp.concatenate(pair_out, axis=1).astype(o_ref.dtype)


def _window_attention(u, kmeta, vmeta, bias, bsz, seq):
    nblk = seq // A_BLOCK
    n_pairs = A_KV_HEADS // 2
    qw = A_WIDTH // n_pairs

    def variant(n):
        return jnp.where(n == 0, 0, jnp.where(n == nblk - 1, 2, 1))

    def kv_spec(col0, shift):
        return pl.BlockSpec(
            (1, A_BLOCK, LANES),
            lambda b, j, n: (b, jnp.clip(n + shift, 0, nblk - 1), col0 + j))

    return pl.pallas_call(
        _window_kernel,
        grid=(bsz, n_pairs, nblk),
        in_specs=[
            pl.BlockSpec((1, A_BLOCK, qw), lambda b, j, n: (b, n, j)),
            kv_spec(COL_KA, -1), kv_spec(COL_KA, 0), kv_spec(COL_KA, 1),
            kv_spec(COL_VA, -1), kv_spec(COL_VA, 0), kv_spec(COL_VA, 1),
            pl.BlockSpec((LANES, LANES), lambda b, j, n: (0, j)),
            pl.BlockSpec((LANES, LANES), lambda b, j, n: (0, j)),
            pl.BlockSpec((1, A_HEADS // n_pairs, A_BLOCK, 4 * LANES),
                         lambda b, j, n: (variant(n), j, 0, 0)),
        ],
        out_specs=pl.BlockSpec((1, A_BLOCK, qw), lambda b, j, n: (b, n, j)),
        out_shape=jax.ShapeDtypeStruct((bsz, seq, A_WIDTH), BF16),
        compiler_params=_cparams(("parallel", "parallel", "arbitrary")),
        name="window_attention",
    )(u, u, u, u, u, u, u, kmeta, vmeta, bias)


def _na_bias(rpb):
    qc = jnp.arange(GRID_W, dtype=jnp.int32)
    kc = jnp.arange(GRID_W, dtype=jnp.int32)
    cs = jnp.clip(qc - NA_COLS // 2, 0, GRID_W - NA_COLS)
    cmask = (kc[None, :] >= cs[:, None]) & (kc[None, :] < cs[:, None] + NA_COLS)
    dc = jnp.clip(kc[None, :] - qc[:, None] + NA_COLS - 1, 0, 2 * NA_COLS - 2)
    n_dc = 2 * NA_COLS - 1
    onehot = (dc[None] == jnp.arange(n_dc, dtype=jnp.int32)[:, None, None]).astype(F32)
    t = jnp.einsum('hrd,dqk->hrqk', rpb.astype(F32), onehot, precision=lax.Precision.HIGHEST)
    t = jnp.where(cmask[None, None], t, NEG)
    variants = []
    for oi in range(NA_ROWS):
        variants.append(jnp.concatenate(
            [t[:, i - oi + NA_ROWS - 1] for i in range(NA_ROWS)], axis=-1))
    return jnp.stack(variants, axis=1)


def _na_kernel(q_ref, k_ref, v_ref, km_ref, vm_ref, bias_ref, o_ref, *, rows):
    span = NA_ROWS * GRID_W
    lane = lax.broadcasted_iota(jnp.int32, (GRID_W, LANES), 1)
    in_lo = lane < HEAD_DIM
    meta_bias = jnp.where(lane < N_META, 0.0, NEG).astype(F32)
    km = km_ref[...]
    vm = vm_ref[...]

    def body(it, carry):
        work = []
        for j in range(NA_UNROLL):
            r = it * NA_UNROLL + j
            rs = jnp.clip(r - NA_ROWS // 2, 0, rows - NA_ROWS)
            oi = r - rs
            q0 = pl.multiple_of(r * GRID_W, GRID_W)
            k0 = pl.multiple_of(rs * GRID_W, GRID_W)
            q = q_ref[0, pl.ds(q0, GRID_W), :]
            q = q * jnp.asarray(SCALE, q.dtype)
            ks = k_ref[0, pl.ds(k0, span), :]
            for hl in range(2):
                keep = in_lo if hl == 0 else jnp.logical_not(in_lo)
                qm = jnp.where(keep, q, jnp.zeros_like(q))
                work.append((_dot_nt(qm, ks), _dot_nt(qm, km), hl, oi, k0, q0))
        probs = []
        for sw, sm, hl, oi, k0, q0 in work:
            sw = sw + bias_ref[hl, oi]
            sm = sm + meta_bias
            m = jnp.maximum(jnp.max(sw, axis=-1, keepdims=True),
                            jnp.max(sm, axis=-1, keepdims=True))
            ew = jnp.exp(sw - m)
            em = jnp.exp(sm - m)
            l = jnp.sum(ew, axis=-1, keepdims=True) + jnp.sum(em, axis=-1, keepdims=True)
            probs.append((ew.astype(BF16), em.astype(BF16), l, k0, q0))
        res = []
        for ew, em, l, k0, q0 in probs:
            vs = v_ref[0, pl.ds(k0, span), :]
            res.append((_dot(ew, vs) + _dot(em, vm)) / l)
        for j in range(NA_UNROLL):
            q0 = probs[2 * j][4]
            out = jnp.where(in_lo, res[2 * j], res[2 * j + 1])
            o_ref[0, pl.ds(q0, GRID_W), :] = out.astype(o_ref.dtype)
        return carry

    lax.fori_loop(0, rows // NA_UNROLL, body, 0)


def _neighbourhood_attention(u, kmeta, vmeta, bias, bsz, seq):
    rows = seq // GRID_W
    assert rows >= NA_ROWS and rows % NA_UNROLL == 0
    n_pairs = B_HEADS // 2

    def col_spec(col0):
        return pl.BlockSpec((1, seq, LANES), lambda b, j: (b, 0, col0 + j))

    return pl.pallas_call(
        functools.partial(_na_kernel, rows=rows),
        grid=(bsz, n_pairs),
        in_specs=[
            col_spec(COL_QB), col_spec(COL_KB), col_spec(COL_VB),
            pl.BlockSpec((LANES, LANES), lambda b, j: (0, j)),
            pl.BlockSpec((LANES, LANES), lambda b, j: (0, j)),
            pl.BlockSpec((2, NA_ROWS, GRID_W, NA_ROWS * GRID_W), lambda b, j: (j, 0, 0, 0)),
        ],
        out_specs=pl.BlockSpec((1, seq, LANES), lambda b, j: (b, 0, j)),
        out_shape=jax.ShapeDtypeStruct((bsz, seq, B_WIDTH), BF16),
        compiler_params=_cparams(("parallel", "parallel")),
        name="neighbourhood_attention",
    )(u, u, u, kmeta, vmeta, bias)


def _route(scores, rbias):
    t = scores.shape[-1]
    ninf = -jnp.inf
    biased = scores + rbias
    b3 = biased.reshape(N_GROUPS, GROUP_SIZE, t)
    s3 = scores.reshape(N_GROUPS, GROUP_SIZE, t)
    io_in = lax.broadcasted_iota(jnp.int32, b3.shape, 1)
    io_g3 = lax.broadcasted_iota(jnp.int32, b3.shape, 0)
    io_e = io_g3 * GROUP_SIZE + io_in
    m1 = jnp.max(b3, axis=1, keepdims=True)
    i1 = jnp.min(jnp.where(b3 == m1, io_in, GROUP_SIZE), axis=1, keepdims=True)
    m2 = jnp.max(jnp.where(io_in == i1, ninf, b3), axis=1, keepdims=True)
    gs = m1 + m2
    io_g = lax.broadcasted_iota(jnp.int32, gs.shape, 0)
    gmask = jnp.zeros(gs.shape, jnp.bool_)
    cur = gs
    for _ in range(TOPK_GROUPS):
        m = jnp.max(cur, axis=0, keepdims=True)
        i = jnp.min(jnp.where(cur == m, io_g, N_GROUPS), axis=0, keepdims=True)
        pick = io_g == i
        gmask = jnp.logical_or(gmask, pick)
        cur = jnp.where(pick, ninf, cur)
    cur = jnp.where(gmask, b3, ninf)
    picks, top_e, top_s = [], [], []
    for _ in range(TOP_K):
        m = jnp.max(jnp.max(cur, axis=1, keepdims=True), axis=0, keepdims=True)
        i = jnp.min(jnp.min(jnp.where(cur == m, io_e, N_EXPERTS), axis=1, keepdims=True),
                    axis=0, keepdims=True)
        pick = io_e == i
        picks.append(pick)
        top_e.append(i.reshape(1, t))
        w = jnp.sum(jnp.sum(jnp.where(pick, s3, 0.0), axis=1, keepdims=True), axis=0, keepdims=True)
        top_s.append(w.reshape(1, t))
        cur = jnp.where(pick, ninf, cur)
    top_e = jnp.concatenate(top_e, axis=0)
    top_s = jnp.concatenate(top_s, axis=0)
    denom = top_s[0:1]
    for k in range(1, TOP_K):
        denom = denom + top_s[k:k + 1]
    top_w = top_s / (denom + 1e-20) * ROUTED_SCALE
    sel = picks[0]
    for k in range(1, TOP_K):
        sel = jnp.logical_or(sel, picks[k])
    sel = jnp.where(sel, 1.0, 0.0).astype(F32).reshape(N_EXPERTS, t)
    return top_e, top_w, sel, picks


def _outproj_router_kernel(ya_ref, yb_ref, x_ref, wo_ref, ga_ref, gb_ref, eg_ref, eb_ref,
                           g1_ref, b1_ref, wrh_ref, wrl_ref, rb_ref,
                           h1_ref, te_ref, tw_ref, tr_ref, cnt_ref, carry):
    tm = x_ref.shape[0]

    @pl.when(pl.program_id(0) == 0)
    def _():
        carry[...] = jnp.zeros_like(carry)

    def rms(y_ref, g_ref):
        y = y_ref[...].astype(F32)
        inv = lax.rsqrt(jnp.mean(y * y, axis=-1, keepdims=True) + RMS_EPS)
        return (y * inv * g_ref[...]).astype(BF16)

    mix = _dot(jnp.concatenate([rms(ya_ref, ga_ref), rms(yb_ref, gb_ref)], axis=1), wo_ref[...])
    h = _layer_norm(x_ref[...], eg_ref[...], eb_ref[...])
    h1 = _layer_norm(DEEPNORM_ALPHA * h + mix, g1_ref[...], b1_ref[...])
    h1_ref[...] = h1
    hb = h1.astype(BF16)

    hlo = (h1 - hb.astype(F32)).astype(BF16)
    logits = (_dot_nt(wrh_ref[...], hb) + _dot_nt(wrh_ref[...], hlo)
              + _dot_nt(wrl_ref[...], hb))
    scores = 1.0 / (1.0 + jnp.exp(-logits))
    top_e, top_w, sel, picks = _route(scores, rb_ref[...])

    row = lax.broadcasted_iota(jnp.int32, (tm, tm), 0)
    col = lax.broadcasted_iota(jnp.int32, (tm, tm), 1)
    before = jnp.where(row < col, 1.0, 0.0).astype(BF16)
    rank = _dot(sel.astype(BF16), before) + carry[...]
    rank3 = rank.reshape(N_GROUPS, GROUP_SIZE, tm)
    ranks = []
    for k in range(TOP_K):
        rk = jnp.sum(jnp.sum(jnp.where(picks[k], rank3, 0.0), axis=1, keepdims=True),
                     axis=0, keepdims=True)
        ranks.append(rk.reshape(1, tm))
    te_ref[...] = top_e
    tw_ref[...] = top_w
    tr_ref[...] = jnp.concatenate(ranks, axis=0).astype(jnp.int32)
    carry[...] = carry[...] + jnp.sum(sel, axis=-1, keepdims=True)
    cnt_ref[...] = jnp.broadcast_to(carry[...], cnt_ref.shape)


def _outproj_router(ya, yb, x2, wo_bf16, ga, gb, eg, eb, g1, b1, wrh, wrl, rb, tm):
    m = x2.shape[0]
    row = lambda w: pl.BlockSpec((tm, w), lambda i: (i, 0))
    full = lambda a: pl.BlockSpec(a.shape, lambda i: (0,) * a.ndim)
    tok = pl.BlockSpec((TOP_K, tm), lambda i: (0, i))
    return pl.pallas_call(
        _outproj_router_kernel,
        grid=(m // tm,),
        in_specs=[row(A_WIDTH), row(B_WIDTH), row(D_MODEL), full(wo_bf16), full(ga), full(gb),
                  full(eg), full(eb), full(g1), full(b1), full(wrh), full(wrl), full(rb)],
        out_specs=[row(D_MODEL), tok, tok, tok,
                   pl.BlockSpec((N_EXPERTS, LANES), lambda i: (0, 0))],
        out_shape=[jax.ShapeDtypeStruct((m, D_MODEL), F32),
                   jax.ShapeDtypeStruct((TOP_K, m), jnp.int32),
                   jax.ShapeDtypeStruct((TOP_K, m), F32),
                   jax.ShapeDtypeStruct((TOP_K, m), jnp.int32),
                   jax.ShapeDtypeStruct((N_EXPERTS, LANES), F32)],
        scratch_shapes=[pltpu.VMEM((N_EXPERTS, 1), F32)],
        compiler_params=_cparams(("arbitrary",)),
        name="outproj_router",
    )(ya, yb, x2, wo_bf16, ga, gb, eg, eb, g1, b1, wrh, wrl, rb)


def _zero_fill_padding(pad_start_ref, pad_len_ref, n_used, xs_hbm, zeros, sem):
    n_blocks = xs_hbm.shape[0] // MOE_BLOCK
    zeros[...] = jnp.zeros_like(zeros)

    def copies(act):
        def per_expert(e, carry):
            start = pad_start_ref[e]
            n = pad_len_ref[e]
            head = jnp.minimum((-start) & (SUBLANES - 1), n)
            for j in range(SUBLANES - 1):
                @pl.when(j < head)
                def _(j=j):
                    act(pltpu.make_async_copy(zeros.at[pl.ds(0, 1), :],
                                              xs_hbm.at[pl.ds(start + j, 1), :], sem.at[1]))
            aligned = start + head
            rest = n - head
            bit = MOE_BLOCK // 2
            while bit >= SUBLANES:
                @pl.when((rest & bit) != 0)
                def _(bit=bit):
                    off = pl.multiple_of(aligned + (rest & (-2 * bit)), SUBLANES)
                    act(pltpu.make_async_copy(zeros.at[pl.ds(0, bit), :],
                                              xs_hbm.at[pl.ds(off, bit), :], sem.at[1]))
                bit //= 2
            return carry

        def per_block(j, carry):
            off = pl.multiple_of(j * MOE_BLOCK, MOE_BLOCK)
            act(pltpu.make_async_copy(zeros, xs_hbm.at[pl.ds(off, MOE_BLOCK), :], sem.at[1]))
            return carry

        lax.fori_loop(0, N_EXPERTS, per_expert, 0)
        lax.fori_loop(n_used, n_blocks, per_block, 0)

    copies(lambda cp: cp.start())
    copies(lambda cp: cp.wait())


def _dispatch_kernel(pad_start_ref, pad_len_ref, nu_ref, dest_ref, h1_ref, wsg_ref, wsu_ref,
                     wsd_ref, home0_ref, wg_ref, wu_ref, wd_ref,
                     xs_hbm, sh_ref, home_ref, wgb_ref, wub_ref, wdb_ref,
                     packed, zeros, sem, *, m):
    i = pl.program_id(0)
    tm = h1_ref.shape[0]
    lane = lax.broadcasted_iota(jnp.int32, (1, LANES), 1)

    @pl.when(i == 0)
    def _():
        home_ref[...] = home0_ref[...]
        _zero_fill_padding(pad_start_ref, pad_len_ref, nu_ref[0], xs_hbm, zeros, sem)

    wgb_ref[...] = wg_ref[...].astype(BF16)
    wub_ref[...] = wu_ref[...].astype(BF16)
    wdb_ref[...] = wd_ref[...].astype(BF16)

    h1 = h1_ref[...]
    packed[...] = _pack_rows(h1)
    for t in range(tm):
        for k in range(TOP_K):
            slot = dest_ref[k, t]
            pltpu.store(home_ref.at[pl.ds(lax.shift_right_logical(slot, LANE_BITS), 1), :],
                        jnp.full((1, LANES), k * m + i * tm + t, jnp.int32),
                        mask=lane == (slot & (LANES - 1)))
            pltpu.make_async_copy(packed.at[pl.ds(t, 1), :],
                                  xs_hbm.at[pl.ds(slot, 1), :], sem.at[0]).start()
    hb = h1.astype(BF16)
    g = _dot(hb, wsg_ref[...])
    u = _dot(hb, wsu_ref[...])
    sh_ref[...] = _dot((_silu(g) * u).astype(BF16), wsd_ref[...]).astype(sh_ref.dtype)
    for k in range(TOP_K):
        pltpu.make_async_copy(packed, xs_hbm.at[pl.ds(0, tm), :], sem.at[0]).wait()


def _dispatch_shared(pad_start, pad_len, n_used, dest, h1, wsg, wsu, wsd, home0,
                     w_gate, w_up, w_down):
    m = h1.shape[0]
    tm = DISPATCH_TILE
    p_rows = home0.size
    full = lambda a: pl.BlockSpec(a.shape, lambda i, *_: (0,) * a.ndim)
    steps = m // tm
    bands = max(1, steps // N_EXPERTS)
    per_step = max(1, N_EXPERTS // steps)
    assert steps * per_step == N_EXPERTS * bands
    share = lambda a: pl.BlockSpec((per_step, a.shape[1] // bands, a.shape[2]),
                                   lambda i, *_: (i // bands, i % bands, 0))
    as_bf16 = lambda a: jax.ShapeDtypeStruct(a.shape, BF16)
    return pl.pallas_call(
        functools.partial(_dispatch_kernel, m=m),
        grid_spec=pltpu.PrefetchScalarGridSpec(
            num_scalar_prefetch=3,
            grid=(m // tm,),
            in_specs=[pl.BlockSpec((TOP_K, tm), lambda i, *_: (0, i), memory_space=pltpu.SMEM),
                      pl.BlockSpec((tm, D_MODEL), lambda i, *_: (i, 0)),
                      full(wsg), full(wsu), full(wsd), full(home0),
                      share(w_gate), share(w_up), share(w_down)],
            out_specs=[pl.BlockSpec(memory_space=pl.ANY),
                       pl.BlockSpec((tm, D_MODEL), lambda i, *_: (i, 0)),
                       full(home0), share(w_gate), share(w_up), share(w_down)],
            scratch_shapes=[pltpu.VMEM((tm, D_HALF), U32),
                            pltpu.VMEM((MOE_BLOCK, D_HALF), U32),
                            pltpu.SemaphoreType.DMA((2,))],
        ),
        out_shape=[jax.ShapeDtypeStruct((p_rows, D_HALF), U32),
                   jax.ShapeDtypeStruct((m, D_MODEL), BF16),
                   jax.ShapeDtypeStruct(home0.shape, jnp.int32),
                   as_bf16(w_gate), as_bf16(w_up), as_bf16(w_down)],
        compiler_params=pltpu.CompilerParams(
            dimension_semantics=("arbitrary",), vmem_limit_bytes=VMEM_LIMIT,
            disable_bounds_checks=True),
        name="dispatch_shared",
    )(pad_start, pad_len, n_used, dest, h1, wsg, wsu, wsd, home0, w_gate, w_up, w_down)


def _moe_kernel(be_ref, nu_ref, home_prev_ref, x_ref, wg_ref, wu_ref, wd_ref, y_hbm,
                ybuf0, ybuf1, sem, wg_b, wu_b, wd_b):
    i = pl.program_id(0)
    n_used = nu_ref[0]
    e = be_ref[i]
    prev = be_ref[jnp.maximum(i - 1, 0)]
    even = i % 2 == 0
    bufs = (ybuf0, ybuf1)
    spare0 = y_hbm.shape[0] - 2 * MOE_BLOCK

    @pl.when(i == 0)
    def _():
        ybuf1[...] = jnp.zeros_like(ybuf1)
        for b in range(2):
            cp = pltpu.make_async_copy(
                ybuf1, y_hbm.at[pl.ds(spare0 + b * MOE_BLOCK, MOE_BLOCK), :], sem.at[1])
            cp.start()
            cp.wait()

    @pl.when(jnp.logical_and(i < n_used, jnp.logical_or(i == 0, e != prev)))
    def _():
        wg_b[...] = wg_ref[0]
        wu_b[...] = wu_ref[0]
        wd_b[...] = wd_ref[0]

    def scatter_rows(b):
        for r in range(MOE_BLOCK):
            pltpu.make_async_copy(bufs[b].at[pl.ds(r, 1), :],
                                  y_hbm.at[pl.ds(home_prev_ref[0, 0, r], 1), :],
                                  sem.at[b]).start()

    def wait_rows(b):
        pltpu.make_async_copy(bufs[b], y_hbm.at[pl.ds(0, MOE_BLOCK), :], sem.at[b]).wait()

    def compute(b):
        hi, lo = _unpack_rows(x_ref[...])
        g = _dot(hi, wg_b[0:D_HALF, :]) + _dot(lo, wg_b[D_HALF:D_MODEL, :])
        u = _dot(hi, wu_b[0:D_HALF, :]) + _dot(lo, wu_b[D_HALF:D_MODEL, :])
        bufs[b][...] = _pack_rows(_dot((_silu(g) * u).astype(BF16), wd_b[...]))

    for b in range(2):
        mine = even if b == 0 else jnp.logical_not(even)

        @pl.when(mine & (i >= 2) & (i <= n_used + 1))
        def _():
            wait_rows(b)

        @pl.when(mine & (i >= 1) & (i < n_used))
        def _():
            scatter_rows(1 - b)
            compute(b)

        @pl.when(mine & (i >= 1) & (i == n_used))
        def _():
            scatter_rows(1 - b)

    @pl.when((i == 0) & (n_used > 0))
    def _():
        compute(0)


def _moe_experts(block_e, n_used, home, x_sorted, w_gate, w_up, w_down, m):
    n_blocks = x_sorted.shape[0] // MOE_BLOCK
    assert block_e.shape[0] == n_blocks + 2
    used = lambda i, be, nu: (jnp.minimum(i, nu[0] - 1), 0)
    weights = lambda shape: pl.BlockSpec(
        (1,) + shape, lambda i, be, nu: (be[jnp.minimum(i, nu[0] - 1)], 0, 0))
    return pl.pallas_call(
        _moe_kernel,
        grid_spec=pltpu.PrefetchScalarGridSpec(
            num_scalar_prefetch=2,
            grid=(n_blocks + 2,),
            in_specs=[
                pl.BlockSpec((1, 1, MOE_BLOCK),
                             lambda i, be, nu: (jnp.clip(i - 1, 0, n_blocks - 1), 0, 0),
                             memory_space=pltpu.SMEM),
                pl.BlockSpec((MOE_BLOCK, D_HALF), used),
                weights((D_MODEL, D_EXPERT)), weights((D_MODEL, D_EXPERT)),
                weights((D_EXPERT, D_MODEL)),
            ],
            out_specs=pl.BlockSpec(memory_space=pl.ANY),
            scratch_shapes=[pltpu.VMEM((MOE_BLOCK, D_HALF), U32),
                            pltpu.VMEM((MOE_BLOCK, D_HALF), U32),
                            pltpu.SemaphoreType.DMA((2,)),
                            pltpu.VMEM((D_MODEL, D_EXPERT), BF16),
                            pltpu.VMEM((D_MODEL, D_EXPERT), BF16),
                            pltpu.VMEM((D_EXPERT, D_MODEL), BF16)],
        ),
        out_shape=jax.ShapeDtypeStruct((TOP_K * m + 2 * MOE_BLOCK, D_HALF), U32),
        compiler_params=pltpu.CompilerParams(
            dimension_semantics=("arbitrary",), vmem_limit_bytes=VMEM_LIMIT,
            disable_bounds_checks=True),
        name="moe_experts",
    )(block_e, n_used, home.reshape(n_blocks, 1, MOE_BLOCK), x_sorted, w_gate, w_up, w_down)


def _final_kernel(h1_ref, sh_ref, *rest):
    ys_refs = rest[:TOP_K]
    tw_ref, g2_ref, b2_ref, o_ref = rest[TOP_K:]
    sh = sh_ref[...].astype(F32)
    left, right = sh[:, :D_HALF], sh[:, D_HALF:]
    for k in range(TOP_K):
        p = ys_refs[k][...]
        w = tw_ref[:, k:k + 1]
        left = left + lax.bitcast_convert_type(p & jnp.uint32(0xFFFF0000), F32) * w
        right = right + lax.bitcast_convert_type(p << 16, F32) * w
    ffn = jnp.concatenate([left, right], axis=1)
    o_ref[...] = _layer_norm(DEEPNORM_ALPHA * h1_ref[...] + ffn, g2_ref[...], b2_ref[...])


def _final(h1, sh, y_home, tw_t, g2, b2, tm):
    m = h1.shape[0]
    row = lambda w: pl.BlockSpec((tm, w), lambda i: (i, 0))
    full = lambda a: pl.BlockSpec(a.shape, lambda i: (0,) * a.ndim)
    choice = lambda k: pl.BlockSpec((tm, D_HALF), lambda i: (k * (m // tm) + i, 0))
    return pl.pallas_call(
        _final_kernel,
        grid=(m // tm,),
        in_specs=[row(D_MODEL), row(D_MODEL)] + [choice(k) for k in range(TOP_K)]
                 + [row(TOP_K), full(g2), full(b2)],
        out_specs=row(D_MODEL),
        out_shape=jax.ShapeDtypeStruct((m, D_MODEL), F32),
        compiler_params=_cparams(("parallel",)),
        name="combine_ln2",
    )(h1, sh, *([y_home] * TOP_K), tw_t, g2, b2)


def _pad_rows(a, rows):
    return jnp.concatenate([a, jnp.zeros((rows - a.shape[0],) + a.shape[1:], a.dtype)], axis=0)


def kernel(x, meta_tokens, ln_emb_g, ln_emb_b, t5_table, w_in, a_sink, na_rpb, g_norm_a, g_norm_b, w_out, ln1_g, ln1_b, w_router, router_bias, w_gate, w_up, w_down, ws_gate, ws_up, ws_down, ln2_g, ln2_b):
    bsz, seq, _ = x.shape
    m = bsz * seq
    r2 = lambda a: a.reshape(1, -1).astype(F32)
    x2 = x.reshape(m, D_MODEL)
    eg, eb = r2(ln_emb_g), r2(ln_emb_b)

    w_in_b = w_in[0].astype(BF16)
    tm = 1024 if m % 1024 == 0 else 128
    u = _ln_inproj(x2, eg, eb, w_in_b, tm, 1152).reshape(bsz, seq, IN_WIDTH)
    um = _ln_inproj(meta_tokens.astype(F32), eg, eb, w_in_b, N_META, 1152)
    cut = lambda c0, width: _pad_rows(um[:, c0 * LANES:c0 * LANES + width], LANES)

    ya = _window_attention(u, cut(COL_KA, KV_WIDTH), cut(COL_VA, KV_WIDTH),
                           _window_bias(t5_table, a_sink[0], seq), bsz, seq)
    yb = _neighbourhood_attention(u, cut(COL_KB, B_WIDTH), cut(COL_VB, B_WIDTH),
                                  _na_bias(na_rpb[0]), bsz, seq)

    wr_t = w_router[0].astype(F32).T
    wr_hi = wr_t.astype(BF16)
    wr_lo = (wr_t - wr_hi.astype(F32)).astype(BF16)
    tm4 = 256 if m % 256 == 0 else 128
    h1, top_e, top_w, top_r, cnt = _outproj_router(
        ya.reshape(m, A_WIDTH), yb.reshape(m, B_WIDTH), x2, w_out[0].astype(BF16),
        r2(g_norm_a), r2(g_norm_b), eg, eb, r2(ln1_g), r2(ln1_b), wr_hi, wr_lo,
        router_bias[0].astype(F32).reshape(N_EXPERTS, 1), tm4)

    counts = cnt[:, 0].astype(jnp.int32)
    nb_e = (counts + MOE_BLOCK - 1) // MOE_BLOCK
    bend = jnp.cumsum(nb_e)
    pstart = (bend - nb_e) * MOE_BLOCK
    n_blocks = (m * TOP_K) // MOE_BLOCK + N_EXPERTS
    p_rows = n_blocks * MOE_BLOCK
    blk = jnp.arange(n_blocks + 2, dtype=jnp.int32)
    block_e = jnp.minimum(jnp.sum((bend[None, :] <= blk[:, None]).astype(jnp.int32), axis=1),
                          N_EXPERTS - 1)
    n_used = bend[-1:].astype(jnp.int32)
    expert_ids = jnp.arange(N_EXPERTS, dtype=jnp.int32)[:, None, None]
    dest = top_r + jnp.sum(jnp.where(top_e[None] == expert_ids, pstart[:, None, None], 0),
                           axis=0)
    slot = jnp.arange(p_rows, dtype=jnp.int32)
    home0 = (TOP_K * m + (slot // MOE_BLOCK % 2) * MOE_BLOCK + slot % MOE_BLOCK
             ).reshape(p_rows // LANES, LANES)

    x_sorted, shared, home, wg_b, wu_b, wd_b = _dispatch_shared(
        pstart + counts, nb_e * MOE_BLOCK - counts, n_used, dest, h1,
        ws_gate[0].astype(BF16), ws_up[0].astype(BF16), ws_down[0].astype(BF16), home0,
        w_gate[0], w_up[0], w_down[0])
    y_home = _moe_experts(block_e, n_used, home, x_sorted, wg_b, wu_b, wd_b, m)

    out = _final(h1, shared, y_home, top_w.T, r2(ln2_g), r2(ln2_b), tm4)
    return out.reshape(bsz, seq, D_MODEL)
```

```python
import functools
import math

import jax
import jax.numpy as jnp
from jax import lax
from jax.experimental import pallas as pl
from jax.experimental.pallas import tpu as pltpu

F32 = jnp.float32
BF16 = jnp.bfloat16

D_MODEL = 2048
HEAD_DIM = 64
N_META = 16
GRID_W = 64
A_HEADS = 16
A_KV_HEADS = 4
A_WINDOW = 128
A_BLOCK = 128
N_BUCKETS = 32
MAX_DISTANCE = 128
B_HEADS = 16
NA_ROWS = 8
NA_COLS = 16
A_WIDTH = A_HEADS * HEAD_DIM
KV_WIDTH = A_KV_HEADS * HEAD_DIM
B_WIDTH = B_HEADS * HEAD_DIM
MIX_WIDTH = A_WIDTH + B_WIDTH
IN_WIDTH = A_WIDTH + 2 * KV_WIDTH + 3 * B_WIDTH
N_EXPERTS = 64
TOP_K = 8
N_GROUPS = 8
GROUP_SIZE = N_EXPERTS // N_GROUPS
TOPK_GROUPS = 4
D_EXPERT = 512
ROUTED_SCALE = 2.5
DEPTH = 1
DEEPNORM_ALPHA = (2 * DEPTH) ** 0.25
LN_EPS = 1e-5
RMS_EPS = 1e-6
NEG = -1e30
SCALE = HEAD_DIM ** -0.5

LANES = 128
LANE_BITS = LANES.bit_length() - 1
SUBLANES = 8
VMEM_LIMIT = 56 * 1024 * 1024

COL_QA = 0
COL_KA = A_WIDTH // LANES
COL_VA = (A_WIDTH + KV_WIDTH) // LANES
COL_QB = (A_WIDTH + 2 * KV_WIDTH) // LANES
COL_KB = COL_QB + B_WIDTH // LANES
COL_VB = COL_KB + B_WIDTH // LANES

WIN_BATCH = 8
NA_UNROLL = 8
MOE_BLOCK = 512
DISPATCH_TILE = 128
D_HALF = D_MODEL // 2
U32 = jnp.uint32


def _pack_rows(x):
    hi = lax.bitcast_convert_type(x[:, :D_HALF].astype(jnp.bfloat16).astype(F32), U32)
    lo = lax.bitcast_convert_type(x[:, D_HALF:].astype(jnp.bfloat16).astype(F32), U32)
    return hi | (lo >> 16)


def _unpack_rows(p):
    hi = lax.bitcast_convert_type(p & jnp.uint32(0xFFFF0000), F32).astype(BF16)
    lo = lax.bitcast_convert_type(p << 16, F32).astype(BF16)
    return hi, lo


def _cparams(sem):
    return pltpu.CompilerParams(dimension_semantics=sem, vmem_limit_bytes=VMEM_LIMIT)


def _layer_norm(x, g, b):
    mu = jnp.mean(x, axis=-1, keepdims=True)
    xc = x - mu
    var = jnp.mean(xc * xc, axis=-1, keepdims=True)
    return xc * lax.rsqrt(var + LN_EPS) * g + b


def _dot(a, b):
    return jnp.dot(a, b, preferred_element_type=F32)


def _dot_nt(a, b):
    return lax.dot_general(a, b, (((1,), (1,)), ((), ())), preferred_element_type=F32)


def _silu(g):
    return g / (1.0 + jnp.exp(-g))


def _ln_inproj_kernel(x_ref, g_ref, b_ref, w_ref, o_ref, h_scr):
    @pl.when(pl.program_id(1) == 0)
    def _():
        h_scr[...] = _layer_norm(x_ref[...], g_ref[...], b_ref[...]).astype(BF16)

    o_ref[...] = _dot(h_scr[...], w_ref[...]).astype(o_ref.dtype)


def _ln_inproj(x2, g, b, w_bf16, tm, tn):
    m = x2.shape[0]
    n = w_bf16.shape[1]
    return pl.pallas_call(
        _ln_inproj_kernel,
        grid=(m // tm, n // tn),
        in_specs=[
            pl.BlockSpec((tm, D_MODEL), lambda i, j: (i, 0)),
            pl.BlockSpec((1, D_MODEL), lambda i, j: (0, 0)),
            pl.BlockSpec((1, D_MODEL), lambda i, j: (0, 0)),
            pl.BlockSpec((D_MODEL, tn), lambda i, j: (0, j)),
        ],
        out_specs=pl.BlockSpec((tm, tn), lambda i, j: (i, j)),
        out_shape=jax.ShapeDtypeStruct((m, n), BF16),
        scratch_shapes=[pltpu.VMEM((tm, D_MODEL), BF16)],
        compiler_params=_cparams(("parallel", "arbitrary")),
        name="ln_inproj",
    )(x2, g, b, w_bf16)


def _t5_bucket(rel):
    nb = N_BUCKETS // 2
    max_exact = nb // 2
    ret = jnp.where(rel > 0, nb, 0)
    n = jnp.abs(rel)
    nf = jnp.maximum(n, 1).astype(F32)
    large = max_exact + (jnp.log(nf / max_exact) / math.log(MAX_DISTANCE / max_exact)
                         * (nb - max_exact)).astype(jnp.int32)
    large = jnp.minimum(large, nb - 1)
    return ret + jnp.where(n < max_exact, n, large)


def _lookup(table_t, idx, n):
    onehot = (idx[None] == jnp.arange(n, dtype=jnp.int32).reshape((n,) + (1,) * idx.ndim))
    return jnp.einsum('hb,b...->h...', table_t, onehot.astype(F32),
                      precision=lax.Precision.HIGHEST)


def _window_bias(t5_table, a_sink, seq):
    assert N_META + A_BLOCK - (N_META - 1) > MAX_DISTANCE
    nblk = seq // A_BLOCK
    t5_t = t5_table.astype(F32).T
    q_loc = jnp.arange(A_BLOCK, dtype=jnp.int32)
    k_loc = jnp.arange(3 * A_BLOCK, dtype=jnp.int32) - A_BLOCK
    rel = k_loc[None, :] - q_loc[:, None]
    band = _lookup(t5_t, _t5_bucket(rel), N_BUCKETS)
    win = jnp.abs(rel) <= A_WINDOW
    sink = jnp.broadcast_to(a_sink.astype(F32)[:, None, None], (A_HEADS, A_BLOCK, 1))
    pad = jnp.full((A_HEADS, A_BLOCK, LANES - N_META - 1), NEG, F32)
    out = []
    for blk in (0, min(1, nblk - 1), nblk - 1):
        gk = blk * A_BLOCK + k_loc
        valid = win & (gk >= 0)[None, :] & (gk < seq)[None, :]
        band_v = jnp.where(valid[None], band, NEG)
        q_pos = N_META + blk * A_BLOCK + q_loc
        rel_m = jnp.arange(N_META, dtype=jnp.int32)[None, :] - q_pos[:, None]
        bias_m = _lookup(t5_t, _t5_bucket(rel_m), N_BUCKETS)
        out.append(jnp.concatenate([bias_m, sink, pad, band_v], axis=-1))
    return jnp.stack(out, axis=0)


def _window_kernel(q_ref, kp_ref, kc_ref, kn_ref, vp_ref, vc_ref, vn_ref, km_ref, vm_ref,
                   bias_ref, o_ref):
    kall = jnp.concatenate([km_ref[...], kp_ref[0], kc_ref[0], kn_ref[0]], axis=0)
    vall = jnp.concatenate([vm_ref[...], vp_ref[0], vc_ref[0], vn_ref[0]], axis=0)
    lane = lax.broadcasted_iota(jnp.int32, (A_BLOCK, LANES), 1)
    in_lo = lane < HEAD_DIM
    v_lo = lax.broadcasted_iota(jnp.int32, vall.shape, 1) < HEAD_DIM
    one = jnp.ones_like(vall)
    v_sum = (jnp.where(v_lo, vall, one), jnp.where(v_lo, one, vall))
    group = A_HEADS // A_KV_HEADS
    n_heads = 2 * group
    pair_out = []
    for h0 in range(0, n_heads, WIN_BATCH):
        scores = []
        for hl in range(h0, h0 + WIN_BATCH):
            p, half = hl // 2, hl % 2
            kv_half = hl // group
            qp = q_ref[0, :, p * LANES:(p + 1) * LANES].astype(F32) * SCALE
            src = qp if half == kv_half else pltpu.roll(qp, HEAD_DIM, axis=1)
            keep = in_lo if kv_half == 0 else jnp.logical_not(in_lo)
            qm = jnp.where(keep, src, 0.0).astype(BF16)
            scores.append(_dot_nt(qm, kall))
        probs = []
        for j, s in enumerate(scores):
            s = s + bias_ref[0, h0 + j]
            m = jnp.max(s, axis=-1, keepdims=True)
            probs.append(jnp.exp(s - m).astype(BF16))
        res = []
        for j, e in enumerate(probs):
            hl = h0 + j
            o = _dot(e, v_sum[hl // group])
            o = o / pltpu.roll(o, HEAD_DIM, axis=1)
            if hl % 2 != hl // group:
                o = pltpu.roll(o, HEAD_DIM, axis=1)
            res.append(o)
        for j in range(0, WIN_BATCH, 2):
            pair_out.append(jnp.where(in_lo, res[j], res[j + 1]))
    o_ref[0] = jnp.concatenate(pair_out, axis=1).astype(o_ref.dtype)


def _window_attention(u, kmeta, vmeta, bias, bsz, seq):
    nblk = seq // A_BLOCK
    n_pairs = A_KV_HEADS // 2
    qw = A_WIDTH // n_pairs

    def variant(n):
        return jnp.where(n == 0, 0, jnp.where(n == nblk - 1, 2, 1))

    def kv_spec(col0, shift):
        return pl.BlockSpec(
            (1, A_BLOCK, LANES),
            lambda b, j, n: (b, jnp.clip(n + shift, 0, nblk - 1), col0 + j))

    return pl.pallas_call(
        _window_kernel,
        grid=(bsz, n_pairs, nblk),
        in_specs=[
            pl.BlockSpec((1, A_BLOCK, qw), lambda b, j, n: (b, n, j)),
            kv_spec(COL_KA, -1), kv_spec(COL_KA, 0), kv_spec(COL_KA, 1),
            kv_spec(COL_VA, -1), kv_spec(COL_VA, 0), kv_spec(COL_VA, 1),
            pl.BlockSpec((LANES, LANES), lambda b, j, n: (0, j)),
            pl.BlockSpec((LANES, LANES), lambda b, j, n: (0, j)),
            pl.BlockSpec((1, A_HEADS // n_pairs, A_BLOCK, 4 * LANES),
                         lambda b, j, n: (variant(n), j, 0, 0)),
        ],
        out_specs=pl.BlockSpec((1, A_BLOCK, qw), lambda b, j, n: (b, n, j)),
        out_shape=jax.ShapeDtypeStruct((bsz, seq, A_WIDTH), BF16),
        compiler_params=_cparams(("parallel", "parallel", "arbitrary")),
        name="window_attention",
    )(u, u, u, u, u, u, u, kmeta, vmeta, bias)


def _na_bias(rpb):
    qc = jnp.arange(GRID_W, dtype=jnp.int32)
    kc = jnp.arange(GRID_W, dtype=jnp.int32)
    cs = jnp.clip(qc - NA_COLS // 2, 0, GRID_W - NA_COLS)
    cmask = (kc[None, :] >= cs[:, None]) & (kc[None, :] < cs[:, None] + NA_COLS)
    dc = jnp.clip(kc[None, :] - qc[:, None] + NA_COLS - 1, 0, 2 * NA_COLS - 2)
    n_dc = 2 * NA_COLS - 1
    onehot = (dc[None] == jnp.arange(n_dc, dtype=jnp.int32)[:, None, None]).astype(F32)
    t = jnp.einsum('hrd,dqk->hrqk', rpb.astype(F32), onehot, precision=lax.Precision.HIGHEST)
    t = jnp.where(cmask[None, None], t, NEG)
    variants = []
    for oi in range(NA_ROWS):
        variants.append(jnp.concatenate(
            [t[:, i - oi + NA_ROWS - 1] for i in range(NA_ROWS)], axis=-1))
    return jnp.stack(variants, axis=1)


def _na_kernel(q_ref, k_ref, v_ref, km_ref, vm_ref, bias_ref, o_ref, *, rows):
    span = NA_ROWS * GRID_W
    lane = lax.broadcasted_iota(jnp.int32, (GRID_W, LANES), 1)
    in_lo = lane < HEAD_DIM
    meta_bias = jnp.where(lane < N_META, 0.0, NEG).astype(F32)
    km = km_ref[...]
    vm = vm_ref[...]

    def body(it, carry):
        work = []
        for j in range(NA_UNROLL):
            r = it * NA_UNROLL + j
            rs = jnp.clip(r - NA_ROWS // 2, 0, rows - NA_ROWS)
            oi = r - rs
            q0 = pl.multiple_of(r * GRID_W, GRID_W)
            k0 = pl.multiple_of(rs * GRID_W, GRID_W)
            q = q_ref[0, pl.ds(q0, GRID_W), :]
            q = q * jnp.asarray(SCALE, q.dtype)
            ks = k_ref[0, pl.ds(k0, span), :]
            for hl in range(2):
                keep = in_lo if hl == 0 else jnp.logical_not(in_lo)
                qm = jnp.where(keep, q, jnp.zeros_like(q))
                work.append((_dot_nt(qm, ks), _dot_nt(qm, km), hl, oi, k0, q0))
        probs = []
        for sw, sm, hl, oi, k0, q0 in work:
            sw = sw + bias_ref[hl, oi]
            sm = sm + meta_bias
            m = jnp.maximum(jnp.max(sw, axis=-1, keepdims=True),
                            jnp.max(sm, axis=-1, keepdims=True))
            ew = jnp.exp(sw - m)
            em = jnp.exp(sm - m)
            l = jnp.sum(ew, axis=-1, keepdims=True) + jnp.sum(em, axis=-1, keepdims=True)
            probs.append((ew.astype(BF16), em.astype(BF16), l, k0, q0))
        res = []
        for ew, em, l, k0, q0 in probs:
            vs = v_ref[0, pl.ds(k0, span), :]
            res.append((_dot(ew, vs) + _dot(em, vm)) / l)
        for j in range(NA_UNROLL):
            q0 = probs[2 * j][4]
            out = jnp.where(in_lo, res[2 * j], res[2 * j + 1])
            o_ref[0, pl.ds(q0, GRID_W), :] = out.astype(o_ref.dtype)
        return carry

    lax.fori_loop(0, rows // NA_UNROLL, body, 0)


def _neighbourhood_attention(u, kmeta, vmeta, bias, bsz, seq):
    rows = seq // GRID_W
    assert rows >= NA_ROWS and rows % NA_UNROLL == 0
    n_pairs = B_HEADS // 2

    def col_spec(col0):
        return pl.BlockSpec((1, seq, LANES), lambda b, j: (b, 0, col0 + j))

    return pl.pallas_call(
        functools.partial(_na_kernel, rows=rows),
        grid=(bsz, n_pairs),
        in_specs=[
            col_spec(COL_QB), col_spec(COL_KB), col_spec(COL_VB),
            pl.BlockSpec((LANES, LANES), lambda b, j: (0, j)),
            pl.BlockSpec((LANES, LANES), lambda b, j: (0, j)),
            pl.BlockSpec((2, NA_ROWS, GRID_W, NA_ROWS * GRID_W), lambda b, j: (j, 0, 0, 0)),
        ],
        out_specs=pl.BlockSpec((1, seq, LANES), lambda b, j: (b, 0, j)),
        out_shape=jax.ShapeDtypeStruct((bsz, seq, B_WIDTH), BF16),
        compiler_params=_cparams(("parallel", "parallel")),
        name="neighbourhood_attention",
    )(u, u, u, kmeta, vmeta, bias)


def _route(scores, rbias):
    t = scores.shape[-1]
    ninf = -jnp.inf
    biased = scores + rbias
    b3 = biased.reshape(N_GROUPS, GROUP_SIZE, t)
    s3 = scores.reshape(N_GROUPS, GROUP_SIZE, t)
    io_in = lax.broadcasted_iota(jnp.int32, b3.shape, 1)
    io_g3 = lax.broadcasted_iota(jnp.int32, b3.shape, 0)
    io_e = io_g3 * GROUP_SIZE + io_in
    m1 = jnp.max(b3, axis=1, keepdims=True)
    i1 = jnp.min(jnp.where(b3 == m1, io_in, GROUP_SIZE), axis=1, keepdims=True)
    m2 = jnp.max(jnp.where(io_in == i1, ninf, b3), axis=1, keepdims=True)
    gs = m1 + m2
    io_g = lax.broadcasted_iota(jnp.int32, gs.shape, 0)
    gmask = jnp.zeros(gs.shape, jnp.bool_)
    cur = gs
    for _ in range(TOPK_GROUPS):
        m = jnp.max(cur, axis=0, keepdims=True)
        i = jnp.min(jnp.where(cur == m, io_g, N_GROUPS), axis=0, keepdims=True)
        pick = io_g == i
        gmask = jnp.logical_or(gmask, pick)
        cur = jnp.where(pick, ninf, cur)
    cur = jnp.where(gmask, b3, ninf)
    picks, top_e, top_s = [], [], []
    for _ in range(TOP_K):
        m = jnp.max(jnp.max(cur, axis=1, keepdims=True), axis=0, keepdims=True)
        i = jnp.min(jnp.min(jnp.where(cur == m, io_e, N_EXPERTS), axis=1, keepdims=True),
                    axis=0, keepdims=True)
        pick = io_e == i
        picks.append(pick)
        top_e.append(i.reshape(1, t))
        w = jnp.sum(jnp.sum(jnp.where(pick, s3, 0.0), axis=1, keepdims=True), axis=0, keepdims=True)
        top_s.append(w.reshape(1, t))
        cur = jnp.where(pick, ninf, cur)
    top_e = jnp.concatenate(top_e, axis=0)
    top_s = jnp.concatenate(top_s, axis=0)
    denom = top_s[0:1]
    for k in range(1, TOP_K):
        denom = denom + top_s[k:k + 1]
    top_w = top_s / (denom + 1e-20) * ROUTED_SCALE
    sel = picks[0]
    for k in range(1, TOP_K):
        sel = jnp.logical_or(sel, picks[k])
    sel = jnp.where(sel, 1.0, 0.0).astype(F32).reshape(N_EXPERTS, t)
    return top_e, top_w, sel, picks


def _outproj_router_kernel(ya_ref, yb_ref, x_ref, wo_ref, ga_ref, gb_ref, eg_ref, eb_ref,
                           g1_ref, b1_ref, wrh_ref, wrl_ref, rb_ref,
                           h1_ref, te_ref, tw_ref, tr_ref, cnt_ref, carry):
    tm = x_ref.shape[0]

    @pl.when(pl.program_id(0) == 0)
    def _():
        carry[...] = jnp.zeros_like(carry)

    def rms(y_ref, g_ref):
        y = y_ref[...].astype(F32)
        inv = lax.rsqrt(jnp.mean(y * y, axis=-1, keepdims=True) + RMS_EPS)
        return (y * inv * g_ref[...]).astype(BF16)

    mix = _dot(jnp.concatenate([rms(ya_ref, ga_ref), rms(yb_ref, gb_ref)], axis=1), wo_ref[...])
    h = _layer_norm(x_ref[...], eg_ref[...], eb_ref[...])
    h1 = _layer_norm(DEEPNORM_ALPHA * h + mix, g1_ref[...], b1_ref[...])
    h1_ref[...] = h1
    hb = h1.astype(BF16)

    hlo = (h1 - hb.astype(F32)).astype(BF16)
    logits = (_dot_nt(wrh_ref[...], hb) + _dot_nt(wrh_ref[...], hlo)
              + _dot_nt(wrl_ref[...], hb))
    scores = 1.0 / (1.0 + jnp.exp(-logits))
    top_e, top_w, sel, picks = _route(scores, rb_ref[...])

    row = lax.broadcasted_iota(jnp.int32, (tm, tm), 0)
    col = lax.broadcasted_iota(jnp.int32, (tm, tm), 1)
    before = jnp.where(row < col, 1.0, 0.0).astype(BF16)
    rank = _dot(sel.astype(BF16), before) + carry[...]
    rank3 = rank.reshape(N_GROUPS, GROUP_SIZE, tm)
    ranks = []
    for k in range(TOP_K):
        rk = jnp.sum(jnp.sum(jnp.where(picks[k], rank3, 0.0), axis=1, keepdims=True),
                     axis=0, keepdims=True)
        ranks.append(rk.reshape(1, tm))
    te_ref[...] = top_e
    tw_ref[...] = top_w
    tr_ref[...] = jnp.concatenate(ranks, axis=0).astype(jnp.int32)
    carry[...] = carry[...] + jnp.sum(sel, axis=-1, keepdims=True)
    cnt_ref[...] = jnp.broadcast_to(carry[...], cnt_ref.shape)


def _outproj_router(ya, yb, x2, wo_bf16, ga, gb, eg, eb, g1, b1, wrh, wrl, rb, tm):
    m = x2.shape[0]
    row = lambda w: pl.BlockSpec((tm, w), lambda i: (i, 0))
    full = lambda a: pl.BlockSpec(a.shape, lambda i: (0,) * a.ndim)
    tok = pl.BlockSpec((TOP_K, tm), lambda i: (0, i))
    return pl.pallas_call(
        _outproj_router_kernel,
        grid=(m // tm,),
        in_specs=[row(A_WIDTH), row(B_WIDTH), row(D_MODEL), full(wo_bf16), full(ga), full(gb),
                  full(eg), full(eb), full(g1), full(b1), full(wrh), full(wrl), full(rb)],
        out_specs=[row(D_MODEL), tok, tok, tok,
                   pl.BlockSpec((N_EXPERTS, LANES), lambda i: (0, 0))],
        out_shape=[jax.ShapeDtypeStruct((m, D_MODEL), F32),
                   jax.ShapeDtypeStruct((TOP_K, m), jnp.int32),
                   jax.ShapeDtypeStruct((TOP_K, m), F32),
                   jax.ShapeDtypeStruct((TOP_K, m), jnp.int32),
                   jax.ShapeDtypeStruct((N_EXPERTS, LANES), F32)],
        scratch_shapes=[pltpu.VMEM((N_EXPERTS, 1), F32)],
        compiler_params=_cparams(("arbitrary",)),
        name="outproj_router",
    )(ya, yb, x2, wo_bf16, ga, gb, eg, eb, g1, b1, wrh, wrl, rb)


def _zero_fill_padding(pad_start_ref, pad_len_ref, n_used, xs_hbm, zeros, sem):
    n_blocks = xs_hbm.shape[0] // MOE_BLOCK
    zeros[...] = jnp.zeros_like(zeros)

    def copies(act):
        def per_expert(e, carry):
            start = pad_start_ref[e]
            n = pad_len_ref[e]
            head = jnp.minimum((-start) & (SUBLANES - 1), n)
            for j in range(SUBLANES - 1):
                @pl.when(j < head)
                def _(j=j):
                    act(pltpu.make_async_copy(zeros.at[pl.ds(0, 1), :],
                                              xs_hbm.at[pl.ds(start + j, 1), :], sem.at[1]))
            aligned = start + head
            rest = n - head
            bit = MOE_BLOCK // 2
            while bit >= SUBLANES:
                @pl.when((rest & bit) != 0)
                def _(bit=bit):
                    off = pl.multiple_of(aligned + (rest & (-2 * bit)), SUBLANES)
                    act(pltpu.make_async_copy(zeros.at[pl.ds(0, bit), :],
                                              xs_hbm.at[pl.ds(off, bit), :], sem.at[1]))
                bit //= 2
            return carry

        def per_block(j, carry):
            off = pl.multiple_of(j * MOE_BLOCK, MOE_BLOCK)
            act(pltpu.make_async_copy(zeros, xs_hbm.at[pl.ds(off, MOE_BLOCK), :], sem.at[1]))
            return carry

        lax.fori_loop(0, N_EXPERTS, per_expert, 0)
        lax.fori_loop(n_used, n_blocks, per_block, 0)

    copies(lambda cp: cp.start())
    copies(lambda cp: cp.wait())


def _dispatch_kernel(pad_start_ref, pad_len_ref, nu_ref, dest_ref, h1_ref, wsg_ref, wsu_ref,
                     wsd_ref, home0_ref, wg_ref, wu_ref, wd_ref,
                     xs_hbm, sh_ref, home_ref, wgb_ref, wub_ref, wdb_ref,
                     packed, zeros, sem, *, m):
    i = pl.program_id(0)
    tm = h1_ref.shape[0]
    lane = lax.broadcasted_iota(jnp.int32, (1, LANES), 1)

    @pl.when(i == 0)
    def _():
        home_ref[...] = home0_ref[...]
        _zero_fill_padding(pad_start_ref, pad_len_ref, nu_ref[0], xs_hbm, zeros, sem)

    wgb_ref[...] = wg_ref[...].astype(BF16)
    wub_ref[...] = wu_ref[...].astype(BF16)
    wdb_ref[...] = wd_ref[...].astype(BF16)

    h1 = h1_ref[...]
    packed[...] = _pack_rows(h1)
    for t in range(tm):
        for k in range(TOP_K):
            slot = dest_ref[k, t]
            pltpu.store(home_ref.at[pl.ds(lax.shift_right_logical(slot, LANE_BITS), 1), :],
                        jnp.full((1, LANES), k * m + i * tm + t, jnp.int32),
                        mask=lane == (slot & (LANES - 1)))
            pltpu.make_async_copy(packed.at[pl.ds(t, 1), :],
                                  xs_hbm.at[pl.ds(slot, 1), :], sem.at[0]).start()
    hb = h1.astype(BF16)
    g = _dot(hb, wsg_ref[...])
    u = _dot(hb, wsu_ref[...])
    sh_ref[...] = _dot((_silu(g) * u).astype(BF16), wsd_ref[...]).astype(sh_ref.dtype)
    for k in range(TOP_K):
        pltpu.make_async_copy(packed, xs_hbm.at[pl.ds(0, tm), :], sem.at[0]).wait()


def _dispatch_shared(pad_start, pad_len, n_used, dest, h1, wsg, wsu, wsd, home0,
                     w_gate, w_up, w_down):
    m = h1.shape[0]
    tm = DISPATCH_TILE
    p_rows = home0.size
    full = lambda a: pl.BlockSpec(a.shape, lambda i, *_: (0,) * a.ndim)
    steps = m // tm
    bands = max(1, steps // N_EXPERTS)
    per_step = max(1, N_EXPERTS // steps)
    assert steps * per_step == N_EXPERTS * bands
    share = lambda a: pl.BlockSpec((per_step, a.shape[1] // bands, a.shape[2]),
                                   lambda i, *_: (i // bands, i % bands, 0))
    as_bf16 = lambda a: jax.ShapeDtypeStruct(a.shape, BF16)
    return pl.pallas_call(
        functools.partial(_dispatch_kernel, m=m),
        grid_spec=pltpu.PrefetchScalarGridSpec(
            num_scalar_prefetch=3,
            grid=(m // tm,),
            in_specs=[pl.BlockSpec((TOP_K, tm), lambda i, *_: (0, i), memory_space=pltpu.SMEM),
                      pl.BlockSpec((tm, D_MODEL), lambda i, *_: (i, 0)),
                      full(wsg), full(wsu), full(wsd), full(home0),
                      share(w_gate), share(w_up), share(w_down)],
            out_specs=[pl.BlockSpec(memory_space=pl.ANY),
                       pl.BlockSpec((tm, D_MODEL), lambda i, *_: (i, 0)),
                       full(home0), share(w_gate), share(w_up), share(w_down)],
            scratch_shapes=[pltpu.VMEM((tm, D_HALF), U32),
                            pltpu.VMEM((MOE_BLOCK, D_HALF), U32),
                            pltpu.SemaphoreType.DMA((2,))],
        ),
        out_shape=[jax.ShapeDtypeStruct((p_rows, D_HALF), U32),
                   jax.ShapeDtypeStruct((m, D_MODEL), BF16),
                   jax.ShapeDtypeStruct(home0.shape, jnp.int32),
                   as_bf16(w_gate), as_bf16(w_up), as_bf16(w_down)],
        compiler_params=pltpu.CompilerParams(
            dimension_semantics=("arbitrary",), vmem_limit_bytes=VMEM_LIMIT,
            disable_bounds_checks=True),
        name="dispatch_shared",
    )(pad_start, pad_len, n_used, dest, h1, wsg, wsu, wsd, home0, w_gate, w_up, w_down)


def _moe_kernel(be_ref, half_ref, nu_ref, home_prev_ref, x_ref, wg_ref, wu_ref, wd_ref, y_hbm,
                ybuf0, ybuf1, sem, wg_b, wu_b, wd_b):
    i = pl.program_id(0)
    n_used = nu_ref[0]
    e = be_ref[i]
    prev = be_ref[jnp.maximum(i - 1, 0)]
    even = i % 2 == 0
    bufs = (ybuf0, ybuf1)
    spare0 = y_hbm.shape[0] - 2 * MOE_BLOCK

    @pl.when(i == 0)
    def _():
        ybuf0[...] = jnp.zeros_like(ybuf0)
        ybuf1[...] = jnp.zeros_like(ybuf1)
        for b in range(2):
            cp = pltpu.make_async_copy(
                ybuf1, y_hbm.at[pl.ds(spare0 + b * MOE_BLOCK, MOE_BLOCK), :], sem.at[1])
            cp.start()
            cp.wait()

    @pl.when(jnp.logical_and(i < n_used, jnp.logical_or(i == 0, e != prev)))
    def _():
        wg_b[...] = wg_ref[0]
        wu_b[...] = wu_ref[0]
        wd_b[...] = wd_ref[0]

    def scatter_rows(b):
        for r in range(MOE_BLOCK):
            pltpu.make_async_copy(bufs[b].at[pl.ds(r, 1), :],
                                  y_hbm.at[pl.ds(home_prev_ref[0, 0, r], 1), :],
                                  sem.at[b]).start()

    def wait_rows(b):
        pltpu.make_async_copy(bufs[b], y_hbm.at[pl.ds(0, MOE_BLOCK), :], sem.at[b]).wait()

    def compute(b, rows):
        hi, lo = _unpack_rows(x_ref[0:rows, :])
        g = _dot(hi, wg_b[0:D_HALF, :]) + _dot(lo, wg_b[D_HALF:D_MODEL, :])
        u = _dot(hi, wu_b[0:D_HALF, :]) + _dot(lo, wu_b[D_HALF:D_MODEL, :])
        bufs[b][0:rows, :] = _pack_rows(_dot((_silu(g) * u).astype(BF16), wd_b[...]))

    half = half_ref[i] != 0
    for b in range(2):
        mine = even if b == 0 else jnp.logical_not(even)

        @pl.when(mine & (i >= 2) & (i <= n_used + 1))
        def _():
            wait_rows(b)

        for rows, this in ((MOE_BLOCK, jnp.logical_not(half)), (MOE_BLOCK // 2, half)):
            @pl.when(mine & this & (i >= 1) & (i < n_used))
            def _(rows=rows):
                scatter_rows(1 - b)
                compute(b, rows)

        @pl.when(mine & (i >= 1) & (i == n_used))
        def _():
            scatter_rows(1 - b)

    for rows, this in ((MOE_BLOCK, jnp.logical_not(half)), (MOE_BLOCK // 2, half)):
        @pl.when(this & (i == 0) & (n_used > 0))
        def _(rows=rows):
            compute(0, rows)


def _moe_experts(block_e, half, n_used, home, x_sorted, w_gate, w_up, w_down, m):
    n_blocks = x_sorted.shape[0] // MOE_BLOCK
    assert block_e.shape[0] == n_blocks + 2 and half.shape == block_e.shape
    used = lambda i, be, hf, nu: (jnp.minimum(i, nu[0] - 1), 0)
    weights = lambda shape: pl.BlockSpec(
        (1,) + shape, lambda i, be, hf, nu: (be[jnp.minimum(i, nu[0] - 1)], 0, 0))
    return pl.pallas_call(
        _moe_kernel,
        grid_spec=pltpu.PrefetchScalarGridSpec(
            num_scalar_prefetch=3,
            grid=(n_blocks + 2,),
            in_specs=[
                pl.BlockSpec((1, 1, MOE_BLOCK),
                             lambda i, be, hf, nu: (jnp.clip(i - 1, 0, n_blocks - 1), 0, 0),
                             memory_space=pltpu.SMEM),
                pl.BlockSpec((MOE_BLOCK, D_HALF), used),
                weights((D_MODEL, D_EXPERT)), weights((D_MODEL, D_EXPERT)),
                weights((D_EXPERT, D_MODEL)),
            ],
            out_specs=pl.BlockSpec(memory_space=pl.ANY),
            scratch_shapes=[pltpu.VMEM((MOE_BLOCK, D_HALF), U32),
                            pltpu.VMEM((MOE_BLOCK, D_HALF), U32),
                            pltpu.SemaphoreType.DMA((2,)),
                            pltpu.VMEM((D_MODEL, D_EXPERT), BF16),
                            pltpu.VMEM((D_MODEL, D_EXPERT), BF16),
                            pltpu.VMEM((D_EXPERT, D_MODEL), BF16)],
        ),
        out_shape=jax.ShapeDtypeStruct((TOP_K * m + 2 * MOE_BLOCK, D_HALF), U32),
        compiler_params=pltpu.CompilerParams(
            dimension_semantics=("arbitrary",), vmem_limit_bytes=VMEM_LIMIT,
            disable_bounds_checks=True),
        name="moe_experts",
    )(block_e, half, n_used, home.reshape(n_blocks, 1, MOE_BLOCK), x_sorted, w_gate, w_up, w_down)


def _final_kernel(h1_ref, sh_ref, *rest):
    ys_refs = rest[:TOP_K]
    tw_ref, g2_ref, b2_ref, o_ref = rest[TOP_K:]
    sh = sh_ref[...].astype(F32)
    left, right = sh[:, :D_HALF], sh[:, D_HALF:]
    for k in range(TOP_K):
        p = ys_refs[k][...]
        w = tw_ref[:, k:k + 1]
        left = left + lax.bitcast_convert_type(p & jnp.uint32(0xFFFF0000), F32) * w
        right = right + lax.bitcast_convert_type(p << 16, F32) * w
    ffn = jnp.concatenate([left, right], axis=1)
    o_ref[...] = _layer_norm(DEEPNORM_ALPHA * h1_ref[...] + ffn, g2_ref[...], b2_ref[...])


def _final(h1, sh, y_home, tw_t, g2, b2, tm):
    m = h1.shape[0]
    row = lambda w: pl.BlockSpec((tm, w), lambda i: (i, 0))
    full = lambda a: pl.BlockSpec(a.shape, lambda i: (0,) * a.ndim)
    choice = lambda k: pl.BlockSpec((tm, D_HALF), lambda i: (k * (m // tm) + i, 0))
    return pl.pallas_call(
        _final_kernel,
        grid=(m // tm,),
        in_specs=[row(D_MODEL), row(D_MODEL)] + [choice(k) for k in range(TOP_K)]
                 + [row(TOP_K), full(g2), full(b2)],
        out_specs=row(D_MODEL),
        out_shape=jax.ShapeDtypeStruct((m, D_MODEL), F32),
        compiler_params=_cparams(("parallel",)),
        name="combine_ln2",
    )(h1, sh, *([y_home] * TOP_K), tw_t, g2, b2)


def _pad_rows(a, rows):
    return jnp.concatenate([a, jnp.zeros((rows - a.shape[0],) + a.shape[1:], a.dtype)], axis=0)


def kernel(x, meta_tokens, ln_emb_g, ln_emb_b, t5_table, w_in, a_sink, na_rpb, g_norm_a, g_norm_b, w_out, ln1_g, ln1_b, w_router, router_bias, w_gate, w_up, w_down, ws_gate, ws_up, ws_down, ln2_g, ln2_b):
    bsz, seq, _ = x.shape
    m = bsz * seq
    r2 = lambda a: a.reshape(1, -1).astype(F32)
    x2 = x.reshape(m, D_MODEL)
    eg, eb = r2(ln_emb_g), r2(ln_emb_b)

    w_in_b = w_in[0].astype(BF16)
    tm = 1024 if m % 1024 == 0 else 128
    u = _ln_inproj(x2, eg, eb, w_in_b, tm, 1152).reshape(bsz, seq, IN_WIDTH)
    um = _ln_inproj(meta_tokens.astype(F32), eg, eb, w_in_b, N_META, 1152)
    cut = lambda c0, width: _pad_rows(um[:, c0 * LANES:c0 * LANES + width], LANES)

    ya = _window_attention(u, cut(COL_KA, KV_WIDTH), cut(COL_VA, KV_WIDTH),
                           _window_bias(t5_table, a_sink[0], seq), bsz, seq)
    yb = _neighbourhood_attention(u, cut(COL_KB, B_WIDTH), cut(COL_VB, B_WIDTH),
                                  _na_bias(na_rpb[0]), bsz, seq)

    wr_t = w_router[0].astype(F32).T
    wr_hi = wr_t.astype(BF16)
    wr_lo = (wr_t - wr_hi.astype(F32)).astype(BF16)
    tm4 = 256 if m % 256 == 0 else 128
    h1, top_e, top_w, top_r, cnt = _outproj_router(
        ya.reshape(m, A_WIDTH), yb.reshape(m, B_WIDTH), x2, w_out[0].astype(BF16),
        r2(g_norm_a), r2(g_norm_b), eg, eb, r2(ln1_g), r2(ln1_b), wr_hi, wr_lo,
        router_bias[0].astype(F32).reshape(N_EXPERTS, 1), tm4)

    counts = cnt[:, 0].astype(jnp.int32)
    nb_e = (counts + MOE_BLOCK - 1) // MOE_BLOCK
    bend = jnp.cumsum(nb_e)
    pstart = (bend - nb_e) * MOE_BLOCK
    n_blocks = (m * TOP_K) // MOE_BLOCK + N_EXPERTS
    p_rows = n_blocks * MOE_BLOCK
    blk = jnp.arange(n_blocks + 2, dtype=jnp.int32)
    block_e = jnp.minimum(jnp.sum((bend[None, :] <= blk[:, None]).astype(jnp.int32), axis=1),
                          N_EXPERTS - 1)
    n_used = bend[-1:].astype(jnp.int32)
    mine = block_e[:, None] == jnp.arange(N_EXPERTS, dtype=jnp.int32)[None, :]
    rows_used = jnp.sum(jnp.where(
        mine, counts[None, :] - (blk[:, None] - (bend - nb_e)[None, :]) * MOE_BLOCK, 0), axis=1)
    half = (rows_used <= MOE_BLOCK // 2).astype(jnp.int32)
    expert_ids = jnp.arange(N_EXPERTS, dtype=jnp.int32)[:, None, None]
    dest = top_r + jnp.sum(jnp.where(top_e[None] == expert_ids, pstart[:, None, None], 0),
                           axis=0)
    slot = jnp.arange(p_rows, dtype=jnp.int32)
    home0 = (TOP_K * m + (slot // MOE_BLOCK % 2) * MOE_BLOCK + slot % MOE_BLOCK
             ).reshape(p_rows // LANES, LANES)

    x_sorted, shared, home, wg_b, wu_b, wd_b = _dispatch_shared(
        pstart + counts, nb_e * MOE_BLOCK - counts, n_used, dest, h1,
        ws_gate[0].astype(BF16), ws_up[0].astype(BF16), ws_down[0].astype(BF16), home0,
        w_gate[0], w_up[0], w_down[0])
    y_home = _moe_experts(block_e, half, n_used, home, x_sorted, wg_b, wu_b, wd_b, m)

    out = _final(h1, shared, y_home, top_w.T, r2(ln2_g), r2(ln2_b), tm4)
    return out.reshape(bsz, seq, D_MODEL)
```

```python
import functools
import math

import jax
import jax.numpy as jnp
from jax import lax
from jax.experimental import pallas as pl
from jax.experimental.pallas import tpu as pltpu

F32 = jnp.float32
BF16 = jnp.bfloat16

D_MODEL = 2048
HEAD_DIM = 64
N_META = 16
GRID_W = 64
A_HEADS = 16
A_KV_HEADS = 4
A_WINDOW = 128
A_BLOCK = 128
N_BUCKETS = 32
MAX_DISTANCE = 128
B_HEADS = 16
NA_ROWS = 8
NA_COLS = 16
A_WIDTH = A_HEADS * HEAD_DIM
KV_WIDTH = A_KV_HEADS * HEAD_DIM
B_WIDTH = B_HEADS * HEAD_DIM
MIX_WIDTH = A_WIDTH + B_WIDTH
IN_WIDTH = A_WIDTH + 2 * KV_WIDTH + 3 * B_WIDTH
N_EXPERTS = 64
TOP_K = 8
N_GROUPS = 8
GROUP_SIZE = N_EXPERTS // N_GROUPS
TOPK_GROUPS = 4
D_EXPERT = 512
D_SHARED = 512
ROUTED_SCALE = 2.5
DEPTH = 1
DEEPNORM_ALPHA = (2 * DEPTH) ** 0.25
LN_EPS = 1e-5
RMS_EPS = 1e-6
NEG = -1e30
SCALE = HEAD_DIM ** -0.5

LANES = 128
LANE_BITS = LANES.bit_length() - 1
SUBLANES = 8
VMEM_LIMIT = 56 * 1024 * 1024

COL_QA = 0
COL_KA = A_WIDTH // LANES
COL_VA = (A_WIDTH + KV_WIDTH) // LANES
COL_QB = (A_WIDTH + 2 * KV_WIDTH) // LANES
COL_KB = COL_QB + B_WIDTH // LANES
COL_VB = COL_KB + B_WIDTH // LANES

WIN_BATCH = 8
NA_UNROLL = 8
ROUTE_TILE = 256
MOE_BLOCK = 512
DISPATCH_TILE = 128
D_HALF = D_MODEL // 2
U32 = jnp.uint32


def _pack_rows(x):
    hi = lax.bitcast_convert_type(x[:, :D_HALF].astype(jnp.bfloat16).astype(F32), U32)
    lo = lax.bitcast_convert_type(x[:, D_HALF:].astype(jnp.bfloat16).astype(F32), U32)
    return hi | (lo >> 16)


def _unpack_rows(p):
    hi = lax.bitcast_convert_type(p & jnp.uint32(0xFFFF0000), F32).astype(BF16)
    lo = lax.bitcast_convert_type(p << 16, F32).astype(BF16)
    return hi, lo


def _cparams(sem):
    return pltpu.CompilerParams(dimension_semantics=sem, vmem_limit_bytes=VMEM_LIMIT)


def _layer_norm(x, g, b):
    mu = jnp.mean(x, axis=-1, keepdims=True)
    xc = x - mu
    var = jnp.mean(xc * xc, axis=-1, keepdims=True)
    return xc * lax.rsqrt(var + LN_EPS) * g + b


def _dot(a, b):
    return jnp.dot(a, b, preferred_element_type=F32)


def _dot_nt(a, b):
    return lax.dot_general(a, b, (((1,), (1,)), ((), ())), preferred_element_type=F32)


def _silu(g):
    return g / (1.0 + jnp.exp(-g))


def _ln_inproj_kernel(x_ref, g_ref, b_ref, w_ref, o_ref, h_scr):
    @pl.when(pl.program_id(1) == 0)
    def _():
        h_scr[...] = _layer_norm(x_ref[...], g_ref[...], b_ref[...]).astype(BF16)

    o_ref[...] = _dot(h_scr[...], w_ref[...]).astype(o_ref.dtype)


def _ln_inproj(x2, g, b, w_bf16, tm, tn):
    m = x2.shape[0]
    n = w_bf16.shape[1]
    return pl.pallas_call(
        _ln_inproj_kernel,
        grid=(m // tm, n // tn),
        in_specs=[
            pl.BlockSpec((tm, D_MODEL), lambda i, j: (i, 0)),
            pl.BlockSpec((1, D_MODEL), lambda i, j: (0, 0)),
            pl.BlockSpec((1, D_MODEL), lambda i, j: (0, 0)),
            pl.BlockSpec((D_MODEL, tn), lambda i, j: (0, j)),
        ],
        out_specs=pl.BlockSpec((tm, tn), lambda i, j: (i, j)),
        out_shape=jax.ShapeDtypeStruct((m, n), BF16),
        scratch_shapes=[pltpu.VMEM((tm, D_MODEL), BF16)],
        compiler_params=_cparams(("parallel", "arbitrary")),
        name="ln_inproj",
    )(x2, g, b, w_bf16)


def _t5_bucket(rel):
    nb = N_BUCKETS // 2
    max_exact = nb // 2
    ret = jnp.where(rel > 0, nb, 0)
    n = jnp.abs(rel)
    nf = jnp.maximum(n, 1).astype(F32)
    large = max_exact + (jnp.log(nf / max_exact) / math.log(MAX_DISTANCE / max_exact)
                         * (nb - max_exact)).astype(jnp.int32)
    large = jnp.minimum(large, nb - 1)
    return ret + jnp.where(n < max_exact, n, large)


def _lookup(table_t, idx, n):
    onehot = (idx[None] == jnp.arange(n, dtype=jnp.int32).reshape((n,) + (1,) * idx.ndim))
    return jnp.einsum('hb,b...->h...', table_t, onehot.astype(F32),
                      precision=lax.Precision.HIGHEST)


def _window_bias(t5_table, a_sink, seq):
    assert N_META + A_BLOCK - (N_META - 1) > MAX_DISTANCE
    nblk = seq // A_BLOCK
    t5_t = t5_table.astype(F32).T
    q_loc = jnp.arange(A_BLOCK, dtype=jnp.int32)
    k_loc = jnp.arange(3 * A_BLOCK, dtype=jnp.int32) - A_BLOCK
    rel = k_loc[None, :] - q_loc[:, None]
    band = _lookup(t5_t, _t5_bucket(rel), N_BUCKETS)
    win = jnp.abs(rel) <= A_WINDOW
    sink = jnp.broadcast_to(a_sink.astype(F32)[:, None, None], (A_HEADS, A_BLOCK, 1))
    pad = jnp.full((A_HEADS, A_BLOCK, LANES - N_META - 1), NEG, F32)
    out = []
    for blk in (0, min(1, nblk - 1), nblk - 1):
        gk = blk * A_BLOCK + k_loc
        valid = win & (gk >= 0)[None, :] & (gk < seq)[None, :]
        band_v = jnp.where(valid[None], band, NEG)
        q_pos = N_META + blk * A_BLOCK + q_loc
        rel_m = jnp.arange(N_META, dtype=jnp.int32)[None, :] - q_pos[:, None]
        bias_m = _lookup(t5_t, _t5_bucket(rel_m), N_BUCKETS)
        out.append(jnp.concatenate([bias_m, sink, pad, band_v], axis=-1))
    return jnp.stack(out, axis=0)


def _window_kernel(q_ref, kp_ref, kc_ref, kn_ref, vp_ref, vc_ref, vn_ref, km_ref, vm_ref,
                   bias_ref, o_ref):
    kall = jnp.concatenate([km_ref[...], kp_ref[0], kc_ref[0], kn_ref[0]], axis=0)
    vall = jnp.concatenate([vm_ref[...], vp_ref[0], vc_ref[0], vn_ref[0]], axis=0)
    lane = lax.broadcasted_iota(jnp.int32, (A_BLOCK, LANES), 1)
    in_lo = lane < HEAD_DIM
    v_lo = lax.broadcasted_iota(jnp.int32, vall.shape, 1) < HEAD_DIM
    one = jnp.ones_like(vall)
    v_sum = (jnp.where(v_lo, vall, one), jnp.where(v_lo, one, vall))
    group = A_HEADS // A_KV_HEADS
    n_heads = 2 * group
    pair_out = []
    for h0 in range(0, n_heads, WIN_BATCH):
        scores = []
        for hl in range(h0, h0 + WIN_BATCH):
            p, half = hl // 2, hl % 2
            kv_half = hl // group
            qp = q_ref[0, :, p * LANES:(p + 1) * LANES].astype(F32) * SCALE
            src = qp if half == kv_half else pltpu.roll(qp, HEAD_DIM, axis=1)
            keep = in_lo if kv_half == 0 else jnp.logical_not(in_lo)
            qm = jnp.where(keep, src, 0.0).astype(BF16)
            scores.append(_dot_nt(qm, kall))
        probs = []
        for j, s in enumerate(scores):
            s = s + bias_ref[0, h0 + j]
            m = jnp.max(s, axis=-1, keepdims=True)
            probs.append(jnp.exp(s - m).astype(BF16))
        res = []
        for j, e in enumerate(probs):
            hl = h0 + j
            o = _dot(e, v_sum[hl // group])
            o = o / pltpu.roll(o, HEAD_DIM, axis=1)
            if hl % 2 != hl // group:
                o = pltpu.roll(o, HEAD_DIM, axis=1)
            res.append(o)
        for j in range(0, WIN_BATCH, 2):
            pair_out.append(jnp.where(in_lo, res[j], res[j + 1]))
    o_ref[0] = jnp.concatenate(pair_out, axis=1).astype(o_ref.dtype)


def _window_attention(u, kmeta, vmeta, bias, bsz, seq):
    nblk = seq // A_BLOCK
    n_pairs = A_KV_HEADS // 2
    qw = A_WIDTH // n_pairs

    def variant(n):
        return jnp.where(n == 0, 0, jnp.where(n == nblk - 1, 2, 1))

    def kv_spec(col0, shift):
        return pl.BlockSpec(
            (1, A_BLOCK, LANES),
            lambda b, j, n: (b, jnp.clip(n + shift, 0, nblk - 1), col0 + j))

    return pl.pallas_call(
        _window_kernel,
        grid=(bsz, n_pairs, nblk),
        in_specs=[
            pl.BlockSpec((1, A_BLOCK, qw), lambda b, j, n: (b, n, j)),
            kv_spec(COL_KA, -1), kv_spec(COL_KA, 0), kv_spec(COL_KA, 1),
            kv_spec(COL_VA, -1), kv_spec(COL_VA, 0), kv_spec(COL_VA, 1),
            pl.BlockSpec((LANES, LANES), lambda b, j, n: (0, j)),
            pl.BlockSpec((LANES, LANES), lambda b, j, n: (0, j)),
            pl.BlockSpec((1, A_HEADS // n_pairs, A_BLOCK, 4 * LANES),
                         lambda b, j, n: (variant(n), j, 0, 0)),
        ],
        out_specs=pl.BlockSpec((1, A_BLOCK, qw), lambda b, j, n: (b, n, j)),
        out_shape=jax.ShapeDtypeStruct((bsz, seq, A_WIDTH), BF16),
        compiler_params=_cparams(("parallel", "parallel", "arbitrary")),
        name="window_attention",
    )(u, u, u, u, u, u, u, kmeta, vmeta, bias)


def _na_bias(rpb):
    qc = jnp.arange(GRID_W, dtype=jnp.int32)
    kc = jnp.arange(GRID_W, dtype=jnp.int32)
    cs = jnp.clip(qc - NA_COLS // 2, 0, GRID_W - NA_COLS)
    cmask = (kc[None, :] >= cs[:, None]) & (kc[None, :] < cs[:, None] + NA_COLS)
    dc = jnp.clip(kc[None, :] - qc[:, None] + NA_COLS - 1, 0, 2 * NA_COLS - 2)
    n_dc = 2 * NA_COLS - 1
    onehot = (dc[None] == jnp.arange(n_dc, dtype=jnp.int32)[:, None, None]).astype(F32)
    t = jnp.einsum('hrd,dqk->hrqk', rpb.astype(F32), onehot, precision=lax.Precision.HIGHEST)
    t = jnp.where(cmask[None, None], t, NEG)
    variants = []
    for oi in range(NA_ROWS):
        variants.append(jnp.concatenate(
            [t[:, i - oi + NA_ROWS - 1] for i in range(NA_ROWS)], axis=-1))
    return jnp.stack(variants, axis=1)


def _na_kernel(q_ref, k_ref, v_ref, km_ref, vm_ref, bias_ref, o_ref, *, rows):
    span = NA_ROWS * GRID_W
    lane = lax.broadcasted_iota(jnp.int32, (GRID_W, LANES), 1)
    in_lo = lane < HEAD_DIM
    meta_bias = jnp.where(lane < N_META, 0.0, NEG).astype(F32)
    km = km_ref[...]
    vm = vm_ref[...]

    def body(it, carry):
        work = []
        for j in range(NA_UNROLL):
            r = it * NA_UNROLL + j
            rs = jnp.clip(r - NA_ROWS // 2, 0, rows - NA_ROWS)
            oi = r - rs
            q0 = pl.multiple_of(r * GRID_W, GRID_W)
            k0 = pl.multiple_of(rs * GRID_W, GRID_W)
            q = q_ref[0, pl.ds(q0, GRID_W), :]
            q = q * jnp.asarray(SCALE, q.dtype)
            ks = k_ref[0, pl.ds(k0, span), :]
            for hl in range(2):
                keep = in_lo if hl == 0 else jnp.logical_not(in_lo)
                qm = jnp.where(keep, q, jnp.zeros_like(q))
                work.append((_dot_nt(qm, ks), _dot_nt(qm, km), hl, oi, k0, q0))
        probs = []
        for sw, sm, hl, oi, k0, q0 in work:
            sw = sw + bias_ref[hl, oi]
            sm = sm + meta_bias
            m = jnp.maximum(jnp.max(sw, axis=-1, keepdims=True),
                            jnp.max(sm, axis=-1, keepdims=True))
            ew = jnp.exp(sw - m)
            em = jnp.exp(sm - m)
            l = jnp.sum(ew, axis=-1, keepdims=True) + jnp.sum(em, axis=-1, keepdims=True)
            probs.append((ew.astype(BF16), em.astype(BF16), l, k0, q0))
        res = []
        for ew, em, l, k0, q0 in probs:
            vs = v_ref[0, pl.ds(k0, span), :]
            res.append((_dot(ew, vs) + _dot(em, vm)) / l)
        for j in range(NA_UNROLL):
            q0 = probs[2 * j][4]
            out = jnp.where(in_lo, res[2 * j], res[2 * j + 1])
            o_ref[0, pl.ds(q0, GRID_W), :] = out.astype(o_ref.dtype)
        return carry

    lax.fori_loop(0, rows // NA_UNROLL, body, 0)


def _neighbourhood_attention(u, kmeta, vmeta, bias, bsz, seq):
    rows = seq // GRID_W
    assert rows >= NA_ROWS and rows % NA_UNROLL == 0
    n_pairs = B_HEADS // 2

    def col_spec(col0):
        return pl.BlockSpec((1, seq, LANES), lambda b, j: (b, 0, col0 + j))

    return pl.pallas_call(
        functools.partial(_na_kernel, rows=rows),
        grid=(bsz, n_pairs),
        in_specs=[
            col_spec(COL_QB), col_spec(COL_KB), col_spec(COL_VB),
            pl.BlockSpec((LANES, LANES), lambda b, j: (0, j)),
            pl.BlockSpec((LANES, LANES), lambda b, j: (0, j)),
            pl.BlockSpec((2, NA_ROWS, GRID_W, NA_ROWS * GRID_W), lambda b, j: (j, 0, 0, 0)),
        ],
        out_specs=pl.BlockSpec((1, seq, LANES), lambda b, j: (b, 0, j)),
        out_shape=jax.ShapeDtypeStruct((bsz, seq, B_WIDTH), BF16),
        compiler_params=_cparams(("parallel", "parallel")),
        name="neighbourhood_attention",
    )(u, u, u, kmeta, vmeta, bias)


def _route(scores, rbias):
    t = scores.shape[-1]
    ninf = -jnp.inf
    biased = scores + rbias
    b3 = biased.reshape(N_GROUPS, GROUP_SIZE, t)
    s3 = scores.reshape(N_GROUPS, GROUP_SIZE, t)
    io_in = lax.broadcasted_iota(jnp.int32, b3.shape, 1)
    io_g3 = lax.broadcasted_iota(jnp.int32, b3.shape, 0)
    io_e = io_g3 * GROUP_SIZE + io_in
    m1 = jnp.max(b3, axis=1, keepdims=True)
    i1 = jnp.min(jnp.where(b3 == m1, io_in, GROUP_SIZE), axis=1, keepdims=True)
    m2 = jnp.max(jnp.where(io_in == i1, ninf, b3), axis=1, keepdims=True)
    gs = m1 + m2
    io_g = lax.broadcasted_iota(jnp.int32, gs.shape, 0)
    gmask = jnp.zeros(gs.shape, jnp.bool_)
    cur = gs
    for _ in range(TOPK_GROUPS):
        m = jnp.max(cur, axis=0, keepdims=True)
        i = jnp.min(jnp.where(cur == m, io_g, N_GROUPS), axis=0, keepdims=True)
        pick = io_g == i
        gmask = jnp.logical_or(gmask, pick)
        cur = jnp.where(pick, ninf, cur)
    cur = jnp.where(gmask, b3, ninf)
    picks, top_e, top_s = [], [], []
    for _ in range(TOP_K):
        m = jnp.max(jnp.max(cur, axis=1, keepdims=True), axis=0, keepdims=True)
        i = jnp.min(jnp.min(jnp.where(cur == m, io_e, N_EXPERTS), axis=1, keepdims=True),
                    axis=0, keepdims=True)
        pick = io_e == i
        picks.append(pick)
        top_e.append(i.reshape(1, t))
        w = jnp.sum(jnp.sum(jnp.where(pick, s3, 0.0), axis=1, keepdims=True), axis=0, keepdims=True)
        top_s.append(w.reshape(1, t))
        cur = jnp.where(pick, ninf, cur)
    top_e = jnp.concatenate(top_e, axis=0)
    top_s = jnp.concatenate(top_s, axis=0)
    denom = top_s[0:1]
    for k in range(1, TOP_K):
        denom = denom + top_s[k:k + 1]
    top_w = top_s / (denom + 1e-20) * ROUTED_SCALE
    sel = picks[0]
    for k in range(1, TOP_K):
        sel = jnp.logical_or(sel, picks[k])
    sel = jnp.where(sel, 1.0, 0.0).astype(F32).reshape(N_EXPERTS, t)
    return top_e, top_w, sel, picks


def _outproj_router_kernel(ya_ref, yb_ref, x_ref, wo_ref, ga_ref, gb_ref, eg_ref, eb_ref,
                           g1_ref, b1_ref, wrh_ref, wrl_ref, rb_ref,
                           h1_ref, te_ref, tw_ref, tr_ref, cnt_ref, carry):
    sub = ROUTE_TILE
    parts = [pl.ds(j * sub, sub) for j in range(x_ref.shape[0] // sub)]

    @pl.when(pl.program_id(0) == 0)
    def _():
        carry[...] = jnp.zeros_like(carry)

    def rms(y_ref, g_ref, rows):
        y = y_ref[rows, :].astype(F32)
        inv = lax.rsqrt(jnp.mean(y * y, axis=-1, keepdims=True) + RMS_EPS)
        return (y * inv * g_ref[...]).astype(BF16)

    normed = [jnp.concatenate([rms(ya_ref, ga_ref, r), rms(yb_ref, gb_ref, r)], axis=1)
              for r in parts]
    mixes = [_dot(n, wo_ref[...]) for n in normed]
    logits = []
    for rows, mix in zip(parts, mixes):
        h = _layer_norm(x_ref[rows, :], eg_ref[...], eb_ref[...])
        h1 = _layer_norm(DEEPNORM_ALPHA * h + mix, g1_ref[...], b1_ref[...])
        h1_ref[rows, :] = h1
        hb = h1.astype(BF16)
        hlo = (h1 - hb.astype(F32)).astype(BF16)
        logits.append(_dot_nt(wrh_ref[...], hb) + _dot_nt(wrh_ref[...], hlo)
                      + _dot_nt(wrl_ref[...], hb))

    row = lax.broadcasted_iota(jnp.int32, (sub, sub), 0)
    col = lax.broadcasted_iota(jnp.int32, (sub, sub), 1)
    before = jnp.where(row < col, 1.0, 0.0).astype(BF16)
    running = carry[...]
    for j, lg in enumerate(logits):
        scores = 1.0 / (1.0 + jnp.exp(-lg))
        top_e, top_w, sel, picks = _route(scores, rb_ref[...])
        rank = _dot(sel.astype(BF16), before) + running
        rank3 = rank.reshape(N_GROUPS, GROUP_SIZE, sub)
        ranks = []
        for k in range(TOP_K):
            rk = jnp.sum(jnp.sum(jnp.where(picks[k], rank3, 0.0), axis=1, keepdims=True),
                         axis=0, keepdims=True)
            ranks.append(rk.reshape(1, sub))
        cols = pl.ds(j * sub, sub)
        te_ref[:, cols] = top_e
        tw_ref[:, cols] = top_w
        tr_ref[:, cols] = jnp.concatenate(ranks, axis=0).astype(jnp.int32)
        running = running + jnp.sum(sel, axis=-1, keepdims=True)
    carry[...] = running
    cnt_ref[...] = jnp.broadcast_to(running, cnt_ref.shape)


def _outproj_router(ya, yb, x2, wo_bf16, ga, gb, eg, eb, g1, b1, wrh, wrl, rb, tm):
    m = x2.shape[0]
    row = lambda w: pl.BlockSpec((tm, w), lambda i: (i, 0))
    full = lambda a: pl.BlockSpec(a.shape, lambda i: (0,) * a.ndim)
    tok = pl.BlockSpec((TOP_K, tm), lambda i: (0, i))
    return pl.pallas_call(
        _outproj_router_kernel,
        grid=(m // tm,),
        in_specs=[row(A_WIDTH), row(B_WIDTH), row(D_MODEL), full(wo_bf16), full(ga), full(gb),
                  full(eg), full(eb), full(g1), full(b1), full(wrh), full(wrl), full(rb)],
        out_specs=[row(D_MODEL), tok, tok, tok,
                   pl.BlockSpec((N_EXPERTS, LANES), lambda i: (0, 0))],
        out_shape=[jax.ShapeDtypeStruct((m, D_MODEL), F32),
                   jax.ShapeDtypeStruct((TOP_K, m), jnp.int32),
                   jax.ShapeDtypeStruct((TOP_K, m), F32),
                   jax.ShapeDtypeStruct((TOP_K, m), jnp.int32),
                   jax.ShapeDtypeStruct((N_EXPERTS, LANES), F32)],
        scratch_shapes=[pltpu.VMEM((N_EXPERTS, 1), F32)],
        compiler_params=_cparams(("arbitrary",)),
        name="outproj_router",
    )(ya, yb, x2, wo_bf16, ga, gb, eg, eb, g1, b1, wrh, wrl, rb)


def _zero_fill_padding(pad_start_ref, pad_len_ref, n_used, xs_hbm, zeros, sem):
    n_blocks = xs_hbm.shape[0] // MOE_BLOCK
    zeros[...] = jnp.zeros_like(zeros)

    def copies(act):
        def per_expert(e, carry):
            start = pad_start_ref[e]
            n = pad_len_ref[e]
            head = jnp.minimum((-start) & (SUBLANES - 1), n)
            for j in range(SUBLANES - 1):
                @pl.when(j < head)
                def _(j=j):
                    act(pltpu.make_async_copy(zeros.at[pl.ds(0, 1), :],
                                              xs_hbm.at[pl.ds(start + j, 1), :], sem.at[1]))
            aligned = start + head
            rest = n - head
            bit = MOE_BLOCK // 2
            while bit >= SUBLANES:
                @pl.when((rest & bit) != 0)
                def _(bit=bit):
                    off = pl.multiple_of(aligned + (rest & (-2 * bit)), SUBLANES)
                    act(pltpu.make_async_copy(zeros.at[pl.ds(0, bit), :],
                                              xs_hbm.at[pl.ds(off, bit), :], sem.at[1]))
                bit //= 2
            return carry

        def per_block(j, carry):
            off = pl.multiple_of(j * MOE_BLOCK, MOE_BLOCK)
            act(pltpu.make_async_copy(zeros, xs_hbm.at[pl.ds(off, MOE_BLOCK), :], sem.at[1]))
            return carry

        lax.fori_loop(0, N_EXPERTS, per_expert, 0)
        lax.fori_loop(n_used, n_blocks, per_block, 0)

    copies(lambda cp: cp.start())
    copies(lambda cp: cp.wait())


def _dispatch_kernel(pad_start_ref, pad_len_ref, nu_ref, dest_ref, h1_ref, wsg_ref, wsu_ref,
                     wsd_ref, home0_ref, wg_ref, wu_ref, wd_ref,
                     xs_hbm, sh_ref, home_ref, wgb_ref, wub_ref, wdb_ref,
                     packed, zeros, sem, *, m):
    i = pl.program_id(0)
    tm = h1_ref.shape[0]
    lane = lax.broadcasted_iota(jnp.int32, (1, LANES), 1)

    @pl.when(i == 0)
    def _():
        home_ref[...] = home0_ref[...]
        _zero_fill_padding(pad_start_ref, pad_len_ref, nu_ref[0], xs_hbm, zeros, sem)

    wgb_ref[...] = wg_ref[...].astype(BF16)
    wub_ref[...] = wu_ref[...].astype(BF16)
    wdb_ref[...] = wd_ref[...].astype(BF16)

    h1 = h1_ref[...]
    packed[...] = _pack_rows(h1)
    for t in range(tm):
        for k in range(TOP_K):
            slot = dest_ref[k, t]
            pltpu.store(home_ref.at[pl.ds(lax.shift_right_logical(slot, LANE_BITS), 1), :],
                        jnp.full((1, LANES), k * m + i * tm + t, jnp.int32),
                        mask=lane == (slot & (LANES - 1)))
            pltpu.make_async_copy(packed.at[pl.ds(t, 1), :],
                                  xs_hbm.at[pl.ds(slot, 1), :], sem.at[0]).start()
    hb = h1.astype(BF16)
    g = _dot(hb, wsg_ref[...])
    u = _dot(hb, wsu_ref[...])
    sh_ref[...] = _dot((_silu(g) * u).astype(BF16), wsd_ref[...]).astype(sh_ref.dtype)
    for k in range(TOP_K):
        pltpu.make_async_copy(packed, xs_hbm.at[pl.ds(0, tm), :], sem.at[0]).wait()


def _dispatch_shared(pad_start, pad_len, n_used, dest, h1, wsg, wsu, wsd, home0,
                     w_gate, w_up, w_down):
    m = h1.shape[0]
    tm = DISPATCH_TILE
    p_rows = home0.size
    full = lambda a: pl.BlockSpec(a.shape, lambda i, *_: (0,) * a.ndim)
    steps = m // tm
    bands = max(1, steps // N_EXPERTS)
    per_step = max(1, N_EXPERTS // steps)
    assert steps * per_step == N_EXPERTS * bands
    share = lambda a: pl.BlockSpec((per_step, a.shape[1] // bands, a.shape[2]),
                                   lambda i, *_: (i // bands, i % bands, 0))
    as_bf16 = lambda a: jax.ShapeDtypeStruct(a.shape, BF16)
    return pl.pallas_call(
        functools.partial(_dispatch_kernel, m=m),
        grid_spec=pltpu.PrefetchScalarGridSpec(
            num_scalar_prefetch=3,
            grid=(m // tm,),
            in_specs=[pl.BlockSpec((TOP_K, tm), lambda i, *_: (0, i), memory_space=pltpu.SMEM),
                      pl.BlockSpec((tm, D_MODEL), lambda i, *_: (i, 0)),
                      full(wsg), full(wsu), full(wsd), full(home0),
                      share(w_gate), share(w_up), share(w_down)],
            out_specs=[pl.BlockSpec(memory_space=pl.ANY),
                       pl.BlockSpec((tm, D_MODEL), lambda i, *_: (i, 0)),
                       full(home0), share(w_gate), share(w_up), share(w_down)],
            scratch_shapes=[pltpu.VMEM((tm, D_HALF), U32),
                            pltpu.VMEM((MOE_BLOCK, D_HALF), U32),
                            pltpu.SemaphoreType.DMA((2,))],
        ),
        out_shape=[jax.ShapeDtypeStruct((p_rows, D_HALF), U32),
                   jax.ShapeDtypeStruct((m, D_MODEL), BF16),
                   jax.ShapeDtypeStruct(home0.shape, jnp.int32),
                   as_bf16(w_gate), as_bf16(w_up), as_bf16(w_down)],
        compiler_params=pltpu.CompilerParams(
            dimension_semantics=("arbitrary",), vmem_limit_bytes=VMEM_LIMIT,
            disable_bounds_checks=True),
        name="dispatch_shared",
    )(pad_start, pad_len, n_used, dest, h1, wsg, wsu, wsd, home0, w_gate, w_up, w_down)


def _moe_kernel(be_ref, nu_ref, home_prev_ref, x_ref, wg_ref, wu_ref, wd_ref, y_hbm,
                ybuf0, ybuf1, sem, wg_b, wu_b, wd_b):
    i = pl.program_id(0)
    n_used = nu_ref[0]
    e = be_ref[i]
    prev = be_ref[jnp.maximum(i - 1, 0)]
    even = i % 2 == 0
    bufs = (ybuf0, ybuf1)
    spare0 = y_hbm.shape[0] - 2 * MOE_BLOCK

    @pl.when(i == 0)
    def _():
        ybuf1[...] = jnp.zeros_like(ybuf1)
        for b in range(2):
            cp = pltpu.make_async_copy(
                ybuf1, y_hbm.at[pl.ds(spare0 + b * MOE_BLOCK, MOE_BLOCK), :], sem.at[1])
            cp.start()
            cp.wait()

    @pl.when(jnp.logical_and(i < n_used, jnp.logical_or(i == 0, e != prev)))
    def _():
        wg_b[...] = wg_ref[0]
        wu_b[...] = wu_ref[0]
        wd_b[...] = wd_ref[0]

    def scatter_rows(b):
        for r in range(MOE_BLOCK):
            pltpu.make_async_copy(bufs[b].at[pl.ds(r, 1), :],
                                  y_hbm.at[pl.ds(home_prev_ref[0, 0, r], 1), :],
                                  sem.at[b]).start()

    def wait_rows(b):
        pltpu.make_async_copy(bufs[b], y_hbm.at[pl.ds(0, MOE_BLOCK), :], sem.at[b]).wait()

    def compute(b):
        hi, lo = _unpack_rows(x_ref[...])
        g = _dot(hi, wg_b[0:D_HALF, :]) + _dot(lo, wg_b[D_HALF:D_MODEL, :])
        u = _dot(hi, wu_b[0:D_HALF, :]) + _dot(lo, wu_b[D_HALF:D_MODEL, :])
        bufs[b][...] = _pack_rows(_dot((_silu(g) * u).astype(BF16), wd_b[...]))

    for b in range(2):
        mine = even if b == 0 else jnp.logical_not(even)

        @pl.when(mine & (i >= 2) & (i <= n_used + 1))
        def _():
            wait_rows(b)

        @pl.when(mine & (i >= 1) & (i < n_used))
        def _():
            scatter_rows(1 - b)
            compute(b)

        @pl.when(mine & (i >= 1) & (i == n_used))
        def _():
            scatter_rows(1 - b)

    @pl.when((i == 0) & (n_used > 0))
    def _():
        compute(0)


def _moe_experts(block_e, n_used, home, x_sorted, w_gate, w_up, w_down, m):
    n_blocks = x_sorted.shape[0] // MOE_BLOCK
    assert block_e.shape[0] == n_blocks + 2
    used = lambda i, be, nu: (jnp.minimum(i, nu[0] - 1), 0)
    weights = lambda shape: pl.BlockSpec(
        (1,) + shape, lambda i, be, nu: (be[jnp.minimum(i, nu[0] - 1)], 0, 0))
    return pl.pallas_call(
        _moe_kernel,
        grid_spec=pltpu.PrefetchScalarGridSpec(
            num_scalar_prefetch=2,
            grid=(n_blocks + 2,),
            in_specs=[
                pl.BlockSpec((1, 1, MOE_BLOCK),
                             lambda i, be, nu: (jnp.clip(i - 1, 0, n_blocks - 1), 0, 0),
                             memory_space=pltpu.SMEM),
                pl.BlockSpec((MOE_BLOCK, D_HALF), used),
                weights((D_MODEL, D_EXPERT)), weights((D_MODEL, D_EXPERT)),
                weights((D_EXPERT, D_MODEL)),
            ],
            out_specs=pl.BlockSpec(memory_space=pl.ANY),
            scratch_shapes=[pltpu.VMEM((MOE_BLOCK, D_HALF), U32),
                            pltpu.VMEM((MOE_BLOCK, D_HALF), U32),
                            pltpu.SemaphoreType.DMA((2,)),
                            pltpu.VMEM((D_MODEL, D_EXPERT), BF16),
                            pltpu.VMEM((D_MODEL, D_EXPERT), BF16),
                            pltpu.VMEM((D_EXPERT, D_MODEL), BF16)],
        ),
        out_shape=jax.ShapeDtypeStruct((TOP_K * m + 2 * MOE_BLOCK, D_HALF), U32),
        compiler_params=pltpu.CompilerParams(
            dimension_semantics=("arbitrary",), vmem_limit_bytes=VMEM_LIMIT,
            disable_bounds_checks=True),
        name="moe_experts",
    )(block_e, n_used, home.reshape(n_blocks, 1, MOE_BLOCK), x_sorted, w_gate, w_up, w_down)


def _final_kernel(h1_ref, sh_ref, *rest):
    ys_refs = rest[:TOP_K]
    tw_ref, g2_ref, b2_ref, o_ref = rest[TOP_K:]
    sh = sh_ref[...].astype(F32)
    left, right = sh[:, :D_HALF], sh[:, D_HALF:]
    for k in range(TOP_K):
        p = ys_refs[k][...]
        w = tw_ref[:, k:k + 1]
        left = left + lax.bitcast_convert_type(p & jnp.uint32(0xFFFF0000), F32) * w
        right = right + lax.bitcast_convert_type(p << 16, F32) * w
    ffn = jnp.concatenate([left, right], axis=1)
    o_ref[...] = _layer_norm(DEEPNORM_ALPHA * h1_ref[...] + ffn, g2_ref[...], b2_ref[...])


def _final(h1, sh, y_home, tw_t, g2, b2, tm):
    m = h1.shape[0]
    row = lambda w: pl.BlockSpec((tm, w), lambda i: (i, 0))
    full = lambda a: pl.BlockSpec(a.shape, lambda i: (0,) * a.ndim)
    choice = lambda k: pl.BlockSpec((tm, D_HALF), lambda i: (k * (m // tm) + i, 0))
    return pl.pallas_call(
        _final_kernel,
        grid=(m // tm,),
        in_specs=[row(D_MODEL), row(D_MODEL)] + [choice(k) for k in range(TOP_K)]
                 + [row(TOP_K), full(g2), full(b2)],
        out_specs=row(D_MODEL),
        out_shape=jax.ShapeDtypeStruct((m, D_MODEL), F32),
        compiler_params=_cparams(("parallel",)),
        name="combine_ln2",
    )(h1, sh, *([y_home] * TOP_K), tw_t, g2, b2)


def _pad_rows(a, rows):
    return jnp.concatenate([a, jnp.zeros((rows - a.shape[0],) + a.shape[1:], a.dtype)], axis=0)


def kernel(x, meta_tokens, ln_emb_g, ln_emb_b, t5_table, w_in, a_sink, na_rpb, g_norm_a, g_norm_b, w_out, ln1_g, ln1_b, w_router, router_bias, w_gate, w_up, w_down, ws_gate, ws_up, ws_down, ln2_g, ln2_b):
    bsz, seq, _ = x.shape
    m = bsz * seq
    r2 = lambda a: a.reshape(1, -1).astype(F32)
    x2 = x.reshape(m, D_MODEL)
    eg, eb = r2(ln_emb_g), r2(ln_emb_b)

    w_in_b = w_in[0].astype(BF16)
    tm = 1024 if m % 1024 == 0 else 128
    u = _ln_inproj(x2, eg, eb, w_in_b, tm, 1152).reshape(bsz, seq, IN_WIDTH)
    um = _ln_inproj(meta_tokens.astype(F32), eg, eb, w_in_b, N_META, 1152)
    cut = lambda c0, width: _pad_rows(um[:, c0 * LANES:c0 * LANES + width], LANES)

    ya = _window_attention(u, cut(COL_KA, KV_WIDTH), cut(COL_VA, KV_WIDTH),
                           _window_bias(t5_table, a_sink[0], seq), bsz, seq)
    yb = _neighbourhood_attention(u, cut(COL_KB, B_WIDTH), cut(COL_VB, B_WIDTH),
                                  _na_bias(na_rpb[0]), bsz, seq)

    wr_t = w_router[0].astype(F32).T
    wr_hi = wr_t.astype(BF16)
    wr_lo = (wr_t - wr_hi.astype(F32)).astype(BF16)
    tm4 = 256 if m % 256 == 0 else 128
    h1, top_e, top_w, top_r, cnt = _outproj_router(
        ya.reshape(m, A_WIDTH), yb.reshape(m, B_WIDTH), x2, w_out[0].astype(BF16),
        r2(g_norm_a), r2(g_norm_b), eg, eb, r2(ln1_g), r2(ln1_b), wr_hi, wr_lo,
        router_bias[0].astype(F32).reshape(N_EXPERTS, 1),
        2 * ROUTE_TILE if m % (2 * ROUTE_TILE) == 0 else ROUTE_TILE)

    counts = cnt[:, 0].astype(jnp.int32)
    nb_e = (counts + MOE_BLOCK - 1) // MOE_BLOCK
    bend = jnp.cumsum(nb_e)
    pstart = (bend - nb_e) * MOE_BLOCK
    n_blocks = (m * TOP_K) // MOE_BLOCK + N_EXPERTS
    p_rows = n_blocks * MOE_BLOCK
    blk = jnp.arange(n_blocks + 2, dtype=jnp.int32)
    block_e = jnp.minimum(jnp.sum((bend[None, :] <= blk[:, None]).astype(jnp.int32), axis=1),
                          N_EXPERTS - 1)
    n_used = bend[-1:].astype(jnp.int32)
    expert_ids = jnp.arange(N_EXPERTS, dtype=jnp.int32)[:, None, None]
    dest = top_r + jnp.sum(jnp.where(top_e[None] == expert_ids, pstart[:, None, None], 0),
                           axis=0)
    slot = jnp.arange(p_rows, dtype=jnp.int32)
    home0 = (TOP_K * m + (slot // MOE_BLOCK % 2) * MOE_BLOCK + slot % MOE_BLOCK
             ).reshape(p_rows // LANES, LANES)

    x_sorted, shared, home, wg_b, wu_b, wd_b = _dispatch_shared(
        pstart + counts, nb_e * MOE_BLOCK - counts, n_used, dest, h1,
        ws_gate[0].astype(BF16), ws_up[0].astype(BF16), ws_down[0].astype(BF16), home0,
        w_gate[0], w_up[0], w_down[0])
    y_home = _moe_experts(block_e, n_used, home, x_sorted, wg_b, wu_b, wd_b, m)

    out = _final(h1, shared, y_home, top_w.T, r2(ln2_g), r2(ln2_b), tm4)
    return out.reshape(bsz, seq, D_MODEL)
```

```python
import functools
import math

import jax
import jax.numpy as jnp
from jax import lax
from jax.experimental import pallas as pl
from jax.experimental.pallas import tpu as pltpu

F32 = jnp.float32
BF16 = jnp.bfloat16

D_MODEL = 2048
HEAD_DIM = 64
N_META = 16
GRID_W = 64
A_HEADS = 16
A_KV_HEADS = 4
A_WINDOW = 128
A_BLOCK = 128
N_BUCKETS = 32
MAX_DISTANCE = 128
B_HEADS = 16
NA_ROWS = 8
NA_COLS = 16
A_WIDTH = A_HEADS * HEAD_DIM
KV_WIDTH = A_KV_HEADS * HEAD_DIM
B_WIDTH = B_HEADS * HEAD_DIM
MIX_WIDTH = A_WIDTH + B_WIDTH
IN_WIDTH = A_WIDTH + 2 * KV_WIDTH + 3 * B_WIDTH
N_EXPERTS = 64
TOP_K = 8
N_GROUPS = 8
GROUP_SIZE = N_EXPERTS // N_GROUPS
TOPK_GROUPS = 4
D_EXPERT = 512
D_SHARED = 512
ROUTED_SCALE = 2.5
DEPTH = 1
DEEPNORM_ALPHA = (2 * DEPTH) ** 0.25
LN_EPS = 1e-5
RMS_EPS = 1e-6
NEG = -1e30
SCALE = HEAD_DIM ** -0.5

LANES = 128
LANE_BITS = LANES.bit_length() - 1
SUBLANES = 8
VMEM_LIMIT = 56 * 1024 * 1024

COL_QA = 0
COL_KA = A_WIDTH // LANES
COL_VA = (A_WIDTH + KV_WIDTH) // LANES
COL_QB = (A_WIDTH + 2 * KV_WIDTH) // LANES
COL_KB = COL_QB + B_WIDTH // LANES
COL_VB = COL_KB + B_WIDTH // LANES

WIN_BATCH = 8
NA_UNROLL = 8
ROUTE_TILE = 256
MOE_BLOCK = 512
DISPATCH_TILE = 128
D_HALF = D_MODEL // 2
U32 = jnp.uint32


def _pack_rows(x):
    hi = lax.bitcast_convert_type(x[:, :D_HALF].astype(jnp.bfloat16).astype(F32), U32)
    lo = lax.bitcast_convert_type(x[:, D_HALF:].astype(jnp.bfloat16).astype(F32), U32)
    return hi | (lo >> 16)


def _unpack_rows(p):
    hi = lax.bitcast_convert_type(p & jnp.uint32(0xFFFF0000), F32).astype(BF16)
    lo = lax.bitcast_convert_type(p << 16, F32).astype(BF16)
    return hi, lo


def _cparams(sem):
    return pltpu.CompilerParams(dimension_semantics=sem, vmem_limit_bytes=VMEM_LIMIT)


def _layer_norm(x, g, b):
    mu = jnp.mean(x, axis=-1, keepdims=True)
    xc = x - mu
    var = jnp.mean(xc * xc, axis=-1, keepdims=True)
    return xc * lax.rsqrt(var + LN_EPS) * g + b


def _dot(a, b):
    return jnp.dot(a, b, preferred_element_type=F32)


def _dot_nt(a, b):
    return lax.dot_general(a, b, (((1,), (1,)), ((), ())), preferred_element_type=F32)


def _silu(g):
    return g / (1.0 + jnp.exp(-g))


def _ln_inproj_kernel(x_ref, g_ref, b_ref, w_ref, o_ref, h_scr):
    @pl.when(pl.program_id(1) == 0)
    def _():
        sub = min(ROUTE_TILE, x_ref.shape[0])
        for j in range(x_ref.shape[0] // sub):
            rows = pl.ds(j * sub, sub)
            h = _layer_norm(x_ref[rows, :], g_ref[...], b_ref[...]).astype(BF16)
            h_scr[rows, :] = h
            o_ref[rows, :] = _dot(h, w_ref[...]).astype(o_ref.dtype)

    @pl.when(pl.program_id(1) != 0)
    def _():
        o_ref[...] = _dot(h_scr[...], w_ref[...]).astype(o_ref.dtype)


def _ln_inproj(x2, g, b, w_bf16, tm, tn):
    m = x2.shape[0]
    n = w_bf16.shape[1]
    return pl.pallas_call(
        _ln_inproj_kernel,
        grid=(m // tm, n // tn),
        in_specs=[
            pl.BlockSpec((tm, D_MODEL), lambda i, j: (i, 0)),
            pl.BlockSpec((1, D_MODEL), lambda i, j: (0, 0)),
            pl.BlockSpec((1, D_MODEL), lambda i, j: (0, 0)),
            pl.BlockSpec((D_MODEL, tn), lambda i, j: (0, j)),
        ],
        out_specs=pl.BlockSpec((tm, tn), lambda i, j: (i, j)),
        out_shape=jax.ShapeDtypeStruct((m, n), BF16),
        scratch_shapes=[pltpu.VMEM((tm, D_MODEL), BF16)],
        compiler_params=_cparams(("parallel", "arbitrary")),
        name="ln_inproj",
    )(x2, g, b, w_bf16)


def _t5_bucket(rel):
    nb = N_BUCKETS // 2
    max_exact = nb // 2
    ret = jnp.where(rel > 0, nb, 0)
    n = jnp.abs(rel)
    nf = jnp.maximum(n, 1).astype(F32)
    large = max_exact + (jnp.log(nf / max_exact) / math.log(MAX_DISTANCE / max_exact)
                         * (nb - max_exact)).astype(jnp.int32)
    large = jnp.minimum(large, nb - 1)
    return ret + jnp.where(n < max_exact, n, large)


def _lookup(table_t, idx, n):
    onehot = (idx[None] == jnp.arange(n, dtype=jnp.int32).reshape((n,) + (1,) * idx.ndim))
    return jnp.einsum('hb,b...->h...', table_t, onehot.astype(F32),
                      precision=lax.Precision.HIGHEST)


def _window_bias(t5_table, a_sink, seq):
    assert N_META + A_BLOCK - (N_META - 1) > MAX_DISTANCE
    nblk = seq // A_BLOCK
    t5_t = t5_table.astype(F32).T
    q_loc = jnp.arange(A_BLOCK, dtype=jnp.int32)
    k_loc = jnp.arange(3 * A_BLOCK, dtype=jnp.int32) - A_BLOCK
    rel = k_loc[None, :] - q_loc[:, None]
    band = _lookup(t5_t, _t5_bucket(rel), N_BUCKETS)
    win = jnp.abs(rel) <= A_WINDOW
    sink = jnp.broadcast_to(a_sink.astype(F32)[:, None, None], (A_HEADS, A_BLOCK, 1))
    pad = jnp.full((A_HEADS, A_BLOCK, LANES - N_META - 1), NEG, F32)
    out = []
    for blk in (0, min(1, nblk - 1), nblk - 1):
        gk = blk * A_BLOCK + k_loc
        valid = win & (gk >= 0)[None, :] & (gk < seq)[None, :]
        band_v = jnp.where(valid[None], band, NEG)
        q_pos = N_META + blk * A_BLOCK + q_loc
        rel_m = jnp.arange(N_META, dtype=jnp.int32)[None, :] - q_pos[:, None]
        bias_m = _lookup(t5_t, _t5_bucket(rel_m), N_BUCKETS)
        out.append(jnp.concatenate([bias_m, sink, pad, band_v], axis=-1))
    return jnp.stack(out, axis=0)


def _window_kernel(q_ref, kp_ref, kc_ref, kn_ref, vp_ref, vc_ref, vn_ref, km_ref, vm_ref,
                   bias_ref, o_ref):
    kall = jnp.concatenate([km_ref[...], kp_ref[0], kc_ref[0], kn_ref[0]], axis=0)
    vall = jnp.concatenate([vm_ref[...], vp_ref[0], vc_ref[0], vn_ref[0]], axis=0)
    lane = lax.broadcasted_iota(jnp.int32, (A_BLOCK, LANES), 1)
    in_lo = lane < HEAD_DIM
    v_lo = lax.broadcasted_iota(jnp.int32, vall.shape, 1) < HEAD_DIM
    one = jnp.ones_like(vall)
    v_sum = (jnp.where(v_lo, vall, one), jnp.where(v_lo, one, vall))
    group = A_HEADS // A_KV_HEADS
    n_heads = 2 * group
    pair_out = []
    for h0 in range(0, n_heads, WIN_BATCH):
        scores = []
        for hl in range(h0, h0 + WIN_BATCH):
            p, half = hl // 2, hl % 2
            kv_half = hl // group
            qp = q_ref[0, :, p * LANES:(p + 1) * LANES].astype(F32) * SCALE
            src = qp if half == kv_half else pltpu.roll(qp, HEAD_DIM, axis=1)
            keep = in_lo if kv_half == 0 else jnp.logical_not(in_lo)
            qm = jnp.where(keep, src, 0.0).astype(BF16)
            scores.append(_dot_nt(qm, kall))
        probs = []
        for j, s in enumerate(scores):
            s = s + bias_ref[0, h0 + j]
            m = jnp.max(s, axis=-1, keepdims=True)
            probs.append(jnp.exp(s - m).astype(BF16))
        res = []
        for j, e in enumerate(probs):
            hl = h0 + j
            o = _dot(e, v_sum[hl // group])
            o = o / pltpu.roll(o, HEAD_DIM, axis=1)
            if hl % 2 != hl // group:
                o = pltpu.roll(o, HEAD_DIM, axis=1)
            res.append(o)
        for j in range(0, WIN_BATCH, 2):
            pair_out.append(jnp.where(in_lo, res[j], res[j + 1]))
    o_ref[0] = jnp.concatenate(pair_out, axis=1).astype(o_ref.dtype)


def _window_attention(u, kmeta, vmeta, bias, bsz, seq):
    nblk = seq // A_BLOCK
    n_pairs = A_KV_HEADS // 2
    qw = A_WIDTH // n_pairs

    def variant(n):
        return jnp.where(n == 0, 0, jnp.where(n == nblk - 1, 2, 1))

    def kv_spec(col0, shift):
        return pl.BlockSpec(
            (1, A_BLOCK, LANES),
            lambda b, j, n: (b, jnp.clip(n + shift, 0, nblk - 1), col0 + j))

    return pl.pallas_call(
        _window_kernel,
        grid=(bsz, n_pairs, nblk),
        in_specs=[
            pl.BlockSpec((1, A_BLOCK, qw), lambda b, j, n: (b, n, j)),
            kv_spec(COL_KA, -1), kv_spec(COL_KA, 0), kv_spec(COL_KA, 1),
            kv_spec(COL_VA, -1), kv_spec(COL_VA, 0), kv_spec(COL_VA, 1),
            pl.BlockSpec((LANES, LANES), lambda b, j, n: (0, j)),
            pl.BlockSpec((LANES, LANES), lambda b, j, n: (0, j)),
            pl.BlockSpec((1, A_HEADS // n_pairs, A_BLOCK, 4 * LANES),
                         lambda b, j, n: (variant(n), j, 0, 0)),
        ],
        out_specs=pl.BlockSpec((1, A_BLOCK, qw), lambda b, j, n: (b, n, j)),
        out_shape=jax.ShapeDtypeStruct((bsz, seq, A_WIDTH), BF16),
        compiler_params=_cparams(("parallel", "parallel", "arbitrary")),
        name="window_attention",
    )(u, u, u, u, u, u, u, kmeta, vmeta, bias)


def _na_bias(rpb):
    qc = jnp.arange(GRID_W, dtype=jnp.int32)
    kc = jnp.arange(GRID_W, dtype=jnp.int32)
    cs = jnp.clip(qc - NA_COLS // 2, 0, GRID_W - NA_COLS)
    cmask = (kc[None, :] >= cs[:, None]) & (kc[None, :] < cs[:, None] + NA_COLS)
    dc = jnp.clip(kc[None, :] - qc[:, None] + NA_COLS - 1, 0, 2 * NA_COLS - 2)
    n_dc = 2 * NA_COLS - 1
    onehot = (dc[None] == jnp.arange(n_dc, dtype=jnp.int32)[:, None, None]).astype(F32)
    t = jnp.einsum('hrd,dqk->hrqk', rpb.astype(F32), onehot, precision=lax.Precision.HIGHEST)
    t = jnp.where(cmask[None, None], t, NEG)
    variants = []
    for oi in range(NA_ROWS):
        variants.append(jnp.concatenate(
            [t[:, i - oi + NA_ROWS - 1] for i in range(NA_ROWS)], axis=-1))
    return jnp.stack(variants, axis=1)


def _na_kernel(q_ref, k_ref, v_ref, km_ref, vm_ref, bias_ref, o_ref, *, rows):
    span = NA_ROWS * GRID_W
    lane = lax.broadcasted_iota(jnp.int32, (GRID_W, LANES), 1)
    in_lo = lane < HEAD_DIM
    meta_bias = jnp.where(lane < N_META, 0.0, NEG).astype(F32)
    km = km_ref[...]
    vm = vm_ref[...]

    def body(it, carry):
        work = []
        for j in range(NA_UNROLL):
            r = it * NA_UNROLL + j
            rs = jnp.clip(r - NA_ROWS // 2, 0, rows - NA_ROWS)
            oi = r - rs
            q0 = pl.multiple_of(r * GRID_W, GRID_W)
            k0 = pl.multiple_of(rs * GRID_W, GRID_W)
            q = q_ref[0, pl.ds(q0, GRID_W), :]
            q = q * jnp.asarray(SCALE, q.dtype)
            ks = k_ref[0, pl.ds(k0, span), :]
            for hl in range(2):
                keep = in_lo if hl == 0 else jnp.logical_not(in_lo)
                qm = jnp.where(keep, q, jnp.zeros_like(q))
                work.append((_dot_nt(qm, ks), _dot_nt(qm, km), hl, oi, k0, q0))
        probs = []
        for sw, sm, hl, oi, k0, q0 in work:
            sw = sw + bias_ref[hl, oi]
            sm = sm + meta_bias
            m = jnp.maximum(jnp.max(sw, axis=-1, keepdims=True),
                            jnp.max(sm, axis=-1, keepdims=True))
            ew = jnp.exp(sw - m)
            em = jnp.exp(sm - m)
            l = jnp.sum(ew, axis=-1, keepdims=True) + jnp.sum(em, axis=-1, keepdims=True)
            probs.append((ew.astype(BF16), em.astype(BF16), l, k0, q0))
        res = []
        for ew, em, l, k0, q0 in probs:
            vs = v_ref[0, pl.ds(k0, span), :]
            res.append((_dot(ew, vs) + _dot(em, vm)) / l)
        for j in range(NA_UNROLL):
            q0 = probs[2 * j][4]
            out = jnp.where(in_lo, res[2 * j], res[2 * j + 1])
            o_ref[0, pl.ds(q0, GRID_W), :] = out.astype(o_ref.dtype)
        return carry

    lax.fori_loop(0, rows // NA_UNROLL, body, 0)


def _neighbourhood_attention(u, kmeta, vmeta, bias, bsz, seq):
    rows = seq // GRID_W
    assert rows >= NA_ROWS and rows % NA_UNROLL == 0
    n_pairs = B_HEADS // 2

    def col_spec(col0):
        return pl.BlockSpec((1, seq, LANES), lambda b, j: (b, 0, col0 + j))

    return pl.pallas_call(
        functools.partial(_na_kernel, rows=rows),
        grid=(bsz, n_pairs),
        in_specs=[
            col_spec(COL_QB), col_spec(COL_KB), col_spec(COL_VB),
            pl.BlockSpec((LANES, LANES), lambda b, j: (0, j)),
            pl.BlockSpec((LANES, LANES), lambda b, j: (0, j)),
            pl.BlockSpec((2, NA_ROWS, GRID_W, NA_ROWS * GRID_W), lambda b, j: (j, 0, 0, 0)),
        ],
        out_specs=pl.BlockSpec((1, seq, LANES), lambda b, j: (b, 0, j)),
        out_shape=jax.ShapeDtypeStruct((bsz, seq, B_WIDTH), BF16),
        compiler_params=_cparams(("parallel", "parallel")),
        name="neighbourhood_attention",
    )(u, u, u, kmeta, vmeta, bias)


def _route(scores, rbias):
    t = scores.shape[-1]
    ninf = -jnp.inf
    biased = scores + rbias
    b3 = biased.reshape(N_GROUPS, GROUP_SIZE, t)
    s3 = scores.reshape(N_GROUPS, GROUP_SIZE, t)
    io_in = lax.broadcasted_iota(jnp.int32, b3.shape, 1)
    io_g3 = lax.broadcasted_iota(jnp.int32, b3.shape, 0)
    io_e = io_g3 * GROUP_SIZE + io_in
    m1 = jnp.max(b3, axis=1, keepdims=True)
    i1 = jnp.min(jnp.where(b3 == m1, io_in, GROUP_SIZE), axis=1, keepdims=True)
    m2 = jnp.max(jnp.where(io_in == i1, ninf, b3), axis=1, keepdims=True)
    gs = m1 + m2
    io_g = lax.broadcasted_iota(jnp.int32, gs.shape, 0)
    gmask = jnp.zeros(gs.shape, jnp.bool_)
    cur = gs
    for _ in range(TOPK_GROUPS):
        m = jnp.max(cur, axis=0, keepdims=True)
        i = jnp.min(jnp.where(cur == m, io_g, N_GROUPS), axis=0, keepdims=True)
        pick = io_g == i
        gmask = jnp.logical_or(gmask, pick)
        cur = jnp.where(pick, ninf, cur)
    cur = jnp.where(gmask, b3, ninf)
    picks, top_e, top_s = [], [], []
    for _ in range(TOP_K):
        m = jnp.max(jnp.max(cur, axis=1, keepdims=True), axis=0, keepdims=True)
        i = jnp.min(jnp.min(jnp.where(cur == m, io_e, N_EXPERTS), axis=1, keepdims=True),
                    axis=0, keepdims=True)
        pick = io_e == i
        picks.append(pick)
        top_e.append(i.reshape(1, t))
        w = jnp.sum(jnp.sum(jnp.where(pick, s3, 0.0), axis=1, keepdims=True), axis=0, keepdims=True)
        top_s.append(w.reshape(1, t))
        cur = jnp.where(pick, ninf, cur)
    top_e = jnp.concatenate(top_e, axis=0)
    top_s = jnp.concatenate(top_s, axis=0)
    denom = top_s[0:1]
    for k in range(1, TOP_K):
        denom = denom + top_s[k:k + 1]
    top_w = top_s / (denom + 1e-20) * ROUTED_SCALE
    sel = picks[0]
    for k in range(1, TOP_K):
        sel = jnp.logical_or(sel, picks[k])
    sel = jnp.where(sel, 1.0, 0.0).astype(F32).reshape(N_EXPERTS, t)
    return top_e, top_w, sel, picks


def _outproj_router_kernel(ya_ref, yb_ref, x_ref, wo_ref, ga_ref, gb_ref, eg_ref, eb_ref,
                           g1_ref, b1_ref, wrh_ref, wrl_ref, rb_ref,
                           h1_ref, te_ref, tw_ref, tr_ref, cnt_ref, carry):
    sub = ROUTE_TILE
    parts = [pl.ds(j * sub, sub) for j in range(x_ref.shape[0] // sub)]

    @pl.when(pl.program_id(0) == 0)
    def _():
        carry[...] = jnp.zeros_like(carry)

    def rms(y_ref, g_ref, rows):
        y = y_ref[rows, :].astype(F32)
        inv = lax.rsqrt(jnp.mean(y * y, axis=-1, keepdims=True) + RMS_EPS)
        return (y * inv * g_ref[...]).astype(BF16)

    normed = [jnp.concatenate([rms(ya_ref, ga_ref, r), rms(yb_ref, gb_ref, r)], axis=1)
              for r in parts]
    mixes = [_dot(n, wo_ref[...]) for n in normed]
    logits = []
    for rows, mix in zip(parts, mixes):
        h = _layer_norm(x_ref[rows, :], eg_ref[...], eb_ref[...])
        h1 = _layer_norm(DEEPNORM_ALPHA * h + mix, g1_ref[...], b1_ref[...])
        h1_ref[rows, :] = h1
        hb = h1.astype(BF16)
        hlo = (h1 - hb.astype(F32)).astype(BF16)
        logits.append(_dot_nt(wrh_ref[...], hb) + _dot_nt(wrh_ref[...], hlo)
                      + _dot_nt(wrl_ref[...], hb))

    row = lax.broadcasted_iota(jnp.int32, (sub, sub), 0)
    col = lax.broadcasted_iota(jnp.int32, (sub, sub), 1)
    before = jnp.where(row < col, 1.0, 0.0).astype(BF16)
    running = carry[...]
    for j, lg in enumerate(logits):
        scores = 1.0 / (1.0 + jnp.exp(-lg))
        top_e, top_w, sel, picks = _route(scores, rb_ref[...])
        rank = _dot(sel.astype(BF16), before) + running
        rank3 = rank.reshape(N_GROUPS, GROUP_SIZE, sub)
        ranks = []
        for k in range(TOP_K):
            rk = jnp.sum(jnp.sum(jnp.where(picks[k], rank3, 0.0), axis=1, keepdims=True),
                         axis=0, keepdims=True)
            ranks.append(rk.reshape(1, sub))
        cols = pl.ds(j * sub, sub)
        te_ref[:, cols] = top_e
        tw_ref[:, cols] = top_w
        tr_ref[:, cols] = jnp.concatenate(ranks, axis=0).astype(jnp.int32)
        running = running + jnp.sum(sel, axis=-1, keepdims=True)
    carry[...] = running
    cnt_ref[...] = jnp.broadcast_to(running, cnt_ref.shape)


def _outproj_router(ya, yb, x2, wo_bf16, ga, gb, eg, eb, g1, b1, wrh, wrl, rb, tm):
    m = x2.shape[0]
    row = lambda w: pl.BlockSpec((tm, w), lambda i: (i, 0))
    full = lambda a: pl.BlockSpec(a.shape, lambda i: (0,) * a.ndim)
    tok = pl.BlockSpec((TOP_K, tm), lambda i: (0, i))
    return pl.pallas_call(
        _outproj_router_kernel,
        grid=(m // tm,),
        in_specs=[row(A_WIDTH), row(B_WIDTH), row(D_MODEL), full(wo_bf16), full(ga), full(gb),
                  full(eg), full(eb), full(g1), full(b1), full(wrh), full(wrl), full(rb)],
        out_specs=[row(D_MODEL), tok, tok, tok,
                   pl.BlockSpec((N_EXPERTS, LANES), lambda i: (0, 0))],
        out_shape=[jax.ShapeDtypeStruct((m, D_MODEL), F32),
                   jax.ShapeDtypeStruct((TOP_K, m), jnp.int32),
                   jax.ShapeDtypeStruct((TOP_K, m), F32),
                   jax.ShapeDtypeStruct((TOP_K, m), jnp.int32),
                   jax.ShapeDtypeStruct((N_EXPERTS, LANES), F32)],
        scratch_shapes=[pltpu.VMEM((N_EXPERTS, 1), F32)],
        compiler_params=_cparams(("arbitrary",)),
        name="outproj_router",
    )(ya, yb, x2, wo_bf16, ga, gb, eg, eb, g1, b1, wrh, wrl, rb)


def _zero_fill_padding(pad_start_ref, pad_len_ref, n_used, xs_hbm, zeros, sem):
    n_blocks = xs_hbm.shape[0] // MOE_BLOCK
    zeros[...] = jnp.zeros_like(zeros)

    def copies(act):
        def per_expert(e, carry):
            start = pad_start_ref[e]
            n = pad_len_ref[e]
            head = jnp.minimum((-start) & (SUBLANES - 1), n)
            for j in range(SUBLANES - 1):
                @pl.when(j < head)
                def _(j=j):
                    act(pltpu.make_async_copy(zeros.at[pl.ds(0, 1), :],
                                              xs_hbm.at[pl.ds(start + j, 1), :], sem.at[1]))
            aligned = start + head
            rest = n - head
            bit = MOE_BLOCK // 2
            while bit >= SUBLANES:
                @pl.when((rest & bit) != 0)
                def _(bit=bit):
                    off = pl.multiple_of(aligned + (rest & (-2 * bit)), SUBLANES)
                    act(pltpu.make_async_copy(zeros.at[pl.ds(0, bit), :],
                                              xs_hbm.at[pl.ds(off, bit), :], sem.at[1]))
                bit //= 2
            return carry

        def per_block(j, carry):
            off = pl.multiple_of(j * MOE_BLOCK, MOE_BLOCK)
            act(pltpu.make_async_copy(zeros, xs_hbm.at[pl.ds(off, MOE_BLOCK), :], sem.at[1]))
            return carry

        lax.fori_loop(0, N_EXPERTS, per_expert, 0)
        lax.fori_loop(n_used, n_blocks, per_block, 0)

    copies(lambda cp: cp.start())
    copies(lambda cp: cp.wait())


def _dispatch_kernel(pad_start_ref, pad_len_ref, nu_ref, dest_ref, h1_ref, wsg_ref, wsu_ref,
                     wsd_ref, home0_ref, wg_ref, wu_ref, wd_ref,
                     xs_hbm, sh_ref, home_ref, wgb_ref, wub_ref, wdb_ref,
                     packed, zeros, sem, *, m):
    i = pl.program_id(0)
    tm = h1_ref.shape[0]
    lane = lax.broadcasted_iota(jnp.int32, (1, LANES), 1)

    @pl.when(i == 0)
    def _():
        home_ref[...] = home0_ref[...]
        _zero_fill_padding(pad_start_ref, pad_len_ref, nu_ref[0], xs_hbm, zeros, sem)

    wgb_ref[...] = wg_ref[...].astype(BF16)
    wub_ref[...] = wu_ref[...].astype(BF16)
    wdb_ref[...] = wd_ref[...].astype(BF16)

    h1 = h1_ref[...]
    packed[...] = _pack_rows(h1)
    for t in range(tm):
        for k in range(TOP_K):
            slot = dest_ref[k, t]
            pltpu.store(home_ref.at[pl.ds(lax.shift_right_logical(slot, LANE_BITS), 1), :],
                        jnp.full((1, LANES), k * m + i * tm + t, jnp.int32),
                        mask=lane == (slot & (LANES - 1)))
            pltpu.make_async_copy(packed.at[pl.ds(t, 1), :],
                                  xs_hbm.at[pl.ds(slot, 1), :], sem.at[0]).start()
    hb = h1.astype(BF16)
    g = _dot(hb, wsg_ref[...])
    u = _dot(hb, wsu_ref[...])
    sh_ref[...] = _dot((_silu(g) * u).astype(BF16), wsd_ref[...]).astype(sh_ref.dtype)
    for k in range(TOP_K):
        pltpu.make_async_copy(packed, xs_hbm.at[pl.ds(0, tm), :], sem.at[0]).wait()


def _dispatch_shared(pad_start, pad_len, n_used, dest, h1, wsg, wsu, wsd, home0,
                     w_gate, w_up, w_down):
    m = h1.shape[0]
    tm = DISPATCH_TILE
    p_rows = home0.size
    full = lambda a: pl.BlockSpec(a.shape, lambda i, *_: (0,) * a.ndim)
    steps = m // tm
    bands = max(1, steps // N_EXPERTS)
    per_step = max(1, N_EXPERTS // steps)
    assert steps * per_step == N_EXPERTS * bands
    share = lambda a: pl.BlockSpec((per_step, a.shape[1] // bands, a.shape[2]),
                                   lambda i, *_: (i // bands, i % bands, 0))
    as_bf16 = lambda a: jax.ShapeDtypeStruct(a.shape, BF16)
    return pl.pallas_call(
        functools.partial(_dispatch_kernel, m=m),
        grid_spec=pltpu.PrefetchScalarGridSpec(
            num_scalar_prefetch=3,
            grid=(m // tm,),
            in_specs=[pl.BlockSpec((TOP_K, tm), lambda i, *_: (0, i), memory_space=pltpu.SMEM),
                      pl.BlockSpec((tm, D_MODEL), lambda i, *_: (i, 0)),
                      full(wsg), full(wsu), full(wsd), full(home0),
                      share(w_gate), share(w_up), share(w_down)],
            out_specs=[pl.BlockSpec(memory_space=pl.ANY),
                       pl.BlockSpec((tm, D_MODEL), lambda i, *_: (i, 0)),
                       full(home0), share(w_gate), share(w_up), share(w_down)],
            scratch_shapes=[pltpu.VMEM((tm, D_HALF), U32),
                            pltpu.VMEM((MOE_BLOCK, D_HALF), U32),
                            pltpu.SemaphoreType.DMA((2,))],
        ),
        out_shape=[jax.ShapeDtypeStruct((p_rows, D_HALF), U32),
                   jax.ShapeDtypeStruct((m, D_MODEL), BF16),
                   jax.ShapeDtypeStruct(home0.shape, jnp.int32),
                   as_bf16(w_gate), as_bf16(w_up), as_bf16(w_down)],
        compiler_params=pltpu.CompilerParams(
            dimension_semantics=("arbitrary",), vmem_limit_bytes=VMEM_LIMIT,
            disable_bounds_checks=True),
        name="dispatch_shared",
    )(pad_start, pad_len, n_used, dest, h1, wsg, wsu, wsd, home0, w_gate, w_up, w_down)


def _moe_kernel(be_ref, nu_ref, home_prev_ref, x_ref, wg_ref, wu_ref, wd_ref, y_hbm,
                ybuf0, ybuf1, sem, wg_b, wu_b, wd_b):
    i = pl.program_id(0)
    n_used = nu_ref[0]
    e = be_ref[i]
    prev = be_ref[jnp.maximum(i - 1, 0)]
    even = i % 2 == 0
    bufs = (ybuf0, ybuf1)
    spare0 = y_hbm.shape[0] - 2 * MOE_BLOCK

    @pl.when(i == 0)
    def _():
        ybuf1[...] = jnp.zeros_like(ybuf1)
        for b in range(2):
            cp = pltpu.make_async_copy(
                ybuf1, y_hbm.at[pl.ds(spare0 + b * MOE_BLOCK, MOE_BLOCK), :], sem.at[1])
            cp.start()
            cp.wait()

    @pl.when(jnp.logical_and(i < n_used, jnp.logical_or(i == 0, e != prev)))
    def _():
        wg_b[...] = wg_ref[0]
        wu_b[...] = wu_ref[0]
        wd_b[...] = wd_ref[0]

    def scatter_rows(b):
        for r in range(MOE_BLOCK):
            pltpu.make_async_copy(bufs[b].at[pl.ds(r, 1), :],
                                  y_hbm.at[pl.ds(home_prev_ref[0, 0, r], 1), :],
                                  sem.at[b]).start()

    def wait_rows(b):
        pltpu.make_async_copy(bufs[b], y_hbm.at[pl.ds(0, MOE_BLOCK), :], sem.at[b]).wait()

    def compute(b):
        hi, lo = _unpack_rows(x_ref[...])
        g = _dot(hi, wg_b[0:D_HALF, :]) + _dot(lo, wg_b[D_HALF:D_MODEL, :])
        u = _dot(hi, wu_b[0:D_HALF, :]) + _dot(lo, wu_b[D_HALF:D_MODEL, :])
        bufs[b][...] = _pack_rows(_dot((_silu(g) * u).astype(BF16), wd_b[...]))

    for b in range(2):
        mine = even if b == 0 else jnp.logical_not(even)

        @pl.when(mine & (i >= 2) & (i <= n_used + 1))
        def _():
            wait_rows(b)

        @pl.when(mine & (i >= 1) & (i < n_used))
        def _():
            scatter_rows(1 - b)
            compute(b)

        @pl.when(mine & (i >= 1) & (i == n_used))
        def _():
            scatter_rows(1 - b)

    @pl.when((i == 0) & (n_used > 0))
    def _():
        compute(0)


def _moe_experts(block_e, n_used, home, x_sorted, w_gate, w_up, w_down, m):
    n_blocks = x_sorted.shape[0] // MOE_BLOCK
    assert block_e.shape[0] == n_blocks + 2
    used = lambda i, be, nu: (jnp.minimum(i, nu[0] - 1), 0)
    weights = lambda shape: pl.BlockSpec(
        (1,) + shape, lambda i, be, nu: (be[jnp.minimum(i, nu[0] - 1)], 0, 0))
    return pl.pallas_call(
        _moe_kernel,
        grid_spec=pltpu.PrefetchScalarGridSpec(
            num_scalar_prefetch=2,
            grid=(n_blocks + 2,),
            in_specs=[
                pl.BlockSpec((1, 1, MOE_BLOCK),
                             lambda i, be, nu: (jnp.clip(i - 1, 0, n_blocks - 1), 0, 0),
                             memory_space=pltpu.SMEM),
                pl.BlockSpec((MOE_BLOCK, D_HALF), used),
                weights((D_MODEL, D_EXPERT)), weights((D_MODEL, D_EXPERT)),
                weights((D_EXPERT, D_MODEL)),
            ],
            out_specs=pl.BlockSpec(memory_space=pl.ANY),
            scratch_shapes=[pltpu.VMEM((MOE_BLOCK, D_HALF), U32),
                            pltpu.VMEM((MOE_BLOCK, D_HALF), U32),
                            pltpu.SemaphoreType.DMA((2,)),
                            pltpu.VMEM((D_MODEL, D_EXPERT), BF16),
                            pltpu.VMEM((D_MODEL, D_EXPERT), BF16),
                            pltpu.VMEM((D_EXPERT, D_MODEL), BF16)],
        ),
        out_shape=jax.ShapeDtypeStruct((TOP_K * m + 2 * MOE_BLOCK, D_HALF), U32),
        compiler_params=pltpu.CompilerParams(
            dimension_semantics=("arbitrary",), vmem_limit_bytes=VMEM_LIMIT,
            disable_bounds_checks=True),
        name="moe_experts",
    )(block_e, n_used, home.reshape(n_blocks, 1, MOE_BLOCK), x_sorted, w_gate, w_up, w_down)


def _final_kernel(h1_ref, sh_ref, *rest):
    ys_refs = rest[:TOP_K]
    tw_ref, g2_ref, b2_ref, o_ref = rest[TOP_K:]
    sh = sh_ref[...].astype(F32)
    left, right = sh[:, :D_HALF], sh[:, D_HALF:]
    for k in range(TOP_K):
        p = ys_refs[k][...]
        w = tw_ref[:, k:k + 1]
        left = left + lax.bitcast_convert_type(p & jnp.uint32(0xFFFF0000), F32) * w
        right = right + lax.bitcast_convert_type(p << 16, F32) * w
    ffn = jnp.concatenate([left, right], axis=1)
    o_ref[...] = _layer_norm(DEEPNORM_ALPHA * h1_ref[...] + ffn, g2_ref[...], b2_ref[...])


def _final(h1, sh, y_home, tw_t, g2, b2, tm):
    m = h1.shape[0]
    row = lambda w: pl.BlockSpec((tm, w), lambda i: (i, 0))
    full = lambda a: pl.BlockSpec(a.shape, lambda i: (0,) * a.ndim)
    choice = lambda k: pl.BlockSpec((tm, D_HALF), lambda i: (k * (m // tm) + i, 0))
    return pl.pallas_call(
        _final_kernel,
        grid=(m // tm,),
        in_specs=[row(D_MODEL), row(D_MODEL)] + [choice(k) for k in range(TOP_K)]
                 + [row(TOP_K), full(g2), full(b2)],
        out_specs=row(D_MODEL),
        out_shape=jax.ShapeDtypeStruct((m, D_MODEL), F32),
        compiler_params=_cparams(("parallel",)),
        name="combine_ln2",
    )(h1, sh, *([y_home] * TOP_K), tw_t, g2, b2)


def _pad_rows(a, rows):
    return jnp.concatenate([a, jnp.zeros((rows - a.shape[0],) + a.shape[1:], a.dtype)], axis=0)


def kernel(x, meta_tokens, ln_emb_g, ln_emb_b, t5_table, w_in, a_sink, na_rpb, g_norm_a, g_norm_b, w_out, ln1_g, ln1_b, w_router, router_bias, w_gate, w_up, w_down, ws_gate, ws_up, ws_down, ln2_g, ln2_b):
    bsz, seq, _ = x.shape
    m = bsz * seq
    r2 = lambda a: a.reshape(1, -1).astype(F32)
    x2 = x.reshape(m, D_MODEL)
    eg, eb = r2(ln_emb_g), r2(ln_emb_b)

    w_in_b = w_in[0].astype(BF16)
    tm = 1024 if m % 1024 == 0 else 128
    u = _ln_inproj(x2, eg, eb, w_in_b, tm, 1152).reshape(bsz, seq, IN_WIDTH)
    um = _ln_inproj(meta_tokens.astype(F32), eg, eb, w_in_b, N_META, 1152)
    cut = lambda c0, width: _pad_rows(um[:, c0 * LANES:c0 * LANES + width], LANES)

    ya = _window_attention(u, cut(COL_KA, KV_WIDTH), cut(COL_VA, KV_WIDTH),
                           _window_bias(t5_table, a_sink[0], seq), bsz, seq)
    yb = _neighbourhood_attention(u, cut(COL_KB, B_WIDTH), cut(COL_VB, B_WIDTH),
                                  _na_bias(na_rpb[0]), bsz, seq)

    wr_t = w_router[0].astype(F32).T
    wr_hi = wr_t.astype(BF16)
    wr_lo = (wr_t - wr_hi.astype(F32)).astype(BF16)
    tm4 = 256 if m % 256 == 0 else 128
    h1, top_e, top_w, top_r, cnt = _outproj_router(
        ya.reshape(m, A_WIDTH), yb.reshape(m, B_WIDTH), x2, w_out[0].astype(BF16),
        r2(g_norm_a), r2(g_norm_b), eg, eb, r2(ln1_g), r2(ln1_b), wr_hi, wr_lo,
        router_bias[0].astype(F32).reshape(N_EXPERTS, 1),
        2 * ROUTE_TILE if m % (2 * ROUTE_TILE) == 0 else ROUTE_TILE)

    counts = cnt[:, 0].astype(jnp.int32)
    nb_e = (counts + MOE_BLOCK - 1) // MOE_BLOCK
    bend = jnp.cumsum(nb_e)
    pstart = (bend - nb_e) * MOE_BLOCK
    n_blocks = (m * TOP_K) // MOE_BLOCK + N_EXPERTS
    p_rows = n_blocks * MOE_BLOCK
    blk = jnp.arange(n_blocks + 2, dtype=jnp.int32)
    block_e = jnp.minimum(jnp.sum((bend[None, :] <= blk[:, None]).astype(jnp.int32), axis=1),
                          N_EXPERTS - 1)
    n_used = bend[-1:].astype(jnp.int32)
    expert_ids = jnp.arange(N_EXPERTS, dtype=jnp.int32)[:, None, None]
    dest = top_r + jnp.sum(jnp.where(top_e[None] == expert_ids, pstart[:, None, None], 0),
                           axis=0)
    slot = jnp.arange(p_rows, dtype=jnp.int32)
    home0 = (TOP_K * m + (slot // MOE_BLOCK % 2) * MOE_BLOCK + slot % MOE_BLOCK
             ).reshape(p_rows // LANES, LANES)

    x_sorted, shared, home, wg_b, wu_b, wd_b = _dispatch_shared(
        pstart + counts, nb_e * MOE_BLOCK - counts, n_used, dest, h1,
        ws_gate[0].astype(BF16), ws_up[0].astype(BF16), ws_down[0].astype(BF16), home0,
        w_gate[0], w_up[0], w_down[0])
    y_home = _moe_experts(block_e, n_used, home, x_sorted, wg_b, wu_b, wd_b, m)

    out = _final(h1, shared, y_home, top_w.T, r2(ln2_g), r2(ln2_b), tm4)
    return out.reshape(bsz, seq, D_MODEL)
```

```python
import functools
import math

import jax
import jax.numpy as jnp
from jax import lax
from jax.experimental import pallas as pl
from jax.experimental.pallas import tpu as pltpu

F32 = jnp.float32
BF16 = jnp.bfloat16

D_MODEL = 2048
HEAD_DIM = 64
N_META = 16
GRID_W = 64
A_HEADS = 16
A_KV_HEADS = 4
A_WINDOW = 128
A_BLOCK = 128
N_BUCKETS = 32
MAX_DISTANCE = 128
B_HEADS = 16
NA_ROWS = 8
NA_COLS = 16
A_WIDTH = A_HEADS * HEAD_DIM
KV_WIDTH = A_KV_HEADS * HEAD_DIM
B_WIDTH = B_HEADS * HEAD_DIM
MIX_WIDTH = A_WIDTH + B_WIDTH
IN_WIDTH = A_WIDTH + 2 * KV_WIDTH + 3 * B_WIDTH
N_EXPERTS = 64
TOP_K = 8
N_GROUPS = 8
GROUP_SIZE = N_EXPERTS // N_GROUPS
TOPK_GROUPS = 4
D_EXPERT = 512
D_SHARED = 512
ROUTED_SCALE = 2.5
DEPTH = 1
DEEPNORM_ALPHA = (2 * DEPTH) ** 0.25
LN_EPS = 1e-5
RMS_EPS = 1e-6
NEG = -1e30
SCALE = HEAD_DIM ** -0.5

LANES = 128
LANE_BITS = LANES.bit_length() - 1
SUBLANES = 8
VMEM_LIMIT = 56 * 1024 * 1024

COL_QA = 0
COL_KA = A_WIDTH // LANES
COL_VA = (A_WIDTH + KV_WIDTH) // LANES
COL_QB = (A_WIDTH + 2 * KV_WIDTH) // LANES
COL_KB = COL_QB + B_WIDTH // LANES
COL_VB = COL_KB + B_WIDTH // LANES

WIN_BATCH = 8
NA_UNROLL = 8
ROUTE_TILE = 256
MOE_BLOCK = 512
DISPATCH_TILE = 128
D_HALF = D_MODEL // 2
U32 = jnp.uint32


def _pack_rows(x):
    hi = lax.bitcast_convert_type(x[:, :D_HALF].astype(jnp.bfloat16).astype(F32), U32)
    lo = lax.bitcast_convert_type(x[:, D_HALF:].astype(jnp.bfloat16).astype(F32), U32)
    return hi | (lo >> 16)


def _unpack_rows(p):
    hi = lax.bitcast_convert_type(p & jnp.uint32(0xFFFF0000), F32).astype(BF16)
    lo = lax.bitcast_convert_type(p << 16, F32).astype(BF16)
    return hi, lo


def _cparams(sem):
    return pltpu.CompilerParams(dimension_semantics=sem, vmem_limit_bytes=VMEM_LIMIT)


def _layer_norm(x, g, b):
    mu = jnp.mean(x, axis=-1, keepdims=True)
    xc = x - mu
    var = jnp.mean(xc * xc, axis=-1, keepdims=True)
    return xc * lax.rsqrt(var + LN_EPS) * g + b


def _dot(a, b):
    return jnp.dot(a, b, preferred_element_type=F32)


def _dot_nt(a, b):
    return lax.dot_general(a, b, (((1,), (1,)), ((), ())), preferred_element_type=F32)


def _silu(g):
    return g / (1.0 + jnp.exp(-g))


def _ln_inproj_kernel(x_ref, g_ref, b_ref, w_ref, o_ref, h_scr):
    @pl.when(pl.program_id(1) == 0)
    def _():
        sub = min(ROUTE_TILE, x_ref.shape[0])
        for j in range(x_ref.shape[0] // sub):
            rows = pl.ds(j * sub, sub)
            h = _layer_norm(x_ref[rows, :], g_ref[...], b_ref[...]).astype(BF16)
            h_scr[rows, :] = h
            o_ref[rows, :] = _dot(h, w_ref[...]).astype(o_ref.dtype)

    @pl.when(pl.program_id(1) != 0)
    def _():
        o_ref[...] = _dot(h_scr[...], w_ref[...]).astype(o_ref.dtype)


def _ln_inproj(x2, g, b, w_bf16, tm, tn):
    m = x2.shape[0]
    n = w_bf16.shape[1]
    return pl.pallas_call(
        _ln_inproj_kernel,
        grid=(m // tm, n // tn),
        in_specs=[
            pl.BlockSpec((tm, D_MODEL), lambda i, j: (i, 0)),
            pl.BlockSpec((1, D_MODEL), lambda i, j: (0, 0)),
            pl.BlockSpec((1, D_MODEL), lambda i, j: (0, 0)),
            pl.BlockSpec((D_MODEL, tn), lambda i, j: (0, j)),
        ],
        out_specs=pl.BlockSpec((tm, tn), lambda i, j: (i, j)),
        out_shape=jax.ShapeDtypeStruct((m, n), BF16),
        scratch_shapes=[pltpu.VMEM((tm, D_MODEL), BF16)],
        compiler_params=_cparams(("parallel", "arbitrary")),
        name="ln_inproj",
    )(x2, g, b, w_bf16)


def _t5_bucket(rel):
    nb = N_BUCKETS // 2
    max_exact = nb // 2
    ret = jnp.where(rel > 0, nb, 0)
    n = jnp.abs(rel)
    nf = jnp.maximum(n, 1).astype(F32)
    large = max_exact + (jnp.log(nf / max_exact) / math.log(MAX_DISTANCE / max_exact)
                         * (nb - max_exact)).astype(jnp.int32)
    large = jnp.minimum(large, nb - 1)
    return ret + jnp.where(n < max_exact, n, large)


def _lookup(table_t, idx, n):
    onehot = (idx[None] == jnp.arange(n, dtype=jnp.int32).reshape((n,) + (1,) * idx.ndim))
    return jnp.einsum('hb,b...->h...', table_t, onehot.astype(F32),
                      precision=lax.Precision.HIGHEST)


def _window_bias(t5_table, a_sink, seq):
    assert N_META + A_BLOCK - (N_META - 1) > MAX_DISTANCE
    nblk = seq // A_BLOCK
    t5_t = t5_table.astype(F32).T
    q_loc = jnp.arange(A_BLOCK, dtype=jnp.int32)
    k_loc = jnp.arange(3 * A_BLOCK, dtype=jnp.int32) - A_BLOCK
    rel = k_loc[None, :] - q_loc[:, None]
    band = _lookup(t5_t, _t5_bucket(rel), N_BUCKETS)
    win = jnp.abs(rel) <= A_WINDOW
    sink = jnp.broadcast_to(a_sink.astype(F32)[:, None, None], (A_HEADS, A_BLOCK, 1))
    pad = jnp.full((A_HEADS, A_BLOCK, LANES - N_META - 1), NEG, F32)
    out = []
    for blk in (0, min(1, nblk - 1), nblk - 1):
        gk = blk * A_BLOCK + k_loc
        valid = win & (gk >= 0)[None, :] & (gk < seq)[None, :]
        band_v = jnp.where(valid[None], band, NEG)
        q_pos = N_META + blk * A_BLOCK + q_loc
        rel_m = jnp.arange(N_META, dtype=jnp.int32)[None, :] - q_pos[:, None]
        bias_m = _lookup(t5_t, _t5_bucket(rel_m), N_BUCKETS)
        out.append(jnp.concatenate([bias_m, sink, pad, band_v], axis=-1))
    return jnp.stack(out, axis=0)


def _window_kernel(q_ref, kp_ref, kc_ref, kn_ref, vp_ref, vc_ref, vn_ref, km_ref, vm_ref,
                   bias_ref, o_ref):
    kall = jnp.concatenate([km_ref[...], kp_ref[0], kc_ref[0], kn_ref[0]], axis=0)
    vall = jnp.concatenate([vm_ref[...], vp_ref[0], vc_ref[0], vn_ref[0]], axis=0)
    lane = lax.broadcasted_iota(jnp.int32, (A_BLOCK, LANES), 1)
    in_lo = lane < HEAD_DIM
    v_lo = lax.broadcasted_iota(jnp.int32, vall.shape, 1) < HEAD_DIM
    one = jnp.ones_like(vall)
    v_sum = (jnp.where(v_lo, vall, one), jnp.where(v_lo, one, vall))
    group = A_HEADS // A_KV_HEADS
    n_heads = 2 * group
    pair_out = []
    for h0 in range(0, n_heads, WIN_BATCH):
        scores = []
        for hl in range(h0, h0 + WIN_BATCH):
            p, half = hl // 2, hl % 2
            kv_half = hl // group
            qp = q_ref[0, :, p * LANES:(p + 1) * LANES].astype(F32) * SCALE
            src = qp if half == kv_half else pltpu.roll(qp, HEAD_DIM, axis=1)
            keep = in_lo if kv_half == 0 else jnp.logical_not(in_lo)
            qm = jnp.where(keep, src, 0.0).astype(BF16)
            scores.append(_dot_nt(qm, kall))
        probs = []
        for j, s in enumerate(scores):
            s = s + bias_ref[0, h0 + j]
            m = jnp.max(s, axis=-1, keepdims=True)
            probs.append(jnp.exp(s - m).astype(BF16))
        res = []
        for j, e in enumerate(probs):
            hl = h0 + j
            o = _dot(e, v_sum[hl // group])
            o = o / pltpu.roll(o, HEAD_DIM, axis=1)
            if hl % 2 != hl // group:
                o = pltpu.roll(o, HEAD_DIM, axis=1)
            res.append(o)
        for j in range(0, WIN_BATCH, 2):
            pair_out.append(jnp.where(in_lo, res[j], res[j + 1]))
    o_ref[0] = jnp.concatenate(pair_out, axis=1).astype(o_ref.dtype)


def _window_attention(u, kmeta, vmeta, bias, bsz, seq):
    nblk = seq // A_BLOCK
    n_pairs = A_KV_HEADS // 2
    qw = A_WIDTH // n_pairs

    def variant(n):
        return jnp.where(n == 0, 0, jnp.where(n == nblk - 1, 2, 1))

    def kv_spec(col0, shift):
        return pl.BlockSpec(
            (1, A_BLOCK, LANES),
            lambda b, j, n: (b, jnp.clip(n + shift, 0, nblk - 1), col0 + j))

    return pl.pallas_call(
        _window_kernel,
        grid=(bsz, n_pairs, nblk),
        in_specs=[
            pl.BlockSpec((1, A_BLOCK, qw), lambda b, j, n: (b, n, j)),
            kv_spec(COL_KA, -1), kv_spec(COL_KA, 0), kv_spec(COL_KA, 1),
            kv_spec(COL_VA, -1), kv_spec(COL_VA, 0), kv_spec(COL_VA, 1),
            pl.BlockSpec((LANES, LANES), lambda b, j, n: (0, j)),
            pl.BlockSpec((LANES, LANES), lambda b, j, n: (0, j)),
            pl.BlockSpec((1, A_HEADS // n_pairs, A_BLOCK, 4 * LANES),
                         lambda b, j, n: (variant(n), j, 0, 0)),
        ],
        out_specs=pl.BlockSpec((1, A_BLOCK, qw), lambda b, j, n: (b, n, j)),
        out_shape=jax.ShapeDtypeStruct((bsz, seq, A_WIDTH), BF16),
        compiler_params=_cparams(("parallel", "parallel", "arbitrary")),
        name="window_attention",
    )(u, u, u, u, u, u, u, kmeta, vmeta, bias)


def _na_bias(rpb):
    qc = jnp.arange(GRID_W, dtype=jnp.int32)
    kc = jnp.arange(GRID_W, dtype=jnp.int32)
    cs = jnp.clip(qc - NA_COLS // 2, 0, GRID_W - NA_COLS)
    cmask = (kc[None, :] >= cs[:, None]) & (kc[None, :] < cs[:, None] + NA_COLS)
    dc = jnp.clip(kc[None, :] - qc[:, None] + NA_COLS - 1, 0, 2 * NA_COLS - 2)
    n_dc = 2 * NA_COLS - 1
    onehot = (dc[None] == jnp.arange(n_dc, dtype=jnp.int32)[:, None, None]).astype(F32)
    t = jnp.einsum('hrd,dqk->hrqk', rpb.astype(F32), onehot, precision=lax.Precision.HIGHEST)
    t = jnp.where(cmask[None, None], t, NEG)
    variants = []
    for oi in range(NA_ROWS):
        variants.append(jnp.concatenate(
            [t[:, i - oi + NA_ROWS - 1] for i in range(NA_ROWS)], axis=-1))
    return jnp.stack(variants, axis=1)


def _na_kernel(q_ref, k_ref, v_ref, km_ref, vm_ref, bias_ref, o_ref, *, rows):
    span = NA_ROWS * GRID_W
    lane = lax.broadcasted_iota(jnp.int32, (GRID_W, LANES), 1)
    in_lo = lane < HEAD_DIM
    meta_bias = jnp.where(lane < N_META, 0.0, NEG).astype(F32)
    km = km_ref[...]
    vm = vm_ref[...]

    def body(it, carry):
        work = []
        for j in range(NA_UNROLL):
            r = it * NA_UNROLL + j
            rs = jnp.clip(r - NA_ROWS // 2, 0, rows - NA_ROWS)
            oi = r - rs
            q0 = pl.multiple_of(r * GRID_W, GRID_W)
            k0 = pl.multiple_of(rs * GRID_W, GRID_W)
            q = q_ref[0, pl.ds(q0, GRID_W), :]
            q = q * jnp.asarray(SCALE, q.dtype)
            ks = k_ref[0, pl.ds(k0, span), :]
            for hl in range(2):
                keep = in_lo if hl == 0 else jnp.logical_not(in_lo)
                qm = jnp.where(keep, q, jnp.zeros_like(q))
                work.append((_dot_nt(qm, ks), _dot_nt(qm, km), hl, oi, k0, q0))
        probs = []
        for sw, sm, hl, oi, k0, q0 in work:
            sw = sw + bias_ref[hl, oi]
            sm = sm + meta_bias
            m = jnp.maximum(jnp.max(sw, axis=-1, keepdims=True),
                            jnp.max(sm, axis=-1, keepdims=True))
            ew = jnp.exp(sw - m)
            em = jnp.exp(sm - m)
            l = jnp.sum(ew, axis=-1, keepdims=True) + jnp.sum(em, axis=-1, keepdims=True)
            probs.append((ew.astype(BF16), em.astype(BF16), l, k0, q0))
        res = []
        for ew, em, l, k0, q0 in probs:
            vs = v_ref[0, pl.ds(k0, span), :]
            res.append((_dot(ew, vs) + _dot(em, vm)) / l)
        for j in range(NA_UNROLL):
            q0 = probs[2 * j][4]
            out = jnp.where(in_lo, res[2 * j], res[2 * j + 1])
            o_ref[0, pl.ds(q0, GRID_W), :] = out.astype(o_ref.dtype)
        return carry

    lax.fori_loop(0, rows // NA_UNROLL, body, 0)


def _neighbourhood_attention(u, kmeta, vmeta, bias, bsz, seq):
    rows = seq // GRID_W
    assert rows >= NA_ROWS and rows % NA_UNROLL == 0
    n_pairs = B_HEADS // 2

    def col_spec(col0):
        return pl.BlockSpec((1, seq, LANES), lambda b, j: (b, 0, col0 + j))

    return pl.pallas_call(
        functools.partial(_na_kernel, rows=rows),
        grid=(bsz, n_pairs),
        in_specs=[
            col_spec(COL_QB), col_spec(COL_KB), col_spec(COL_VB),
            pl.BlockSpec((LANES, LANES), lambda b, j: (0, j)),
            pl.BlockSpec((LANES, LANES), lambda b, j: (0, j)),
            pl.BlockSpec((2, NA_ROWS, GRID_W, NA_ROWS * GRID_W), lambda b, j: (j, 0, 0, 0)),
        ],
        out_specs=pl.BlockSpec((1, seq, LANES), lambda b, j: (b, 0, j)),
        out_shape=jax.ShapeDtypeStruct((bsz, seq, B_WIDTH), BF16),
        compiler_params=_cparams(("parallel", "parallel")),
        name="neighbourhood_attention",
    )(u, u, u, kmeta, vmeta, bias)


def _route(scores, rbias):
    t = scores.shape[-1]
    ninf = -jnp.inf
    biased = scores + rbias
    b3 = biased.reshape(N_GROUPS, GROUP_SIZE, t)
    s3 = scores.reshape(N_GROUPS, GROUP_SIZE, t)
    io_in = lax.broadcasted_iota(jnp.int32, b3.shape, 1)
    io_g3 = lax.broadcasted_iota(jnp.int32, b3.shape, 0)
    io_e = io_g3 * GROUP_SIZE + io_in
    m1 = jnp.max(b3, axis=1, keepdims=True)
    i1 = jnp.min(jnp.where(b3 == m1, io_in, GROUP_SIZE), axis=1, keepdims=True)
    m2 = jnp.max(jnp.where(io_in == i1, ninf, b3), axis=1, keepdims=True)
    gs = m1 + m2
    io_g = lax.broadcasted_iota(jnp.int32, gs.shape, 0)
    gmask = jnp.zeros(gs.shape, jnp.bool_)
    cur = gs
    for _ in range(TOPK_GROUPS):
        m = jnp.max(cur, axis=0, keepdims=True)
        i = jnp.min(jnp.where(cur == m, io_g, N_GROUPS), axis=0, keepdims=True)
        pick = io_g == i
        gmask = jnp.logical_or(gmask, pick)
        cur = jnp.where(pick, ninf, cur)
    cur = jnp.where(gmask, b3, ninf)
    picks, top_e, top_s = [], [], []
    for _ in range(TOP_K):
        m = jnp.max(jnp.max(cur, axis=1, keepdims=True), axis=0, keepdims=True)
        i = jnp.min(jnp.min(jnp.where(cur == m, io_e, N_EXPERTS), axis=1, keepdims=True),
                    axis=0, keepdims=True)
        pick = io_e == i
        picks.append(pick)
        top_e.append(i.reshape(1, t))
        w = jnp.sum(jnp.sum(jnp.where(pick, s3, 0.0), axis=1, keepdims=True), axis=0, keepdims=True)
        top_s.append(w.reshape(1, t))
        cur = jnp.where(pick, ninf, cur)
    top_e = jnp.concatenate(top_e, axis=0)
    top_s = jnp.concatenate(top_s, axis=0)
    denom = top_s[0:1]
    for k in range(1, TOP_K):
        denom = denom + top_s[k:k + 1]
    top_w = top_s / (denom + 1e-20) * ROUTED_SCALE
    sel = picks[0]
    for k in range(1, TOP_K):
        sel = jnp.logical_or(sel, picks[k])
    sel = jnp.where(sel, 1.0, 0.0).astype(F32).reshape(N_EXPERTS, t)
    return top_e, top_w, sel, picks


def _outproj_router_kernel(ya_ref, yb_ref, x_ref, wo_ref, ga_ref, gb_ref, eg_ref, eb_ref,
                           g1_ref, b1_ref, wrh_ref, wrl_ref, rb_ref,
                           h1_ref, te_ref, tw_ref, tr_ref, cnt_ref, carry):
    sub = ROUTE_TILE
    parts = [pl.ds(j * sub, sub) for j in range(x_ref.shape[0] // sub)]

    @pl.when(pl.program_id(0) == 0)
    def _():
        carry[...] = jnp.zeros_like(carry)

    def rms(y_ref, g_ref, rows):
        y = y_ref[rows, :].astype(F32)
        inv = lax.rsqrt(jnp.mean(y * y, axis=-1, keepdims=True) + RMS_EPS)
        return (y * inv * g_ref[...]).astype(BF16)

    normed = [jnp.concatenate([rms(ya_ref, ga_ref, r), rms(yb_ref, gb_ref, r)], axis=1)
              for r in parts]
    mixes = [_dot(n, wo_ref[...]) for n in normed]
    logits = []
    for rows, mix in zip(parts, mixes):
        h = _layer_norm(x_ref[rows, :], eg_ref[...], eb_ref[...])
        h1 = _layer_norm(DEEPNORM_ALPHA * h + mix, g1_ref[...], b1_ref[...])
        h1_ref[rows, :] = h1
        hb = h1.astype(BF16)
        hlo = (h1 - hb.astype(F32)).astype(BF16)
        logits.append(_dot_nt(wrh_ref[...], hb) + _dot_nt(wrh_ref[...], hlo)
                      + _dot_nt(wrl_ref[...], hb))

    row = lax.broadcasted_iota(jnp.int32, (sub, sub), 0)
    col = lax.broadcasted_iota(jnp.int32, (sub, sub), 1)
    before = jnp.where(row < col, 1.0, 0.0).astype(BF16)
    running = carry[...]
    for j, lg in enumerate(logits):
        scores = 1.0 / (1.0 + jnp.exp(-lg))
        top_e, top_w, sel, picks = _route(scores, rb_ref[...])
        rank = _dot(sel.astype(BF16), before) + running
        rank3 = rank.reshape(N_GROUPS, GROUP_SIZE, sub)
        ranks = []
        for k in range(TOP_K):
            rk = jnp.sum(jnp.sum(jnp.where(picks[k], rank3, 0.0), axis=1, keepdims=True),
                         axis=0, keepdims=True)
            ranks.append(rk.reshape(1, sub))
        cols = pl.ds(j * sub, sub)
        te_ref[:, cols] = top_e
        tw_ref[:, cols] = top_w
        tr_ref[:, cols] = jnp.concatenate(ranks, axis=0).astype(jnp.int32)
        running = running + jnp.sum(sel, axis=-1, keepdims=True)
    carry[...] = running
    cnt_ref[...] = jnp.broadcast_to(running, cnt_ref.shape)


def _outproj_router(ya, yb, x2, wo_bf16, ga, gb, eg, eb, g1, b1, wrh, wrl, rb, tm):
    m = x2.shape[0]
    row = lambda w: pl.BlockSpec((tm, w), lambda i: (i, 0))
    full = lambda a: pl.BlockSpec(a.shape, lambda i: (0,) * a.ndim)
    tok = pl.BlockSpec((TOP_K, tm), lambda i: (0, i))
    return pl.pallas_call(
        _outproj_router_kernel,
        grid=(m // tm,),
        in_specs=[row(A_WIDTH), row(B_WIDTH), row(D_MODEL), full(wo_bf16), full(ga), full(gb),
                  full(eg), full(eb), full(g1), full(b1), full(wrh), full(wrl), full(rb)],
        out_specs=[row(D_MODEL), tok, tok, tok,
                   pl.BlockSpec((N_EXPERTS, LANES), lambda i: (0, 0))],
        out_shape=[jax.ShapeDtypeStruct((m, D_MODEL), F32),
                   jax.ShapeDtypeStruct((TOP_K, m), jnp.int32),
                   jax.ShapeDtypeStruct((TOP_K, m), F32),
                   jax.ShapeDtypeStruct((TOP_K, m), jnp.int32),
                   jax.ShapeDtypeStruct((N_EXPERTS, LANES), F32)],
        scratch_shapes=[pltpu.VMEM((N_EXPERTS, 1), F32)],
        compiler_params=_cparams(("arbitrary",)),
        name="outproj_router",
    )(ya, yb, x2, wo_bf16, ga, gb, eg, eb, g1, b1, wrh, wrl, rb)


def _zero_fill_padding(pad_start_ref, pad_len_ref, n_used, xs_hbm, zeros, sem):
    n_blocks = xs_hbm.shape[0] // MOE_BLOCK
    zeros[...] = jnp.zeros_like(zeros)

    def copies(act):
        def per_expert(e, carry):
            start = pad_start_ref[e]
            n = pad_len_ref[e]
            head = jnp.minimum((-start) & (SUBLANES - 1), n)
            for j in range(SUBLANES - 1):
                @pl.when(j < head)
                def _(j=j):
                    act(pltpu.make_async_copy(zeros.at[pl.ds(0, 1), :],
                                              xs_hbm.at[pl.ds(start + j, 1), :], sem.at[1]))
            aligned = start + head
            rest = n - head
            bit = MOE_BLOCK // 2
            while bit >= SUBLANES:
                @pl.when((rest & bit) != 0)
                def _(bit=bit):
                    off = pl.multiple_of(aligned + (rest & (-2 * bit)), SUBLANES)
                    act(pltpu.make_async_copy(zeros.at[pl.ds(0, bit), :],
                                              xs_hbm.at[pl.ds(off, bit), :], sem.at[1]))
                bit //= 2
            return carry

        def per_block(j, carry):
            off = pl.multiple_of(j * MOE_BLOCK, MOE_BLOCK)
            act(pltpu.make_async_copy(zeros, xs_hbm.at[pl.ds(off, MOE_BLOCK), :], sem.at[1]))
            return carry

        lax.fori_loop(0, N_EXPERTS, per_expert, 0)
        lax.fori_loop(n_used, n_blocks, per_block, 0)

    copies(lambda cp: cp.start())
    copies(lambda cp: cp.wait())


def _dispatch_kernel(pad_start_ref, pad_len_ref, nu_ref, dest_ref, h1_ref, wsg_ref, wsu_ref,
                     wsd_ref, home0_ref, wg_ref, wu_ref, wd_ref,
                     xs_hbm, sh_ref, home_ref, wgb_ref, wub_ref, wdb_ref,
                     packed, zeros, sem, *, m):
    i = pl.program_id(0)
    tm = h1_ref.shape[0]
    lane = lax.broadcasted_iota(jnp.int32, (1, LANES), 1)

    @pl.when(i == 0)
    def _():
        home_ref[...] = home0_ref[...]
        _zero_fill_padding(pad_start_ref, pad_len_ref, nu_ref[0], xs_hbm, zeros, sem)

    wgb_ref[...] = wg_ref[...].astype(BF16)
    wub_ref[...] = wu_ref[...].astype(BF16)
    wdb_ref[...] = wd_ref[...].astype(BF16)

    h1 = h1_ref[...]
    packed[...] = _pack_rows(h1)
    for t in range(tm):
        for k in range(TOP_K):
            slot = dest_ref[k, t]
            pltpu.store(home_ref.at[pl.ds(lax.shift_right_logical(slot, LANE_BITS), 1), :],
                        jnp.full((1, LANES), k * m + i * tm + t, jnp.int32),
                        mask=lane == (slot & (LANES - 1)))
            pltpu.make_async_copy(packed.at[pl.ds(t, 1), :],
                                  xs_hbm.at[pl.ds(slot, 1), :], sem.at[0]).start(priority=k % 2)
    hb = h1.astype(BF16)
    g = _dot(hb, wsg_ref[...])
    u = _dot(hb, wsu_ref[...])
    sh_ref[...] = _dot((_silu(g) * u).astype(BF16), wsd_ref[...]).astype(sh_ref.dtype)
    for k in range(TOP_K):
        pltpu.make_async_copy(packed, xs_hbm.at[pl.ds(0, tm), :], sem.at[0]).wait()


def _dispatch_shared(pad_start, pad_len, n_used, dest, h1, wsg, wsu, wsd, home0,
                     w_gate, w_up, w_down):
    m = h1.shape[0]
    tm = DISPATCH_TILE
    p_rows = home0.size
    full = lambda a: pl.BlockSpec(a.shape, lambda i, *_: (0,) * a.ndim)
    steps = m // tm
    bands = max(1, steps // N_EXPERTS)
    per_step = max(1, N_EXPERTS // steps)
    assert steps * per_step == N_EXPERTS * bands
    share = lambda a: pl.BlockSpec((per_step, a.shape[1] // bands, a.shape[2]),
                                   lambda i, *_: (i // bands, i % bands, 0))
    as_bf16 = lambda a: jax.ShapeDtypeStruct(a.shape, BF16)
    return pl.pallas_call(
        functools.partial(_dispatch_kernel, m=m),
        grid_spec=pltpu.PrefetchScalarGridSpec(
            num_scalar_prefetch=3,
            grid=(m // tm,),
            in_specs=[pl.BlockSpec((TOP_K, tm), lambda i, *_: (0, i), memory_space=pltpu.SMEM),
                      pl.BlockSpec((tm, D_MODEL), lambda i, *_: (i, 0)),
                      full(wsg), full(wsu), full(wsd), full(home0),
                      share(w_gate), share(w_up), share(w_down)],
            out_specs=[pl.BlockSpec(memory_space=pl.ANY),
                       pl.BlockSpec((tm, D_MODEL), lambda i, *_: (i, 0)),
                       full(home0), share(w_gate), share(w_up), share(w_down)],
            scratch_shapes=[pltpu.VMEM((tm, D_HALF), U32),
                            pltpu.VMEM((MOE_BLOCK, D_HALF), U32),
                            pltpu.SemaphoreType.DMA((2,))],
        ),
        out_shape=[jax.ShapeDtypeStruct((p_rows, D_HALF), U32),
                   jax.ShapeDtypeStruct((m, D_MODEL), BF16),
                   jax.ShapeDtypeStruct(home0.shape, jnp.int32),
                   as_bf16(w_gate), as_bf16(w_up), as_bf16(w_down)],
        compiler_params=pltpu.CompilerParams(
            dimension_semantics=("arbitrary",), vmem_limit_bytes=VMEM_LIMIT,
            disable_bounds_checks=True),
        name="dispatch_shared",
    )(pad_start, pad_len, n_used, dest, h1, wsg, wsu, wsd, home0, w_gate, w_up, w_down)


def _moe_kernel(be_ref, nu_ref, home_prev_ref, x_ref, wg_ref, wu_ref, wd_ref, y_hbm,
                ybuf0, ybuf1, sem, wg_b, wu_b, wd_b):
    i = pl.program_id(0)
    n_used = nu_ref[0]
    e = be_ref[i]
    prev = be_ref[jnp.maximum(i - 1, 0)]
    even = i % 2 == 0
    bufs = (ybuf0, ybuf1)
    spare0 = y_hbm.shape[0] - 2 * MOE_BLOCK

    @pl.when(i == 0)
    def _():
        ybuf1[...] = jnp.zeros_like(ybuf1)
        for b in range(2):
            cp = pltpu.make_async_copy(
                ybuf1, y_hbm.at[pl.ds(spare0 + b * MOE_BLOCK, MOE_BLOCK), :], sem.at[1])
            cp.start()
            cp.wait()

    @pl.when(jnp.logical_and(i < n_used, jnp.logical_or(i == 0, e != prev)))
    def _():
        wg_b[...] = wg_ref[0]
        wu_b[...] = wu_ref[0]
        wd_b[...] = wd_ref[0]

    def scatter_rows(b):
        for r in range(MOE_BLOCK):
            pltpu.make_async_copy(bufs[b].at[pl.ds(r, 1), :],
                                  y_hbm.at[pl.ds(home_prev_ref[0, 0, r], 1), :],
                                  sem.at[b]).start()

    def wait_rows(b):
        pltpu.make_async_copy(bufs[b], y_hbm.at[pl.ds(0, MOE_BLOCK), :], sem.at[b]).wait()

    def compute(b):
        hi, lo = _unpack_rows(x_ref[...])
        g = _dot(hi, wg_b[0:D_HALF, :]) + _dot(lo, wg_b[D_HALF:D_MODEL, :])
        u = _dot(hi, wu_b[0:D_HALF, :]) + _dot(lo, wu_b[D_HALF:D_MODEL, :])
        bufs[b][...] = _pack_rows(_dot((_silu(g) * u).astype(BF16), wd_b[...]))

    for b in range(2):
        mine = even if b == 0 else jnp.logical_not(even)

        @pl.when(mine & (i >= 2) & (i <= n_used + 1))
        def _():
            wait_rows(b)

        @pl.when(mine & (i >= 1) & (i < n_used))
        def _():
            scatter_rows(1 - b)
            compute(b)

        @pl.when(mine & (i >= 1) & (i == n_used))
        def _():
            scatter_rows(1 - b)

    @pl.when((i == 0) & (n_used > 0))
    def _():
        compute(0)


def _moe_experts(block_e, n_used, home, x_sorted, w_gate, w_up, w_down, m):
    n_blocks = x_sorted.shape[0] // MOE_BLOCK
    assert block_e.shape[0] == n_blocks + 2
    used = lambda i, be, nu: (jnp.minimum(i, nu[0] - 1), 0)
    weights = lambda shape: pl.BlockSpec(
        (1,) + shape, lambda i, be, nu: (be[jnp.minimum(i, nu[0] - 1)], 0, 0))
    return pl.pallas_call(
        _moe_kernel,
        grid_spec=pltpu.PrefetchScalarGridSpec(
            num_scalar_prefetch=2,
            grid=(n_blocks + 2,),
            in_specs=[
                pl.BlockSpec((1, 1, MOE_BLOCK),
                             lambda i, be, nu: (jnp.clip(i - 1, 0, n_blocks - 1), 0, 0),
                             memory_space=pltpu.SMEM),
                pl.BlockSpec((MOE_BLOCK, D_HALF), used),
                weights((D_MODEL, D_EXPERT)), weights((D_MODEL, D_EXPERT)),
                weights((D_EXPERT, D_MODEL)),
            ],
            out_specs=pl.BlockSpec(memory_space=pl.ANY),
            scratch_shapes=[pltpu.VMEM((MOE_BLOCK, D_HALF), U32),
                            pltpu.VMEM((MOE_BLOCK, D_HALF), U32),
                            pltpu.SemaphoreType.DMA((2,)),
                            pltpu.VMEM((D_MODEL, D_EXPERT), BF16),
                            pltpu.VMEM((D_MODEL, D_EXPERT), BF16),
                            pltpu.VMEM((D_EXPERT, D_MODEL), BF16)],
        ),
        out_shape=jax.ShapeDtypeStruct((TOP_K * m + 2 * MOE_BLOCK, D_HALF), U32),
        compiler_params=pltpu.CompilerParams(
            dimension_semantics=("arbitrary",), vmem_limit_bytes=VMEM_LIMIT,
            disable_bounds_checks=True),
        name="moe_experts",
    )(block_e, n_used, home.reshape(n_blocks, 1, MOE_BLOCK), x_sorted, w_gate, w_up, w_down)


def _final_kernel(h1_ref, sh_ref, *rest):
    ys_refs = rest[:TOP_K]
    tw_ref, g2_ref, b2_ref, o_ref = rest[TOP_K:]
    sh = sh_ref[...].astype(F32)
    left, right = sh[:, :D_HALF], sh[:, D_HALF:]
    for k in range(TOP_K):
        p = ys_refs[k][...]
        w = tw_ref[:, k:k + 1]
        left = left + lax.bitcast_convert_type(p & jnp.uint32(0xFFFF0000), F32) * w
        right = right + lax.bitcast_convert_type(p << 16, F32) * w
    ffn = jnp.concatenate([left, right], axis=1)
    o_ref[...] = _layer_norm(DEEPNORM_ALPHA * h1_ref[...] + ffn, g2_ref[...], b2_ref[...])


def _final(h1, sh, y_home, tw_t, g2, b2, tm):
    m = h1.shape[0]
    row = lambda w: pl.BlockSpec((tm, w), lambda i: (i, 0))
    full = lambda a: pl.BlockSpec(a.shape, lambda i: (0,) * a.ndim)
    choice = lambda k: pl.BlockSpec((tm, D_HALF), lambda i: (k * (m // tm) + i, 0))
    return pl.pallas_call(
        _final_kernel,
        grid=(m // tm,),
        in_specs=[row(D_MODEL), row(D_MODEL)] + [choice(k) for k in range(TOP_K)]
                 + [row(TOP_K), full(g2), full(b2)],
        out_specs=row(D_MODEL),
        out_shape=jax.ShapeDtypeStruct((m, D_MODEL), F32),
        compiler_params=_cparams(("parallel",)),
        name="combine_ln2",
    )(h1, sh, *([y_home] * TOP_K), tw_t, g2, b2)


def _pad_rows(a, rows):
    return jnp.concatenate([a, jnp.zeros((rows - a.shape[0],) + a.shape[1:], a.dtype)], axis=0)


def kernel(x, meta_tokens, ln_emb_g, ln_emb_b, t5_table, w_in, a_sink, na_rpb, g_norm_a, g_norm_b, w_out, ln1_g, ln1_b, w_router, router_bias, w_gate, w_up, w_down, ws_gate, ws_up, ws_down, ln2_g, ln2_b):
    bsz, seq, _ = x.shape
    m = bsz * seq
    r2 = lambda a: a.reshape(1, -1).astype(F32)
    x2 = x.reshape(m, D_MODEL)
    eg, eb = r2(ln_emb_g), r2(ln_emb_b)

    w_in_b = w_in[0].astype(BF16)
    tm = 1024 if m % 1024 == 0 else 128
    u = _ln_inproj(x2, eg, eb, w_in_b, tm, 1152).reshape(bsz, seq, IN_WIDTH)
    um = _ln_inproj(meta_tokens.astype(F32), eg, eb, w_in_b, N_META, 1152)
    cut = lambda c0, width: _pad_rows(um[:, c0 * LANES:c0 * LANES + width], LANES)

    ya = _window_attention(u, cut(COL_KA, KV_WIDTH), cut(COL_VA, KV_WIDTH),
                           _window_bias(t5_table, a_sink[0], seq), bsz, seq)
    yb = _neighbourhood_attention(u, cut(COL_KB, B_WIDTH), cut(COL_VB, B_WIDTH),
                                  _na_bias(na_rpb[0]), bsz, seq)

    wr_t = w_router[0].astype(F32).T
    wr_hi = wr_t.astype(BF16)
    wr_lo = (wr_t - wr_hi.astype(F32)).astype(BF16)
    tm4 = 256 if m % 256 == 0 else 128
    h1, top_e, top_w, top_r, cnt = _outproj_router(
        ya.reshape(m, A_WIDTH), yb.reshape(m, B_WIDTH), x2, w_out[0].astype(BF16),
        r2(g_norm_a), r2(g_norm_b), eg, eb, r2(ln1_g), r2(ln1_b), wr_hi, wr_lo,
        router_bias[0].astype(F32).reshape(N_EXPERTS, 1),
        2 * ROUTE_TILE if m % (2 * ROUTE_TILE) == 0 else ROUTE_TILE)

    counts = cnt[:, 0].astype(jnp.int32)
    nb_e = (counts + MOE_BLOCK - 1) // MOE_BLOCK
    bend = jnp.cumsum(nb_e)
    pstart = (bend - nb_e) * MOE_BLOCK
    n_blocks = (m * TOP_K) // MOE_BLOCK + N_EXPERTS
    p_rows = n_blocks * MOE_BLOCK
    blk = jnp.arange(n_blocks + 2, dtype=jnp.int32)
    block_e = jnp.minimum(jnp.sum((bend[None, :] <= blk[:, None]).astype(jnp.int32), axis=1),
                          N_EXPERTS - 1)
    n_used = bend[-1:].astype(jnp.int32)
    expert_ids = jnp.arange(N_EXPERTS, dtype=jnp.int32)[:, None, None]
    dest = top_r + jnp.sum(jnp.where(top_e[None] == expert_ids, pstart[:, None, None], 0),
                           axis=0)
    slot = jnp.arange(p_rows, dtype=jnp.int32)
    home0 = (TOP_K * m + (slot // MOE_BLOCK % 2) * MOE_BLOCK + slot % MOE_BLOCK
             ).reshape(p_rows // LANES, LANES)

    x_sorted, shared, home, wg_b, wu_b, wd_b = _dispatch_shared(
        pstart + counts, nb_e * MOE_BLOCK - counts, n_used, dest, h1,
        ws_gate[0].astype(BF16), ws_up[0].astype(BF16), ws_down[0].astype(BF16), home0,
        w_gate[0], w_up[0], w_down[0])
    y_home = _moe_experts(block_e, n_used, home, x_sorted, wg_b, wu_b, wd_b, m)

    out = _final(h1, shared, y_home, top_w.T, r2(ln2_g), r2(ln2_b), tm4)
    return out.reshape(bsz, seq, D_MODEL)
```
